```python
import jax, jax.numpy as jnp
from jax import lax
import numpy as np

D_MODEL = 1024
BATCH = 8
SEQ = 8192
DEPTH = 4

N_META = 16
N_HEADS = 16
HEAD_DIM = D_MODEL // N_HEADS
D_FF = 4 * D_MODEL
CONV_WIDTH = 3
BLOCK_Q = 128
N_A_LAYERS = DEPTH // 2
N_B_LAYERS = DEPTH - N_A_LAYERS
N_NORMS = 4
RMS_EPS = 1e-6
NORM_NOISE = 0.02

kernel_name = "yoco_shortconv_stickbreaking_hybrid"


def rms_norm(x, g):
    xf = x.astype(jnp.float32)
    y = xf * lax.rsqrt(jnp.mean(xf * xf, axis=-1, keepdims=True) + RMS_EPS)
    return (y * g.astype(jnp.float32)).astype(x.dtype)


def short_conv_mixer(h, w_in, conv_w, w_out):
    L = h.shape[1]
    b_gate, c_gate, xv = jnp.split(h @ w_in, 3, axis=-1)
    u = c_gate * xv
    u_pad = jnp.pad(u, ((0, 0), (CONV_WIDTH - 1, 0), (0, 0)))
    conv = u_pad[:, 0:L] * conv_w[0]
    for tap in range(1, CONV_WIDTH):
        conv = conv + u_pad[:, tap:tap + L] * conv_w[tap]
    return (b_gate * conv) @ w_out


def squared_relu_mlp(h, w1, w2):
    return jnp.square(jax.nn.relu(h @ w1)) @ w2


def stick_breaking_attention(q, k, v):
    L = q.shape[2]
    pad = (-N_META) % BLOCK_Q
    widths = ((0, 0), (0, 0), (pad, 0), (0, 0))
    qp, kp, vp = jnp.pad(q, widths), jnp.pad(k, widths), jnp.pad(v, widths)
    n_blocks = (L + pad) // BLOCK_Q
    scale = HEAD_DIM ** -0.5
    outs = []
    for i in range(n_blocks):
        q_lo = i * BLOCK_Q
        kv_hi = q_lo + BLOCK_Q
        qb = qp[:, :, q_lo:kv_hi]
        kb = kp[:, :, :kv_hi]
        vb = vp[:, :, :kv_hi]
        z = jnp.einsum('bhqd,bhkd->bhqk', qb, kb).astype(jnp.float32) * scale
        q_pos = q_lo + jnp.arange(BLOCK_Q)[:, None]
        k_pos = jnp.arange(kv_hi)[None, :]
        visible = (k_pos < q_pos) & (k_pos >= pad)
        log_beta = jax.nn.log_sigmoid(z)
        log_one_minus = jnp.where(visible, jax.nn.log_sigmoid(-z), 0.0)
        rev = lax.cumsum(log_one_minus, axis=3, reverse=True)
        after = jnp.pad(rev[..., 1:], ((0, 0), (0, 0), (0, 0), (0, 1)))
        a = jnp.where(visible, jnp.exp(log_beta + after), 0.0)
        outs.append(jnp.einsum('bhqk,bhkd->bhqd', a.astype(vb.dtype), vb))
    return jnp.concatenate(outs, axis=2)[:, :, pad:]


def _fwd_setup_inputs(seed: int = 0) -> dict:
    key = jax.random.key(seed)
    ks = jax.random.split(key, 14)
    nA, nB = N_A_LAYERS, N_B_LAYERS
    f32 = jnp.float32
    dm = D_MODEL ** -0.5
    return {
        "x": jax.random.normal(ks[0], (BATCH, SEQ, D_MODEL), f32),
        "meta_tokens": jax.random.normal(ks[1], (N_META, D_MODEL), f32),
        "norm_gains": 1.0 + NORM_NOISE * jax.random.normal(ks[2], (DEPTH, N_NORMS, D_MODEL), f32),
        "conv_in_proj": jax.random.normal(ks[3], (nA, D_MODEL, 3 * D_MODEL), f32) * dm,
        "conv_w": jax.random.normal(ks[4], (nA, CONV_WIDTH, D_MODEL), f32) * CONV_WIDTH ** -0.5,
        "conv_out_proj": jax.random.normal(ks[5], (nA, D_MODEL, D_MODEL), f32) * dm,
        "kv_norm": 1.0 + NORM_NOISE * jax.random.normal(ks[6], (D_MODEL,), f32),
        "w_k": jax.random.normal(ks[7], (D_MODEL, D_MODEL), f32) * dm,
        "w_v": jax.random.normal(ks[8], (D_MODEL, D_MODEL), f32) * dm,
        "w_q": jax.random.normal(ks[9], (nB, D_MODEL, D_MODEL), f32) * dm,
        "w_o": jax.random.normal(ks[10], (nB, D_MODEL, D_MODEL), f32) * dm,
        "mlp_w1": jax.random.normal(ks[11], (DEPTH, D_MODEL, D_FF), f32) * dm,
        "mlp_w2": jax.random.normal(ks[12], (DEPTH, D_FF, D_MODEL), f32) * D_FF ** -0.5,
    }


def _fwd_reference(x, meta_tokens, norm_gains, conv_in_proj, conv_w, conv_out_proj,
              kv_norm, w_k, w_v, w_q, w_o, mlp_w1, mlp_w2):
    bsz = x.shape[0]
    meta = jnp.broadcast_to(meta_tokens[None].astype(x.dtype), (bsz, N_META, D_MODEL))
    h = jnp.concatenate([meta, x], axis=1)
    L = h.shape[1]

    def split_heads(t):
        return t.reshape(bsz, L, N_HEADS, HEAD_DIM).transpose(0, 2, 1, 3)

    k_shared = v_shared = None
    for layer in range(DEPTH):
        g = norm_gains[layer]
        if layer < N_A_LAYERS:
            mix = short_conv_mixer(rms_norm(h, g[0]), conv_in_proj[layer], conv_w[layer],
                                   conv_out_proj[layer])
        else:
            if layer == N_A_LAYERS:
                hk = rms_norm(h, kv_norm)
                k_shared = split_heads(hk @ w_k)
                v_shared = split_heads(hk @ w_v)
            j = layer - N_A_LAYERS
            q = split_heads(rms_norm(h, g[0]) @ w_q[j])
            o = stick_breaking_attention(q, k_shared, v_shared)
            mix = o.transpose(0, 2, 1, 3).reshape(bsz, L, D_MODEL) @ w_o[j]
        h = h + rms_norm(mix, g[1])
        ff = squared_relu_mlp(rms_norm(h, g[2]), mlp_w1[layer], mlp_w2[layer])
        h = h + rms_norm(ff, g[3])
    return h[:, N_META:]


import jax as _jax
import jax.numpy as _jnp

TWIN_FORMAT = 'train_step'
FWD_PARAMS = ['x', 'meta_tokens', 'norm_gains', 'conv_in_proj', 'conv_w', 'conv_out_proj', 'kv_norm', 'w_k', 'w_v', 'w_q', 'w_o', 'mlp_w1', 'mlp_w2']
TWIN_WEIGHTS = ['meta_tokens', 'norm_gains', 'conv_in_proj', 'conv_w', 'conv_out_proj', 'kv_norm', 'w_k', 'w_v', 'w_q', 'w_o', 'mlp_w1', 'mlp_w2']
TWIN_DIFF_INPUT = 'x'
TWIN_INPUTS = ['x', 'meta_tokens', 'norm_gains', 'conv_in_proj', 'conv_w', 'conv_out_proj', 'kv_norm', 'w_k', 'w_v', 'w_q', 'w_o', 'mlp_w1', 'mlp_w2', 'loss_target', 'm_meta_tokens', 'm_norm_gains', 'm_conv_in_proj', 'm_conv_w', 'm_conv_out_proj', 'm_kv_norm', 'm_w_k', 'm_w_v', 'm_w_q', 'm_w_o', 'm_mlp_w1', 'm_mlp_w2', 'v_meta_tokens', 'v_norm_gains', 'v_conv_in_proj', 'v_conv_w', 'v_conv_out_proj', 'v_kv_norm', 'v_w_k', 'v_w_v', 'v_w_q', 'v_w_o', 'v_mlp_w1', 'v_mlp_w2']
TWIN_OUTPUTS = ['loss', 'grad_x', 'grad_meta_tokens', 'grad_norm_gains', 'grad_conv_in_proj', 'grad_conv_w', 'grad_conv_out_proj', 'grad_kv_norm', 'grad_w_k', 'grad_w_v', 'grad_w_q', 'grad_w_o', 'grad_mlp_w1', 'grad_mlp_w2', 'delta_meta_tokens', 'delta_norm_gains', 'delta_conv_in_proj', 'delta_conv_w', 'delta_conv_out_proj', 'delta_kv_norm', 'delta_w_k', 'delta_w_v', 'delta_w_q', 'delta_w_o', 'delta_mlp_w1', 'delta_mlp_w2', 'new_m_meta_tokens', 'new_m_norm_gains', 'new_m_conv_in_proj', 'new_m_conv_w', 'new_m_conv_out_proj', 'new_m_kv_norm', 'new_m_w_k', 'new_m_w_v', 'new_m_w_q', 'new_m_w_o', 'new_m_mlp_w1', 'new_m_mlp_w2', 'new_v_meta_tokens', 'new_v_norm_gains', 'new_v_conv_in_proj', 'new_v_conv_w', 'new_v_conv_out_proj', 'new_v_kv_norm', 'new_v_w_k', 'new_v_w_v', 'new_v_w_q', 'new_v_w_o', 'new_v_mlp_w1', 'new_v_mlp_w2']
TWIN_LEAF_KINDS = {'loss': 'loss', 'grad_x': 'grad_x', 'grad_meta_tokens': 'grad_w', 'grad_norm_gains': 'grad_w', 'grad_conv_in_proj': 'grad_w', 'grad_conv_w': 'grad_w', 'grad_conv_out_proj': 'grad_w', 'grad_kv_norm': 'grad_w', 'grad_w_k': 'grad_w', 'grad_w_v': 'grad_w', 'grad_w_q': 'grad_w', 'grad_w_o': 'grad_w', 'grad_mlp_w1': 'grad_w', 'grad_mlp_w2': 'grad_w', 'delta_meta_tokens': 'delta_w', 'delta_norm_gains': 'delta_w', 'delta_conv_in_proj': 'delta_w', 'delta_conv_w': 'delta_w', 'delta_conv_out_proj': 'delta_w', 'delta_kv_norm': 'delta_w', 'delta_w_k': 'delta_w', 'delta_w_v': 'delta_w', 'delta_w_q': 'delta_w', 'delta_w_o': 'delta_w', 'delta_mlp_w1': 'delta_w', 'delta_mlp_w2': 'delta_w', 'new_m_meta_tokens': 'new_m', 'new_m_norm_gains': 'new_m', 'new_m_conv_in_proj': 'new_m', 'new_m_conv_w': 'new_m', 'new_m_conv_out_proj': 'new_m', 'new_m_kv_norm': 'new_m', 'new_m_w_k': 'new_m', 'new_m_w_v': 'new_m', 'new_m_w_q': 'new_m', 'new_m_w_o': 'new_m', 'new_m_mlp_w1': 'new_m', 'new_m_mlp_w2': 'new_m', 'new_v_meta_tokens': 'new_v', 'new_v_norm_gains': 'new_v', 'new_v_conv_in_proj': 'new_v', 'new_v_conv_w': 'new_v', 'new_v_conv_out_proj': 'new_v', 'new_v_kv_norm': 'new_v', 'new_v_w_k': 'new_v', 'new_v_w_v': 'new_v', 'new_v_w_q': 'new_v', 'new_v_w_o': 'new_v', 'new_v_mlp_w1': 'new_v', 'new_v_mlp_w2': 'new_v'}


def _forward(args):
    return _fwd_reference(*[args[k] for k in FWD_PARAMS])


def _output_shape():
    def fwd():
        inp = _fwd_setup_inputs(0)
        return _fwd_reference(*[inp[k] for k in FWD_PARAMS])
    out = _jax.eval_shape(fwd)
    return out.shape, out.dtype

N_MICROBATCH = 1
ADAM_LR = 0.001
ADAM_B1 = 0.9
ADAM_B2 = 0.999
ADAM_EPS = 1e-08
ADAM_WD = 0.01
ADAM_STEP = 10
PER_EXAMPLE_BATCH_AXIS = {'x': 0, 'loss_target': 0}
SHARED_INPUTS = []
_WEIGHT_DTYPES = {'meta_tokens': _jnp.float32, 'norm_gains': _jnp.float32, 'conv_in_proj': _jnp.float32, 'conv_w': _jnp.float32, 'conv_out_proj': _jnp.float32, 'kv_norm': _jnp.float32, 'w_k': _jnp.float32, 'w_v': _jnp.float32, 'w_q': _jnp.float32, 'w_o': _jnp.float32, 'mlp_w1': _jnp.float32, 'mlp_w2': _jnp.float32}
MOMENT_SCALE = {'meta_tokens': 1.045119e-01, 'norm_gains': 5.046580e+01, 'conv_in_proj': 2.021657e+00, 'conv_w': 2.373689e+00, 'conv_out_proj': 2.671990e+00, 'kv_norm': 3.482721e+01, 'w_k': 6.395727e-01, 'w_v': 3.602305e+01, 'w_q': 4.543863e-01, 'w_o': 2.516914e+01, 'mlp_w1': 5.426829e+00, 'mlp_w2': 3.441319e+01}


def _to_microbatches(a, axis):
    t = _jnp.moveaxis(a, axis, 0)
    t = t.reshape((N_MICROBATCH, t.shape[0] // N_MICROBATCH) + t.shape[1:])
    return _jnp.moveaxis(t, 1, axis + 1)


def setup_inputs(seed: int = 0) -> dict:
    inp = _fwd_setup_inputs(seed)
    key = _jax.random.fold_in(_jax.random.key(seed), 7919)
    shape, _ = _output_shape()
    out = dict(inp)
    out["loss_target"] = _jax.random.normal(_jax.random.fold_in(key, 0), shape, _jnp.float32)
    for i, name in enumerate(TWIN_WEIGHTS):
        w = inp[name].astype(_jnp.float32)
        if MOMENT_SCALE is None:
            s = _jnp.sqrt(_jnp.mean(_jnp.square(w)) + 1e-30)
        else:
            s = MOMENT_SCALE[name]
        km, kv = _jax.random.split(_jax.random.fold_in(key, i + 1))
        out[name] = w
        out["m_" + name] = s * _jax.random.normal(km, w.shape, _jnp.float32)
        out["v_" + name] = (s * s) * _jax.random.uniform(kv, w.shape, _jnp.float32, 0.5, 1.5)
    if N_MICROBATCH > 1:
        for name, axis in PER_EXAMPLE_BATCH_AXIS.items():
            out[name] = _to_microbatches(out[name], axis)
    return {'x': out['x'], 'meta_tokens': out['meta_tokens'], 'norm_gains': out['norm_gains'], 'conv_in_proj': out['conv_in_proj'], 'conv_w': out['conv_w'], 'conv_out_proj': out['conv_out_proj'], 'kv_norm': out['kv_norm'], 'w_k': out['w_k'], 'w_v': out['w_v'], 'w_q': out['w_q'], 'w_o': out['w_o'], 'mlp_w1': out['mlp_w1'], 'mlp_w2': out['mlp_w2'], 'loss_target': out['loss_target'], 'm_meta_tokens': out['m_meta_tokens'], 'm_norm_gains': out['m_norm_gains'], 'm_conv_in_proj': out['m_conv_in_proj'], 'm_conv_w': out['m_conv_w'], 'm_conv_out_proj': out['m_conv_out_proj'], 'm_kv_norm': out['m_kv_norm'], 'm_w_k': out['m_w_k'], 'm_w_v': out['m_w_v'], 'm_w_q': out['m_w_q'], 'm_w_o': out['m_w_o'], 'm_mlp_w1': out['m_mlp_w1'], 'm_mlp_w2': out['m_mlp_w2'], 'v_meta_tokens': out['v_meta_tokens'], 'v_norm_gains': out['v_norm_gains'], 'v_conv_in_proj': out['v_conv_in_proj'], 'v_conv_w': out['v_conv_w'], 'v_conv_out_proj': out['v_conv_out_proj'], 'v_kv_norm': out['v_kv_norm'], 'v_w_k': out['v_w_k'], 'v_w_v': out['v_w_v'], 'v_w_q': out['v_w_q'], 'v_w_o': out['v_w_o'], 'v_mlp_w1': out['v_mlp_w1'], 'v_mlp_w2': out['v_mlp_w2']}


def _loss(weights, diff, rest, loss_target):
    with _jax.named_scope("forward"):
        args = {**rest, TWIN_DIFF_INPUT: diff, **{k: w.astype(_WEIGHT_DTYPES[k]) for k, w in weights.items()}}
        y = _forward(args)
    with _jax.named_scope("loss_head"):
        err = _jnp.square(y.astype(_jnp.float32) - loss_target)
        return 0.5 * _jnp.sum(_jnp.mean(err, axis=-1)) if err.ndim else 0.5 * err


def _adamw(w, g, m, v):
    m = ADAM_B1 * m + (1.0 - ADAM_B1) * g
    v = ADAM_B2 * v + (1.0 - ADAM_B2) * _jnp.square(g)
    m_hat = m / (1.0 - ADAM_B1 ** ADAM_STEP)
    v_hat = v / (1.0 - ADAM_B2 ** ADAM_STEP)
    delta = -ADAM_LR * (m_hat / (_jnp.sqrt(v_hat) + ADAM_EPS) + ADAM_WD * w)
    return delta, m, v


def reference(x, meta_tokens, norm_gains, conv_in_proj, conv_w, conv_out_proj, kv_norm, w_k, w_v, w_q, w_o, mlp_w1, mlp_w2, loss_target, m_meta_tokens, m_norm_gains, m_conv_in_proj, m_conv_w, m_conv_out_proj, m_kv_norm, m_w_k, m_w_v, m_w_q, m_w_o, m_mlp_w1, m_mlp_w2, v_meta_tokens, v_norm_gains, v_conv_in_proj, v_conv_w, v_conv_out_proj, v_kv_norm, v_w_k, v_w_v, v_w_q, v_w_o, v_mlp_w1, v_mlp_w2):
    given = dict(x=x, meta_tokens=meta_tokens, norm_gains=norm_gains, conv_in_proj=conv_in_proj, conv_w=conv_w, conv_out_proj=conv_out_proj, kv_norm=kv_norm, w_k=w_k, w_v=w_v, w_q=w_q, w_o=w_o, mlp_w1=mlp_w1, mlp_w2=mlp_w2, loss_target=loss_target, m_meta_tokens=m_meta_tokens, m_norm_gains=m_norm_gains, m_conv_in_proj=m_conv_in_proj, m_conv_w=m_conv_w, m_conv_out_proj=m_conv_out_proj, m_kv_norm=m_kv_norm, m_w_k=m_w_k, m_w_v=m_w_v, m_w_q=m_w_q, m_w_o=m_w_o, m_mlp_w1=m_mlp_w1, m_mlp_w2=m_mlp_w2, v_meta_tokens=v_meta_tokens, v_norm_gains=v_norm_gains, v_conv_in_proj=v_conv_in_proj, v_conv_w=v_conv_w, v_conv_out_proj=v_conv_out_proj, v_kv_norm=v_kv_norm, v_w_k=v_w_k, v_w_v=v_w_v, v_w_q=v_w_q, v_w_o=v_w_o, v_mlp_w1=v_mlp_w1, v_mlp_w2=v_mlp_w2)
    weights = {n: given[n] for n in TWIN_WEIGHTS}
    shared = {n: given[n] for n in SHARED_INPUTS}
    per_example = {n: given[n] for n in ['x']}
    grad_fn = _jax.value_and_grad(_loss, argnums=(0, 1))

    def one_microbatch(ex, loss_target):
        ex = dict(ex)
        diff = ex.pop(TWIN_DIFF_INPUT)
        return grad_fn(weights, diff, {**shared, **ex}, loss_target)

    if N_MICROBATCH == 1:
        loss, (grad_w, grad_x) = one_microbatch(per_example, given["loss_target"])
    else:
        def body(carry, xs):
            loss_sum, grad_sum = carry
            l_k, (gw_k, gx_k) = one_microbatch(xs[0], xs[1])
            with _jax.named_scope("update"):
                return (loss_sum + l_k, _jax.tree.map(_jnp.add, grad_sum, gw_k)), gx_k

        init = (_jnp.zeros((), _jnp.float32), _jax.tree.map(_jnp.zeros_like, weights))
        (loss, grad_w), grad_x = _jax.lax.scan(body, init, (per_example, given["loss_target"]))
    with _jax.named_scope("update"):
        delta_w, new_m, new_v = {}, {}, {}
        for n in TWIN_WEIGHTS:
            delta_w[n], new_m[n], new_v[n] = _adamw(weights[n], grad_w[n], given["m_" + n], given["v_" + n])
    return (loss, grad_x, *[grad_w[n] for n in TWIN_WEIGHTS], *[delta_w[n] for n in TWIN_WEIGHTS],
            *[new_m[n] for n in TWIN_WEIGHTS], *[new_v[n] for n in TWIN_WEIGHTS])
```

```python
import functools

import jax
import jax.numpy as jnp
from jax import lax
from jax.experimental import pallas as pl
from jax.experimental.pallas import tpu as pltpu

F32 = jnp.float32
BF16 = jnp.bfloat16
MESH = pl.DeviceIdType.MESH

N_DEV = 8
N_META = 16
BLK = 128
PAD = (-N_META) % BLK
HEAD_DIM = 64
DEPTH = 4
N_A = 2
RMS_EPS = 1e-6
ATTN_SCALE = HEAD_DIM ** -0.5
LOG_ZERO = -105.0
ADAM_LR, ADAM_B1, ADAM_B2, ADAM_EPS, ADAM_WD, ADAM_STEP = 0.001, 0.9, 0.999, 1e-08, 0.01, 10
VMEM_LIMIT = 56 * 2 ** 20

NT_DIMS = (((1,), (1,)), ((), ()))
TN_DIMS = (((0,), (0,)), ((), ()))


def _params(*sem):
    return pltpu.CompilerParams(dimension_semantics=sem, vmem_limit_bytes=VMEM_LIMIT)


def _row_tile(rows, pref):
    best = None
    for t in range(16, min(rows, pref) + 1, 16):
        if rows % t == 0:
            best = t
    assert best is not None, (rows, pref)
    return best


def _mm_nn(name, a, w, layer, layout, out_dtypes, tm, tn=None, epi=None, extras=()):
    M, K = a.shape
    if layout == "col":
        nb = w.shape[3]
        N, tn = N_DEV * nb, nb
        w_spec = pl.BlockSpec((None, None, K, nb), lambda i, j: (j, layer, 0, 0))
    else:
        kb, N = w.shape[2], w.shape[3]
        assert N_DEV * kb == K
        w_spec = pl.BlockSpec((N_DEV, None, kb, tn), lambda i, j: (0, layer, 0, j))
    ne = len(extras)

    def body(a_ref, w_ref, *rest):
        wv = w_ref[...]
        if layout == "row":
            wv = wv.reshape(K, tn)
        acc = jnp.dot(a_ref[...], wv, preferred_element_type=F32)
        vals = epi(acc, *[r[...] for r in rest[:ne]]) if epi else (acc,)
        for o_ref, val in zip(rest[ne:], vals):
            o_ref[...] = val.astype(o_ref.dtype)

    tile = pl.BlockSpec((tm, tn), lambda i, j: (i, j))
    return pl.pallas_call(
        body, name=name, grid=(M // tm, N // tn),
        in_specs=[pl.BlockSpec((tm, K), lambda i, j: (i, 0)), w_spec] + [tile] * ne,
        out_specs=[tile] * len(out_dtypes),
        out_shape=[jax.ShapeDtypeStruct((M, N), d) for d in out_dtypes],
        compiler_params=_params("parallel", "parallel"),
    )(a, w, *extras)


def _mm_nt(name, a, w, layer, layout, out_dtypes, tm, tn=None, group=1, epi=None, extras=()):
    M, Nc = a.shape
    ne = len(extras)
    if layout == "row":
        kb = w.shape[2]
        No, tno = N_DEV * kb, group * kb
        w_spec = pl.BlockSpec((group, None, kb, Nc), lambda i, j: (j, layer, 0, 0))
        a_spec = pl.BlockSpec((tm, Nc), lambda i, j: (i, 0))
        grid = (M // tm, N_DEV // group)
        scratch = []

        def body(a_ref, w_ref, *rest):
            av = a_ref[...]
            for t in range(group):
                cols = slice(t * kb, (t + 1) * kb)
                acc = lax.dot_general(av, w_ref[t], NT_DIMS, preferred_element_type=F32)
                vals = epi(acc, *[r[:, cols] for r in rest[:ne]]) if epi else (acc,)
                for o_ref, val in zip(rest[ne:], vals):
                    o_ref[:, cols] = val.astype(o_ref.dtype)
    else:
        No, nb = w.shape[2], w.shape[3]
        assert N_DEV * nb == Nc
        tno = tn
        w_spec = pl.BlockSpec((N_DEV, None, tn, nb), lambda i, j: (0, layer, j, 0))
        a_spec = pl.BlockSpec((tm, Nc), lambda i, j: (i, 0))
        grid = (M // tm, No // tn)
        scratch = [pltpu.VMEM((tm, tn), F32)]

        def body(a_ref, w_ref, *rest):
            acc_ref = rest[-1]
            for d in range(N_DEV):
                part = lax.dot_general(a_ref[:, d * nb:(d + 1) * nb], w_ref[d], NT_DIMS,
                                       preferred_element_type=F32)
                if d == 0:
                    acc_ref[...] = part
                else:
                    acc_ref[...] += part
            acc = acc_ref[...]
            vals = epi(acc, *[r[...] for r in rest[:ne]]) if epi else (acc,)
            for o_ref, val in zip(rest[ne:-1], vals):
                o_ref[...] = val.astype(o_ref.dtype)

    tile = pl.BlockSpec((tm, tno), lambda i, j: (i, j))
    return pl.pallas_call(
        body, name=name, grid=grid,
        in_specs=[a_spec, w_spec] + [tile] * ne,
        out_specs=[tile] * len(out_dtypes),
        out_shape=[jax.ShapeDtypeStruct((M, No), d) for d in out_dtypes],
        scratch_shapes=scratch,
        compiler_params=_params("parallel", "parallel"),
    )(a, w, *extras)


def _mm_tn(name, a, g, layout, tl, group=1):
    L, Ka = a.shape
    N = g.shape[1]
    nl = L // tl
    if layout == "col":
        nb = N // N_DEV
        grid = (N_DEV, nl)
        in_specs = [pl.BlockSpec((tl, Ka), lambda j, k: (k, 0)), pl.BlockSpec((tl, nb), lambda j, k: (k, j))]
        out_spec = pl.BlockSpec((None, Ka, nb), lambda j, k: (j, 0, 0))
        out_shape = jax.ShapeDtypeStruct((N_DEV, Ka, nb), F32)

        def body(a_ref, g_ref, o_ref):
            part = lax.dot_general(a_ref[...], g_ref[...], TN_DIMS, preferred_element_type=F32)

            @pl.when(pl.program_id(1) == 0)
            def _():
                o_ref[...] = part

            @pl.when(pl.program_id(1) > 0)
            def _():
                o_ref[...] += part
    else:
        kb = Ka // N_DEV
        grid = (N_DEV // group, nl)
        in_specs = [pl.BlockSpec((tl, group * kb), lambda j, k: (k, j)), pl.BlockSpec((tl, N), lambda j, k: (k, 0))]
        out_spec = pl.BlockSpec((group, kb, N), lambda j, k: (j, 0, 0))
        out_shape = jax.ShapeDtypeStruct((N_DEV, kb, N), F32)

        def body(a_ref, g_ref, o_ref):
            gv = g_ref[...]
            for t in range(group):
                part = lax.dot_general(a_ref[:, t * kb:(t + 1) * kb], gv, TN_DIMS, preferred_element_type=F32)

                @pl.when(pl.program_id(1) == 0)
                def _():
                    o_ref[t] = part

                @pl.when(pl.program_id(1) > 0)
                def _():
                    o_ref[t] += part

    return pl.pallas_call(
        body, name=name, grid=grid, in_specs=in_specs, out_specs=out_spec, out_shape=out_shape,
        compiler_params=_params("parallel", "arbitrary"),
    )(a, g)


def _rstd(x):
    return lax.rsqrt(jnp.mean(x * x, axis=-1, keepdims=True) + RMS_EPS)


def _norm_fwd(name, x, gain):
    L, D = x.shape
    tr = _row_tile(L, 640)

    def body(x_ref, g_ref, o_ref):
        xv = x_ref[...]
        o_ref[...] = (xv * _rstd(xv) * g_ref[...]).astype(BF16)

    return pl.pallas_call(
        body, name=name, grid=(L // tr,),
        in_specs=[pl.BlockSpec((tr, D), lambda i: (i, 0)), pl.BlockSpec((1, D), lambda i: (0, 0))],
        out_specs=pl.BlockSpec((tr, D), lambda i: (i, 0)),
        out_shape=jax.ShapeDtypeStruct((L, D), BF16),
        compiler_params=_params("parallel"),
    )(x, gain)


def _res_norm(name, h, branch, g_post, g_pre):
    L, D = h.shape
    tr = _row_tile(L, 640)
    npre = len(g_pre)

    def body(h_ref, b_ref, gp_ref, *rest):
        bv = b_ref[...]
        hn = h_ref[...] + bv * _rstd(bv) * gp_ref[...]
        rest[npre][...] = hn
        if npre:
            xh = hn * _rstd(hn)
            for t in range(npre):
                rest[npre + 1 + t][...] = (xh * rest[t][...]).astype(BF16)

    row = pl.BlockSpec((tr, D), lambda i: (i, 0))
    gain = pl.BlockSpec((1, D), lambda i: (0, 0))
    return pl.pallas_call(
        body, name=name, grid=(L // tr,),
        in_specs=[row, row, gain] + [gain] * npre,
        out_specs=[row] * (1 + npre),
        out_shape=[jax.ShapeDtypeStruct((L, D), F32)] + [jax.ShapeDtypeStruct((L, D), BF16)] * npre,
        compiler_params=_params("parallel"),
    )(h, branch, g_post, *g_pre)


def _norm_bwd(name, dres, stream=None, pre=(), post=None):
    L, D = dres.shape
    tr = _row_tile(L, 320)
    npre = len(pre)
    has_post = post is not None
    n_in = 1 + (1 + 2 * npre if npre else 0) + (2 if has_post else 0)
    n_dg = npre + (1 if has_post else 0)

    def body(*refs):
        ins, outs = refs[:n_in], refs[n_in:]
        i = pl.program_id(0)
        dsum = ins[0][...]
        dgs = []
        pos = 1
        if npre:
            xv = ins[pos][...]
            pos += 1
            rs = _rstd(xv)
            xh = xv * rs
            dxh = None
            for t in range(npre):
                gv, dy = ins[pos][...], ins[pos + 1][...]
                pos += 2
                dgs.append(jnp.sum(dy * xh, axis=0, keepdims=True))
                term = dy * gv
                dxh = term if dxh is None else dxh + term
            dsum = dsum + rs * (dxh - xh * jnp.mean(dxh * xh, axis=-1, keepdims=True))
        outs[0][...] = dsum
        o = 1
        if has_post:
            bv, gp = ins[pos][...], ins[pos + 1][...]
            rs = _rstd(bv)
            bh = bv * rs
            dgs.append(jnp.sum(dsum * bh, axis=0, keepdims=True))
            dbh = dsum * gp
            outs[1][...] = (rs * (dbh - bh * jnp.mean(dbh * bh, axis=-1, keepdims=True))).astype(BF16)
            o = 2

        @pl.when(i == 0)
        def _():
            for t in range(n_dg):
                outs[o + t][...] = jnp.zeros((8, D), F32)

        for t in range(n_dg):
            outs[o + t][0:1, :] += dgs[t]

    row = pl.BlockSpec((tr, D), lambda i: (i, 0))
    gain = pl.BlockSpec((1, D), lambda i: (0, 0))
    acc = pl.BlockSpec((8, D), lambda i: (0, 0))
    args, in_specs = [dres], [row]
    if npre:
        args.append(stream)
        in_specs.append(row)
        for gv, dy in pre:
            args += [gv, dy]
            in_specs += [gain, row]
    if has_post:
        args += [post[0], post[1]]
        in_specs += [row, gain]
    out_specs = [row] + ([row] if has_post else []) + [acc] * n_dg
    out_shape = ([jax.ShapeDtypeStruct((L, D), F32)] + ([jax.ShapeDtypeStruct((L, D), BF16)] if has_post else [])
                 + [jax.ShapeDtypeStruct((8, D), F32)] * n_dg)
    return pl.pallas_call(
        body, name=name, grid=(L // tr,), in_specs=in_specs, out_specs=out_specs, out_shape=out_shape,
        compiler_params=_params("arbitrary"),
    )(*args)


def _conv_fwd(name, p, taps):
    L, D3 = p.shape
    D = D3 // 3
    nblk = L // BLK

    def body(p_ref, prev_ref, w_ref, y_ref, u_scr):
        i = pl.program_id(0)
        for cg in range(D // BLK):
            lo = cg * BLK
            b = p_ref[:, lo:lo + BLK]
            u = p_ref[:, D + lo:D + lo + BLK] * p_ref[:, 2 * D + lo:2 * D + lo + BLK]
            up = prev_ref[:, D + lo:D + lo + BLK] * prev_ref[:, 2 * D + lo:2 * D + lo + BLK]
            u_scr[0:8, :] = jnp.where(i > 0, up, 0.0)
            u_scr[8:8 + BLK, :] = u
            conv = (w_ref[0:1, lo:lo + BLK] * u_scr[6:6 + BLK, :] + w_ref[1:2, lo:lo + BLK] * u_scr[7:7 + BLK, :]
                    + w_ref[2:3, lo:lo + BLK] * u)
            y_ref[:, lo:lo + BLK] = (b * conv).astype(BF16)

    return pl.pallas_call(
        body, name=name, grid=(nblk,),
        in_specs=[pl.BlockSpec((BLK, D3), lambda i: (i, 0)),
                  pl.BlockSpec((8, D3), lambda i: (jnp.maximum(i * (BLK // 8) - 1, 0), 0)),
                  pl.BlockSpec((8, D), lambda i: (0, 0))],
        out_specs=pl.BlockSpec((BLK, D), lambda i: (i, 0)),
        out_shape=jax.ShapeDtypeStruct((L, D), BF16),
        scratch_shapes=[pltpu.VMEM((BLK + 8, BLK), F32)],
        compiler_params=_params("parallel"),
    )(p, p, taps)


def _conv_bwd(name, p, dy, taps):
    L, D3 = p.shape
    D = D3 // 3
    nblk = L // BLK
    last8 = L // 8 - 1

    def body(p_ref, prev_ref, next_ref, dy_ref, dyn_ref, w_ref, dp_ref, dw_ref, u_scr, d_scr):
        i = pl.program_id(0)

        @pl.when(i == 0)
        def _():
            dw_ref[...] = jnp.zeros((8, D), F32)

        for cg in range(D // BLK):
            lo = cg * BLK
            b = p_ref[:, lo:lo + BLK]
            c = p_ref[:, D + lo:D + lo + BLK]
            xv = p_ref[:, 2 * D + lo:2 * D + lo + BLK]
            u = c * xv
            up = prev_ref[:, D + lo:D + lo + BLK] * prev_ref[:, 2 * D + lo:2 * D + lo + BLK]
            u_scr[0:8, :] = jnp.where(i > 0, up, 0.0)
            u_scr[8:8 + BLK, :] = u
            u2 = u_scr[6:6 + BLK, :]
            u1 = u_scr[7:7 + BLK, :]
            w0, w1, w2 = w_ref[0:1, lo:lo + BLK], w_ref[1:2, lo:lo + BLK], w_ref[2:3, lo:lo + BLK]
            conv = w0 * u2 + w1 * u1 + w2 * u
            dyv = dy_ref[:, lo:lo + BLK]
            dconv = dyv * b
            dnext = dyn_ref[:, lo:lo + BLK] * next_ref[:, lo:lo + BLK]
            d_scr[0:BLK, :] = dconv
            d_scr[BLK:BLK + 8, :] = jnp.where(i < nblk - 1, dnext, 0.0)
            du = w2 * dconv + w1 * d_scr[1:1 + BLK, :] + w0 * d_scr[2:2 + BLK, :]
            dp_ref[:, lo:lo + BLK] = (dyv * conv).astype(BF16)
            dp_ref[:, D + lo:D + lo + BLK] = (du * xv).astype(BF16)
            dp_ref[:, 2 * D + lo:2 * D + lo + BLK] = (du * c).astype(BF16)
            dw_ref[0:1, lo:lo + BLK] += jnp.sum(dconv * u2, axis=0, keepdims=True)
            dw_ref[1:2, lo:lo + BLK] += jnp.sum(dconv * u1, axis=0, keepdims=True)
            dw_ref[2:3, lo:lo + BLK] += jnp.sum(dconv * u, axis=0, keepdims=True)

    halo_prev = lambda i: (jnp.maximum(i * (BLK // 8) - 1, 0), 0)
    halo_next = lambda i: (jnp.minimum((i + 1) * (BLK // 8), last8), 0)
    return pl.pallas_call(
        body, name=name, grid=(nblk,),
        in_specs=[pl.BlockSpec((BLK, D3), lambda i: (i, 0)), pl.BlockSpec((8, D3), halo_prev),
                  pl.BlockSpec((8, D3), halo_next), pl.BlockSpec((BLK, D), lambda i: (i, 0)),
                  pl.BlockSpec((8, D), halo_next), pl.BlockSpec((8, D), lambda i: (0, 0))],
        out_specs=[pl.BlockSpec((BLK, D3), lambda i: (i, 0)), pl.BlockSpec((8, D), lambda i: (0, 0))],
        out_shape=[jax.ShapeDtypeStruct((L, D3), BF16), jax.ShapeDtypeStruct((8, D), F32)],
        scratch_shapes=[pltpu.VMEM((BLK + 8, BLK), F32), pltpu.VMEM((BLK + 8, BLK), F32)],
        compiler_params=_params("arbitrary"),
    )(p, p, p, dy, dy, taps)


def _score_block(qm, kblk, kb, qpos, lane, rr, tri):
    z = lax.dot_general(qm, kblk, NT_DIMS, preferred_element_type=F32)
    kpos = kb * BLK + lane
    vis = (kpos < qpos) & (kpos >= PAD)
    sp = jnp.log(1.0 + jnp.exp(-jnp.abs(z)))
    lb = jnp.minimum(z, 0.0) - sp
    lom = jnp.where(vis, lb - z, 0.0)
    hi = lom.astype(BF16)
    lo = (lom - hi.astype(F32)).astype(BF16)
    cs = jnp.dot(hi, tri, preferred_element_type=F32) + jnp.dot(lo, tri, preferred_element_type=F32)
    a = jnp.where(vis, jnp.exp(lb + rr + cs), 0.0)
    return vis, lb, lom, a


def _attn_fwd(name, q, k, v):
    L, D = q.shape

    def body(q_ref, k_ref, v_ref, of_ref, ob_ref, acc_ref, r_ref):
        i = pl.program_id(1)
        lane = lax.broadcasted_iota(jnp.int32, (BLK, BLK), 1)
        row = lax.broadcasted_iota(jnp.int32, (BLK, BLK), 0)
        tri = (row > lane).astype(BF16)
        qpos = i * BLK + row
        q2 = q_ref[...]
        heads = []
        for hh in range(2):
            hmask = (lane < HEAD_DIM) if hh == 0 else (lane >= HEAD_DIM)
            qm = jnp.where(hmask, q2, jnp.zeros_like(q2))
            acc_ref[...] = jnp.zeros((BLK, BLK), F32)
            r_ref[...] = jnp.zeros((BLK, 1), F32)

            def step(carry):
                kb, _ = carry
                off = pl.multiple_of(kb * BLK, BLK)
                rr = r_ref[...]
                _, _, lom, a = _score_block(qm, k_ref[pl.ds(off, BLK), :], kb, qpos, lane, rr, tri)
                vblk = v_ref[pl.ds(off, BLK), :]
                ahi = a.astype(BF16)
                alo = (a - ahi.astype(F32)).astype(BF16)
                acc_ref[...] += (jnp.dot(ahi, vblk, preferred_element_type=F32)
                                 + jnp.dot(alo, vblk, preferred_element_type=F32))
                rn = rr + jnp.sum(lom, axis=1, keepdims=True)
                r_ref[...] = rn
                return kb - 1, (jnp.max(rn) < LOG_ZERO).astype(jnp.int32)

            lax.while_loop(lambda c: (c[0] >= 0) & (c[1] == 0), step, (i, jnp.int32(0)))
            heads.append(acc_ref[...])
        o = jnp.where(lane < HEAD_DIM, heads[0], heads[1])
        of_ref[...] = o
        ob_ref[...] = o.astype(BF16)

    qspec = pl.BlockSpec((BLK, BLK), lambda h, i: (i, h))
    kvspec = pl.BlockSpec((L, BLK), lambda h, i: (0, h))
    return pl.pallas_call(
        body, name=name, grid=(D // BLK, L // BLK),
        in_specs=[qspec, kvspec, kvspec], out_specs=[qspec, qspec],
        out_shape=[jax.ShapeDtypeStruct((L, D), F32), jax.ShapeDtypeStruct((L, D), BF16)],
        scratch_shapes=[pltpu.VMEM((BLK, BLK), F32), pltpu.VMEM((BLK, 1), F32)],
        compiler_params=_params("parallel", "arbitrary"),
    )(q, k, v)


def _attn_bwd(name, q, k, v, o, do):
    L, D = q.shape

    def body(q_ref, k_ref, v_ref, o_ref, do_ref, dq_ref, dk_ref, dv_ref, acc_ref, r_ref, c_ref):
        i = pl.program_id(1)

        @pl.when(i == 0)
        def _():
            dk_ref[...] = jnp.zeros((L, BLK), F32)
            dv_ref[...] = jnp.zeros((L, BLK), F32)

        lane = lax.broadcasted_iota(jnp.int32, (BLK, BLK), 1)
        row = lax.broadcasted_iota(jnp.int32, (BLK, BLK), 0)
        tri = (row > lane).astype(BF16)
        tri_incl = (row >= lane).astype(BF16)
        qpos = i * BLK + row
        q2 = q_ref[...]
        dov = do_ref[...].astype(BF16)
        prod = dov.astype(F32) * o_ref[...]
        heads = []
        for hh in range(2):
            hmask = (lane < HEAD_DIM) if hh == 0 else (lane >= HEAD_DIM)
            qm = jnp.where(hmask, q2, jnp.zeros_like(q2))
            dom = jnp.where(hmask, dov, jnp.zeros_like(dov))
            total = jnp.sum(jnp.where(hmask, prod, 0.0), axis=1, keepdims=True)
            acc_ref[...] = jnp.zeros((BLK, BLK), F32)
            r_ref[...] = jnp.zeros((BLK, 1), F32)
            c_ref[...] = jnp.zeros((BLK, 1), F32)

            def step(carry):
                kb, _ = carry
                off = pl.multiple_of(kb * BLK, BLK)
                rr = r_ref[...]
                kblk = k_ref[pl.ds(off, BLK), :]
                vis, lb, lom, a = _score_block(qm, kblk, kb, qpos, lane, rr, tri)
                da = lax.dot_general(dom, v_ref[pl.ds(off, BLK), :], NT_DIMS, preferred_element_type=F32)
                g = da * a
                ghi = g.astype(BF16)
                glo = (g - ghi.astype(F32)).astype(BF16)
                sfx = (jnp.dot(ghi, tri_incl, preferred_element_type=F32)
                       + jnp.dot(glo, tri_incl, preferred_element_type=F32))
                cc = c_ref[...]
                before = total - cc - sfx
                beta = jnp.exp(lb)
                dz = jnp.where(vis, g * (1.0 - beta) - beta * before, 0.0).astype(BF16)
                acc_ref[...] += jnp.dot(dz, kblk, preferred_element_type=F32)
                dk_ref[pl.ds(off, BLK), :] += lax.dot_general(dz, qm, TN_DIMS, preferred_element_type=F32)
                dv_ref[pl.ds(off, BLK), :] += lax.dot_general(a.astype(BF16), dom, TN_DIMS,
                                                              preferred_element_type=F32)
                c_ref[...] = cc + jnp.sum(g, axis=1, keepdims=True)
                rn = rr + jnp.sum(lom, axis=1, keepdims=True)
                r_ref[...] = rn
                return kb - 1, (jnp.max(rn) < LOG_ZERO).astype(jnp.int32)

            lax.while_loop(lambda c: (c[0] >= 0) & (c[1] == 0), step, (i, jnp.int32(0)))
            heads.append(acc_ref[...])
        dq_ref[...] = (jnp.where(lane < HEAD_DIM, heads[0], heads[1]) * ATTN_SCALE).astype(BF16)

    qspec = pl.BlockSpec((BLK, BLK), lambda h, i: (i, h))
    kvspec = pl.BlockSpec((L, BLK), lambda h, i: (0, h))
    return pl.pallas_call(
        body, name=name, grid=(D // BLK, L // BLK),
        in_specs=[qspec, kvspec, kvspec, qspec, qspec], out_specs=[qspec, kvspec, kvspec],
        out_shape=[jax.ShapeDtypeStruct((L, D), BF16), jax.ShapeDtypeStruct((L, D), F32),
                   jax.ShapeDtypeStruct((L, D), F32)],
        scratch_shapes=[pltpu.VMEM((BLK, BLK), F32), pltpu.VMEM((BLK, 1), F32), pltpu.VMEM((BLK, 1), F32)],
        compiler_params=_params("parallel", "arbitrary"),
    )(q, k, v, o, do)


def _loss_and_grad(name, h, target):
    L, D = h.shape
    first = (PAD + N_META) // BLK

    def body(h_ref, t_ref, sq_ref, dh_ref):
        i = pl.program_id(0)

        @pl.when(i == 0)
        def _():
            sq_ref[...] = jnp.zeros((8, BLK), F32)

        @pl.when(i < first)
        def _():
            dh_ref[...] = jnp.zeros((BLK, D), F32)

        @pl.when(i >= first)
        def _():
            err = h_ref[...] - t_ref[...]
            sq_ref[...] += jnp.sum(err * err)
            dh_ref[...] = err * (1.0 / D)

    return pl.pallas_call(
        body, name=name, grid=(L // BLK,),
        in_specs=[pl.BlockSpec((BLK, D), lambda i: (i, 0)),
                  pl.BlockSpec((BLK, D), lambda i: (jnp.maximum(i - first, 0), 0))],
        out_specs=[pl.BlockSpec((8, BLK), lambda i: (0, 0)), pl.BlockSpec((BLK, D), lambda i: (i, 0))],
        out_shape=[jax.ShapeDtypeStruct((8, BLK), F32), jax.ShapeDtypeStruct((L, D), F32)],
        compiler_params=_params("arbitrary"),
    )(h, target)


def _add_cast(name, a, b):
    L, D = a.shape
    tr = _row_tile(L, 640)

    def body(a_ref, b_ref, o_ref):
        o_ref[...] = (a_ref[...] + b_ref[...]).astype(BF16)

    row = pl.BlockSpec((tr, D), lambda i: (i, 0))
    return pl.pallas_call(body, name=name, grid=(L // tr,), in_specs=[row, row], out_specs=row,
                          out_shape=jax.ShapeDtypeStruct((L, D), BF16), compiler_params=_params("parallel"))(a, b)


def _adamw(w, g, m, v):
    m = ADAM_B1 * m + (1.0 - ADAM_B1) * g
    v = ADAM_B2 * v + (1.0 - ADAM_B2) * (g * g)
    m_hat = m / (1.0 - ADAM_B1 ** ADAM_STEP)
    v_hat = v / (1.0 - ADAM_B2 ** ADAM_STEP)
    delta = -ADAM_LR * (m_hat / (jnp.sqrt(v_hat) + ADAM_EPS) + ADAM_WD * w)
    return delta, m, v


def _adam_small(name, w, g, m, v):
    def body(w_ref, g_ref, m_ref, v_ref, d_ref, mo_ref, vo_ref):
        d, mn, vn = _adamw(w_ref[...], g_ref[...], m_ref[...], v_ref[...])
        d_ref[...] = d
        mo_ref[...] = mn
        vo_ref[...] = vn

    return pl.pallas_call(body, name=name, out_shape=[jax.ShapeDtypeStruct(w.shape, F32)] * 3)(w, g, m, v)


def _sum_slots(name, slots):
    def body(s_ref, o_ref):
        acc = s_ref[0]
        for d in range(1, N_DEV):
            acc = acc + s_ref[d]
        o_ref[...] = acc

    return pl.pallas_call(body, name=name, out_shape=jax.ShapeDtypeStruct(slots.shape[1:], F32))(slots)


def _pair_sum(name, g, recv, c_idx):
    _, R, C = g.shape
    tr = _row_tile(R, 512)

    def body(sel_ref, g_ref, r_ref, o_ref):
        o_ref[...] = g_ref[...] + r_ref[...]

    return pl.pallas_call(
        body, name=name,
        grid_spec=pltpu.PrefetchScalarGridSpec(
            num_scalar_prefetch=1, grid=(4, R // tr),
            in_specs=[pl.BlockSpec((None, tr, C), lambda k, i, sel: (2 * k + sel[0], i, 0)),
                      pl.BlockSpec((None, tr, C), lambda k, i, sel: (k, i, 0))],
            out_specs=pl.BlockSpec((None, tr, C), lambda k, i, sel: (k, i, 0))),
        out_shape=jax.ShapeDtypeStruct((4, R, C), F32),
        compiler_params=_params("parallel", "parallel"),
    )(c_idx, g, recv)


def _final_sum_adam(name, part, recv, chip_idx, w, m, v):
    R, C = w.shape
    tr = _row_tile(R, 512)

    def body(sel_ref, p_ref, r_ref, w_ref, m_ref, v_ref, g_ref, d_ref, mo_ref, vo_ref):
        g = ((p_ref[...] + r_ref[0]) + r_ref[1]) + r_ref[2]
        d, mn, vn = _adamw(w_ref[...], g, m_ref[...], v_ref[...])
        g_ref[...] = g
        d_ref[...] = d
        mo_ref[...] = mn
        vo_ref[...] = vn

    row = pl.BlockSpec((tr, C), lambda i, sel: (i, 0))
    return pl.pallas_call(
        body, name=name,
        grid_spec=pltpu.PrefetchScalarGridSpec(
            num_scalar_prefetch=1, grid=(R // tr,),
            in_specs=[pl.BlockSpec((None, tr, C), lambda i, sel: (sel[0], i, 0)),
                      pl.BlockSpec((3, tr, C), lambda i, sel: (0, i, 0)), row, row, row],
            out_specs=[row] * 4),
        out_shape=[jax.ShapeDtypeStruct((R, C), F32)] * 4,
        compiler_params=_params("parallel"),
    )(chip_idx, part, recv, w, m, v)


def _position():
    return lax.axis_index("x"), lax.axis_index("y"), lax.axis_index("c")


ANY = pl.BlockSpec(memory_space=pl.ANY)


def _all_gather(name, shards):
    n = len(shards)

    def body(*refs):
        x_refs, o_refs = refs[:n], refs[n:2 * n]
        send_sems, recv_sems, local_sems = refs[2 * n:]
        x, y, c = _position()
        me, sibling = (x, y, c), (x, y, 1 - c)
        chips = [(1 - x, y), (x, 1 - y), (1 - x, 1 - y)]

        def copy(a, sem, block, to, src=None):
            dst = o_refs[a].at[4 * block[0] + 2 * block[1] + block[2]]
            return pltpu.make_async_remote_copy(
                src_ref=dst if src is None else src, dst_ref=dst,
                send_sem=send_sems.at[a, sem], recv_sem=recv_sems.at[a, sem], device_id=to, device_id_type=MESH)

        mine = [pltpu.make_async_copy(x_refs[a], o_refs[a].at[4 * x + 2 * y + c], local_sems.at[a]) for a in range(n)]
        for cp in mine:
            cp.start()
        first = []
        for a in range(n):
            first.append(copy(a, 0, me, sibling, src=x_refs[a]))
            first += [copy(a, 1 + j, me, (*chip, c), src=x_refs[a]) for j, chip in enumerate(chips)]
        for cp in first:
            cp.start()
        passed = []
        for j, chip in enumerate(chips):
            for a in range(n):
                copy(a, 1 + j, (*chip, c), me).wait_recv()
                fwd = copy(a, 4 + j, (*chip, c), sibling)
                fwd.start()
                passed.append(fwd)
        for a in range(n):
            copy(a, 0, sibling, me).wait_recv()
        for j, chip in enumerate(chips):
            for a in range(n):
                copy(a, 4 + j, (*chip, 1 - c), me).wait_recv()
        for cp in first + passed:
            cp.wait_send()
        for cp in mine:
            cp.wait()

    return pl.pallas_call(
        body, name=name, in_specs=[ANY] * n, out_specs=[ANY] * n,
        out_shape=[jax.ShapeDtypeStruct((N_DEV,) + s.shape, s.dtype) for s in shards],
        scratch_shapes=[pltpu.SemaphoreType.DMA((n, 7)), pltpu.SemaphoreType.DMA((n, 7)),
                        pltpu.SemaphoreType.DMA((n,))],
        compiler_params=pltpu.CompilerParams(has_side_effects=True),
    )(*shards)


def _exchange_sibling(name, grads):
    n = len(grads)

    def body(*refs):
        g_refs, r_refs = refs[:n], refs[n:2 * n]
        send_sems, recv_sems = refs[2 * n:]
        x, y, c = _position()
        copies = []
        for a in range(n):
            for k in range(4):
                copies.append(pltpu.make_async_remote_copy(
                    src_ref=g_refs[a].at[2 * k + (1 - c)], dst_ref=r_refs[a].at[k],
                    send_sem=send_sems.at[a, k], recv_sem=recv_sems.at[a, k],
                    device_id=(x, y, 1 - c), device_id_type=MESH))
        for cp in copies:
            cp.start()
        for cp in copies:
            cp.wait()

    return pl.pallas_call(
        body, name=name, in_specs=[ANY] * n, out_specs=[ANY] * n,
        out_shape=[jax.ShapeDtypeStruct((4,) + g.shape[1:], g.dtype) for g in grads],
        scratch_shapes=[pltpu.SemaphoreType.DMA((n, 4)), pltpu.SemaphoreType.DMA((n, 4))],
        compiler_params=pltpu.CompilerParams(has_side_effects=True),
    )(*grads)


def _exchange_chips(name, parts):
    n = len(parts)

    def body(*refs):
        p_refs, r_refs = refs[:n], refs[n:2 * n]
        send_sems, recv_sems = refs[2 * n:]
        x, y, c = _position()
        chips = [(1 - x, y), (x, 1 - y), (1 - x, 1 - y)]
        copies = []
        for a in range(n):
            for j, chip in enumerate(chips):
                copies.append(pltpu.make_async_remote_copy(
                    src_ref=p_refs[a].at[2 * chip[0] + chip[1]], dst_ref=r_refs[a].at[j],
                    send_sem=send_sems.at[a, j], recv_sem=recv_sems.at[a, j],
                    device_id=(*chip, c), device_id_type=MESH))
        for cp in copies:
            cp.start()
        for cp in copies:
            cp.wait()

    return pl.pallas_call(
        body, name=name, in_specs=[ANY] * n, out_specs=[ANY] * n,
        out_shape=[jax.ShapeDtypeStruct((3,) + p.shape[1:], p.dtype) for p in parts],
        scratch_shapes=[pltpu.SemaphoreType.DMA((n, 3)), pltpu.SemaphoreType.DMA((n, 3))],
        compiler_params=pltpu.CompilerParams(has_side_effects=True),
    )(*parts)


def kernel(x, meta_tokens, norm_gains, conv_in_proj, conv_w, conv_out_proj, kv_norm, w_k, w_v, w_q, w_o, mlp_w1, mlp_w2, loss_target, m_meta_tokens, m_norm_gains, m_conv_in_proj, m_conv_w, m_conv_out_proj, m_kv_norm, m_w_k, m_w_v, m_w_q, m_w_o, m_mlp_w1, m_mlp_w2, v_meta_tokens, v_norm_gains, v_conv_in_proj, v_conv_w, v_conv_out_proj, v_kv_norm, v_w_k, v_w_v, v_w_q, v_w_o, v_mlp_w1, v_mlp_w2):
    xi, target = x[0], loss_target[0]
    S, D = xi.shape
    L = PAD + N_META + S
    dsh = D // N_DEV
    px, py, pc = _position()
    dev = 4 * px + 2 * py + pc
    tm_big = _row_tile(L, 1664)
    tm_mid = _row_tile(L, 640)

    def pack_small(meta, gains, taps):
        return jnp.concatenate([meta, gains.reshape(DEPTH * 4, dsh), taps.reshape(N_A * 3, dsh),
                                jnp.zeros((2, dsh), F32)], axis=0)

    big_w = [conv_in_proj, conv_out_proj, w_k[None], w_v[None], w_q, w_o, mlp_w1, mlp_w2]
    big_m = [m_conv_in_proj, m_conv_out_proj, m_w_k[None], m_w_v[None], m_w_q, m_w_o, m_mlp_w1, m_mlp_w2]
    big_v = [v_conv_in_proj, v_conv_out_proj, v_w_k[None], v_w_v[None], v_w_q, v_w_o, v_mlp_w1, v_mlp_w2]
    small_w = pack_small(meta_tokens, norm_gains, conv_w)
    gathered = _all_gather("gather_weights", [small_w] + [w.astype(BF16) for w in big_w])
    small_full = gathered[0].transpose(1, 0, 2).reshape(40, D)
    win_g, wout_g, wk_g, wv_g, wq_g, wo_g, w1_g, w2_g = gathered[1:]
    meta_full = small_full[0:N_META]
    gain = lambda layer, n: small_full[N_META + 4 * layer + n][None]
    taps = [jnp.concatenate([small_full[32 + 3 * l:35 + 3 * l], jnp.zeros((5, D), F32)], axis=0) for l in range(N_A)]
    kvn = kv_norm[None]

    h = jnp.concatenate([jnp.zeros((PAD, D), F32), meta_full, xi], axis=0)
    n1 = _norm_fwd("norm_in", h, gain(0, 0))
    saved = []
    hk = k = v = None
    for layer in range(DEPTH):
        s = {"h0": h, "n1": n1}
        if layer < N_A:
            s["p"] = _mm_nn(f"conv_in{layer}", n1, win_g, layer, "col", [F32], tm_big)[0]
            s["y"] = _conv_fwd(f"conv{layer}", s["p"], taps[layer])
            mix = _mm_nn(f"conv_out{layer}", s["y"], wout_g, layer, "row", [F32], tm_big, tn=D)[0]
        else:
            j = layer - N_A
            if j == 0:
                k = _mm_nn("k_proj", hk, wk_g, 0, "row", [BF16], tm_big, tn=D)[0]
                v = _mm_nn("v_proj", hk, wv_g, 0, "row", [BF16], tm_big, tn=D)[0]
            s["q"] = _mm_nn(f"q_proj{j}", n1, wq_g, j, "row", [BF16], tm_big, tn=D,
                            epi=lambda acc: (acc * ATTN_SCALE,))[0]
            s["o"], s["ob"] = _attn_fwd(f"attn{j}", s["q"], k, v)
            mix = _mm_nn(f"o_proj{j}", s["ob"], wo_g, j, "row", [F32], tm_big, tn=D)[0]
        s["mix"] = mix
        s["h1"], s["n3"] = _res_norm(f"mix_norm{layer}", h, mix, gain(layer, 1), [gain(layer, 2)])
        s["r"], s["act"] = _mm_nn(f"mlp_up{layer}", s["n3"], w1_g, layer, "col", [BF16, BF16], tm_big,
                                  epi=lambda acc: (jnp.maximum(acc, 0.0), jnp.square(jnp.maximum(acc, 0.0))))
        s["ff"] = _mm_nn(f"mlp_down{layer}", s["act"], w2_g, layer, "row", [F32], tm_mid, tn=D)[0]
        pre = [] if layer == DEPTH - 1 else [gain(layer + 1, 0)] + ([kvn] if layer == N_A - 1 else [])
        outs = _res_norm(f"mlp_norm{layer}", s["h1"], s["ff"], gain(layer, 3), pre)
        h = outs[0]
        if pre:
            n1 = outs[1]
        if layer == N_A - 1:
            hk = outs[2]
        saved.append(s)

    sq, dh = _loss_and_grad("loss", h, target)
    loss = lax.psum(0.5 * sq[0, 0] / D, ("x", "y", "c"))

    g_gain = [[None] * 4 for _ in range(DEPTH)]
    g_taps = [None] * N_A
    g_w = {n: [None] * (DEPTH if n in ("w1", "w2") else 2) for n in ("win", "wout", "wq", "wo", "w1", "w2")}
    dk_parts, dv_parts = [], []
    g_kvn = None
    s = saved[DEPTH - 1]
    dh, dff, g_gain[DEPTH - 1][3] = _norm_bwd("bwd_top", dh, post=(s["ff"], gain(DEPTH - 1, 3)))
    for layer in reversed(range(DEPTH)):
        s = saved[layer]
        da1 = _mm_nt(f"mlp_down_bwd{layer}", dff, w2_g, layer, "row", [BF16], tm_big,
                     epi=lambda acc, r: (acc * (2.0 * r.astype(F32)),), extras=(s["r"],))[0]
        g_w["w2"][layer] = _mm_tn(f"mlp_w2_grad{layer}", s["act"], dff, "row", tm_big)
        g_w["w1"][layer] = _mm_tn(f"mlp_w1_grad{layer}", s["n3"], da1, "col", tm_big)
        dn3 = _mm_nt(f"mlp_up_bwd{layer}", da1, w1_g, layer, "col", [F32], tm_mid, tn=D)[0]
        dh, dmix, g_gain[layer][2], g_gain[layer][1] = _norm_bwd(
            f"bwd_mid{layer}", dh, stream=s["h1"], pre=[(gain(layer, 2), dn3)], post=(s["mix"], gain(layer, 1)))
        pre = []
        if layer < N_A:
            dy = _mm_nt(f"conv_out_bwd{layer}", dmix, wout_g, layer, "row", [F32], tm_big, group=4)[0]
            g_w["wout"][layer] = _mm_tn(f"conv_out_grad{layer}", s["y"], dmix, "row", tm_big, group=4)
            dp, g_taps[layer] = _conv_bwd(f"conv_bwd{layer}", s["p"], dy, taps[layer])
            g_w["win"][layer] = _mm_tn(f"conv_in_grad{layer}", s["n1"], dp, "col", tm_big)
            dn1 = _mm_nt(f"conv_in_bwd{layer}", dp, win_g, layer, "col", [F32], tm_mid, tn=D)[0]
        else:
            j = layer - N_A
            do = _mm_nt(f"o_proj_bwd{j}", dmix, wo_g, j, "row", [F32], tm_big, group=4)[0]
            g_w["wo"][j] = _mm_tn(f"o_proj_grad{j}", s["ob"], dmix, "row", tm_big, group=4)
            dq, dk_j, dv_j = _attn_bwd(f"attn_bwd{j}", s["q"], k, v, s["o"], do)
            dk_parts.append(dk_j)
            dv_parts.append(dv_j)
            g_w["wq"][j] = _mm_tn(f"q_proj_grad{j}", s["n1"], dq, "row", tm_big, group=4)
            dn1 = _mm_nt(f"q_proj_bwd{j}", dq, wq_g, j, "row", [F32], tm_big, group=4)[0]
            if j == 0:
                dkb = _add_cast("dk_sum", dk_parts[0], dk_parts[1])
                dvb = _add_cast("dv_sum", dv_parts[0], dv_parts[1])
                g_wk = _mm_tn("k_proj_grad", hk, dkb, "row", tm_big, group=4)
                g_wv = _mm_tn("v_proj_grad", hk, dvb, "row", tm_big, group=4)
                dhk_k = _mm_nt("k_proj_bwd", dkb, wk_g, 0, "row", [F32], tm_big, group=4)[0]
                dhk = _mm_nt("v_proj_bwd", dvb, wv_g, 0, "row", [F32], tm_big, group=4,
                             epi=lambda acc, other: (acc + other,), extras=(dhk_k,))[0]
                pre = [(kvn, dhk)]
        pre = [(gain(layer, 0), dn1)] + pre
        if layer > 0:
            sp = saved[layer - 1]
            outs = _norm_bwd(f"bwd_in{layer}", dh, stream=s["h0"], pre=pre, post=(sp["ff"], gain(layer - 1, 3)))
            dh, dff = outs[0], outs[1]
            g_gain[layer][0] = outs[2]
            if len(pre) == 2:
                g_kvn = outs[3]
            g_gain[layer - 1][3] = outs[-1]
        else:
            dh, g_gain[0][0] = _norm_bwd("bwd_in0", dh, stream=s["h0"], pre=pre)

    grad_x = dh[PAD + N_META:][None]
    g_meta = dh[PAD:PAD + N_META]

    small_g = jnp.concatenate(
        [g_meta] + [g_gain[l][n][0:1] for l in range(DEPTH) for n in range(4)]
        + [g_taps[l][0:3] for l in range(N_A)] + [g_kvn[0:1], jnp.zeros((1, D), F32)], axis=0)
    small_sum = _sum_slots("small_grad_sum", _all_gather("gather_small_grads", [small_g])[0])
    small_mine = lax.dynamic_slice_in_dim(small_sum, dev * dsh, dsh, axis=1)
    small_m = pack_small(m_meta_tokens, m_norm_gains, m_conv_w)
    small_v = pack_small(v_meta_tokens, v_norm_gains, v_conv_w)
    small_d, small_mn, small_vn = _adam_small("adam_small", small_w, small_mine, small_m, small_v)
    pad8 = lambda a: jnp.concatenate([a[None], jnp.zeros((7, D), F32)], axis=0)
    g_kv = small_sum[38]
    kv_d, kv_mn, kv_vn = _adam_small("adam_kv_norm", pad8(kv_norm), pad8(g_kv), pad8(m_kv_norm), pad8(v_kv_norm))

    def unpack_small(a):
        return (a[0:N_META], a[N_META:N_META + 16].reshape(DEPTH, 4, dsh), a[32:38].reshape(N_A, 3, dsh))

    stack = lambda parts: jnp.stack(parts, axis=1)
    big_g = [stack(g_w["win"]), stack(g_w["wout"]), g_wk[:, None], g_wv[:, None], stack(g_w["wq"]),
             stack(g_w["wo"]), stack(g_w["w1"]), stack(g_w["w2"])]
    flat = lambda a, lead: a.reshape(lead + (-1, a.shape[-1]))
    big_g = [flat(g, (N_DEV,)) for g in big_g]
    from_sibling = _exchange_sibling("grads_to_sibling", big_g)
    c_idx = jnp.reshape(pc, (1,)).astype(jnp.int32)
    chip_idx = jnp.reshape(2 * px + py, (1,)).astype(jnp.int32)
    names = ["conv_in", "conv_out", "w_k", "w_v", "w_q", "w_o", "mlp_w1", "mlp_w2"]
    parts = [_pair_sum(f"pair_sum_{nm}", g, r, c_idx) for nm, g, r in zip(names, big_g, from_sibling)]
    from_chips = _exchange_chips("grads_to_chips", parts)
    big_out = []
    for nm, part, recv, w, m, v2 in zip(names, parts, from_chips, big_w, big_m, big_v):
        res = _final_sum_adam(f"adam_{nm}", part, recv, chip_idx, flat(w, ()), flat(m, ()), flat(v2, ()))
        shape = w.shape[1:] if nm in ("w_k", "w_v") else w.shape
        big_out.append([r.reshape(shape) for r in res])

    def assemble(kind, small_parts, kv_part):
        meta_p, gains_p, taps_p = small_parts
        b = [o[kind] for o in big_out]
        return [meta_p, gains_p, b[0], taps_p, b[1], kv_part, b[2], b[3], b[4], b[5], b[6], b[7]]

    grads = assemble(0, unpack_small(small_mine), g_kv)
    deltas = assemble(1, unpack_small(small_d), kv_d[0])
    new_m = assemble(2, unpack_small(small_mn), kv_mn[0])
    new_v = assemble(3, unpack_small(small_vn), kv_vn[0])
    return (loss, grad_x, *grads, *deltas, *new_m, *new_v)
```

```python
import functools

import jax
import jax.numpy as jnp
from jax import lax
from jax.experimental import pallas as pl
from jax.experimental.pallas import tpu as pltpu

F32 = jnp.float32
BF16 = jnp.bfloat16
MESH = pl.DeviceIdType.MESH

N_DEV = 8
N_META = 16
BLK = 128
PAD = (-N_META) % BLK
HEAD_DIM = 64
DEPTH = 4
N_A = 2
RMS_EPS = 1e-6
ATTN_SCALE = HEAD_DIM ** -0.5
LOG_ZERO = -105.0
ADAM_LR, ADAM_B1, ADAM_B2, ADAM_EPS, ADAM_WD, ADAM_STEP = 0.001, 0.9, 0.999, 1e-08, 0.01, 10
VMEM_LIMIT = 56 * 2 ** 20

NT_DIMS = (((1,), (1,)), ((), ()))
TN_DIMS = (((0,), (0,)), ((), ()))
ANY = pl.BlockSpec(memory_space=pl.ANY)


def _params(*sem):
    return pltpu.CompilerParams(dimension_semantics=sem, vmem_limit_bytes=VMEM_LIMIT)


def _row_tile(rows, pref):
    best = None
    for t in range(16, min(rows, pref) + 1, 16):
        if rows % t == 0:
            best = t
    assert best is not None, (rows, pref)
    return best


def _mm_nn(name, a, w, layer, layout, out_dtypes, tm, tn=None, epi=None, extras=()):
    M, K = a.shape
    if layout == "col":
        nb = w.shape[3]
        N, tn = N_DEV * nb, nb
        w_spec = pl.BlockSpec((None, None, K, nb), lambda i, j: (j, layer, 0, 0))
    else:
        kb, N = w.shape[2], w.shape[3]
        assert N_DEV * kb == K
        w_spec = pl.BlockSpec((N_DEV, None, kb, tn), lambda i, j: (0, layer, 0, j))
    ne = len(extras)

    def body(a_ref, w_ref, *rest):
        wv = w_ref[...]
        if layout == "row":
            wv = wv.reshape(K, tn)
        acc = jnp.dot(a_ref[...], wv, preferred_element_type=F32)
        vals = epi(acc, *[r[...] for r in rest[:ne]]) if epi else (acc,)
        for o_ref, val in zip(rest[ne:], vals):
            o_ref[...] = val.astype(o_ref.dtype)

    tile = pl.BlockSpec((tm, tn), lambda i, j: (i, j))
    return pl.pallas_call(
        body, name=name, grid=(M // tm, N // tn),
        in_specs=[pl.BlockSpec((tm, K), lambda i, j: (i, 0)), w_spec] + [tile] * ne,
        out_specs=[tile] * len(out_dtypes),
        out_shape=[jax.ShapeDtypeStruct((M, N), d) for d in out_dtypes],
        compiler_params=_params("parallel", "parallel"),
    )(a, w, *extras)


def _mm_nt(name, a, w, layer, layout, out_dtypes, tm, tn=None, group=1, epi=None, extras=()):
    M, Nc = a.shape
    ne = len(extras)
    if layout == "row":
        kb = w.shape[2]
        No, tno = N_DEV * kb, group * kb
        w_spec = pl.BlockSpec((group, None, kb, Nc), lambda i, j: (j, layer, 0, 0))
        a_spec = pl.BlockSpec((tm, Nc), lambda i, j: (i, 0))
        grid = (M // tm, N_DEV // group)
        scratch = []

        def body(a_ref, w_ref, *rest):
            av = a_ref[...]
            for t in range(group):
                cols = slice(t * kb, (t + 1) * kb)
                acc = lax.dot_general(av, w_ref[t], NT_DIMS, preferred_element_type=F32)
                vals = epi(acc, *[r[:, cols] for r in rest[:ne]]) if epi else (acc,)
                for o_ref, val in zip(rest[ne:], vals):
                    o_ref[:, cols] = val.astype(o_ref.dtype)
    else:
        No, nb = w.shape[2], w.shape[3]
        assert N_DEV * nb == Nc
        tno = tn
        w_spec = pl.BlockSpec((N_DEV, None, tn, nb), lambda i, j: (0, layer, j, 0))
        a_spec = pl.BlockSpec((tm, Nc), lambda i, j: (i, 0))
        grid = (M // tm, No // tn)
        scratch = [pltpu.VMEM((tm, tn), F32)]

        def body(a_ref, w_ref, *rest):
            acc_ref = rest[-1]
            for d in range(N_DEV):
                part = lax.dot_general(a_ref[:, d * nb:(d + 1) * nb], w_ref[d], NT_DIMS,
                                       preferred_element_type=F32)
                if d == 0:
                    acc_ref[...] = part
                else:
                    acc_ref[...] += part
            acc = acc_ref[...]
            vals = epi(acc, *[r[...] for r in rest[:ne]]) if epi else (acc,)
            for o_ref, val in zip(rest[ne:-1], vals):
                o_ref[...] = val.astype(o_ref.dtype)

    tile = pl.BlockSpec((tm, tno), lambda i, j: (i, j))
    return pl.pallas_call(
        body, name=name, grid=grid,
        in_specs=[a_spec, w_spec] + [tile] * ne,
        out_specs=[tile] * len(out_dtypes),
        out_shape=[jax.ShapeDtypeStruct((M, No), d) for d in out_dtypes],
        scratch_shapes=scratch,
        compiler_params=_params("parallel", "parallel"),
    )(a, w, *extras)


def _mm_tn(name, a, g, layout, tl, group=1, layer=0, layers=1, into=None):
    L, Ka = a.shape
    N = g.shape[1]
    nl = L // tl
    if layout == "col":
        nb = N // N_DEV
        grid = (N_DEV, nl)
        in_specs = [pl.BlockSpec((tl, Ka), lambda j, k: (k, 0)), pl.BlockSpec((tl, nb), lambda j, k: (k, j))]
        out_spec = pl.BlockSpec((None, None, Ka, nb), lambda j, k: (j, layer, 0, 0))
        out_shape = jax.ShapeDtypeStruct((N_DEV, layers, Ka, nb), F32)

        def body(a_ref, g_ref, *rest):
            o_ref = rest[-1]
            part = lax.dot_general(a_ref[...], g_ref[...], TN_DIMS, preferred_element_type=F32)

            @pl.when(pl.program_id(1) == 0)
            def _():
                o_ref[...] = part

            @pl.when(pl.program_id(1) > 0)
            def _():
                o_ref[...] += part
    else:
        kb = Ka // N_DEV
        grid = (N_DEV // group, nl)
        in_specs = [pl.BlockSpec((tl, group * kb), lambda j, k: (k, j)), pl.BlockSpec((tl, N), lambda j, k: (k, 0))]
        out_spec = pl.BlockSpec((group, None, kb, N), lambda j, k: (j, layer, 0, 0))
        out_shape = jax.ShapeDtypeStruct((N_DEV, layers, kb, N), F32)

        def body(a_ref, g_ref, *rest):
            o_ref = rest[-1]
            gv = g_ref[...]
            for t in range(group):
                part = lax.dot_general(a_ref[:, t * kb:(t + 1) * kb], gv, TN_DIMS, preferred_element_type=F32)

                @pl.when(pl.program_id(1) == 0)
                def _():
                    o_ref[t] = part

                @pl.when(pl.program_id(1) > 0)
                def _():
                    o_ref[t] += part

    if into is None:
        return pl.pallas_call(
            body, name=name, grid=grid, in_specs=in_specs, out_specs=out_spec, out_shape=out_shape,
            compiler_params=_params("parallel", "arbitrary"),
        )(a, g)
    return pl.pallas_call(
        body, name=name, grid=grid, in_specs=in_specs + [ANY], out_specs=out_spec, out_shape=out_shape,
        input_output_aliases={2: 0}, compiler_params=_params("parallel", "arbitrary"),
    )(a, g, into)


def _rstd(x):
    return lax.rsqrt(jnp.mean(x * x, axis=-1, keepdims=True) + RMS_EPS)


def _norm_fwd(name, x, gain):
    L, D = x.shape
    tr = _row_tile(L, 640)

    def body(x_ref, g_ref, o_ref):
        xv = x_ref[...]
        o_ref[...] = (xv * _rstd(xv) * g_ref[...]).astype(BF16)

    return pl.pallas_call(
        body, name=name, grid=(L // tr,),
        in_specs=[pl.BlockSpec((tr, D), lambda i: (i, 0)), pl.BlockSpec((1, D), lambda i: (0, 0))],
        out_specs=pl.BlockSpec((tr, D), lambda i: (i, 0)),
        out_shape=jax.ShapeDtypeStruct((L, D), BF16),
        compiler_params=_params("parallel"),
    )(x, gain)


def _res_norm(name, h, branch, g_post, g_pre):
    L, D = h.shape
    tr = _row_tile(L, 640)
    npre = len(g_pre)

    def body(h_ref, b_ref, gp_ref, *rest):
        bv = b_ref[...]
        hn = h_ref[...] + bv * _rstd(bv) * gp_ref[...]
        rest[npre][...] = hn
        if npre:
            xh = hn * _rstd(hn)
            for t in range(npre):
                rest[npre + 1 + t][...] = (xh * rest[t][...]).astype(BF16)

    row = pl.BlockSpec((tr, D), lambda i: (i, 0))
    gain = pl.BlockSpec((1, D), lambda i: (0, 0))
    return pl.pallas_call(
        body, name=name, grid=(L // tr,),
        in_specs=[row, row, gain] + [gain] * npre,
        out_specs=[row] * (1 + npre),
        out_shape=[jax.ShapeDtypeStruct((L, D), F32)] + [jax.ShapeDtypeStruct((L, D), BF16)] * npre,
        compiler_params=_params("parallel"),
    )(h, branch, g_post, *g_pre)


def _norm_bwd(name, dres, stream=None, pre=(), post=None):
    L, D = dres.shape
    tr = _row_tile(L, 320)
    npre = len(pre)
    has_post = post is not None
    n_in = 1 + (1 + 2 * npre if npre else 0) + (2 if has_post else 0)
    n_dg = npre + (1 if has_post else 0)

    def body(*refs):
        ins, outs = refs[:n_in], refs[n_in:]
        i = pl.program_id(0)
        dsum = ins[0][...]
        dgs = []
        pos = 1
        if npre:
            xv = ins[pos][...]
            pos += 1
            rs = _rstd(xv)
            xh = xv * rs
            dxh = None
            for t in range(npre):
                gv, dy = ins[pos][...], ins[pos + 1][...]
                pos += 2
                dgs.append(jnp.sum(dy * xh, axis=0, keepdims=True))
                term = dy * gv
                dxh = term if dxh is None else dxh + term
            dsum = dsum + rs * (dxh - xh * jnp.mean(dxh * xh, axis=-1, keepdims=True))
        outs[0][...] = dsum
        o = 1
        if has_post:
            bv, gp = ins[pos][...], ins[pos + 1][...]
            rs = _rstd(bv)
            bh = bv * rs
            dgs.append(jnp.sum(dsum * bh, axis=0, keepdims=True))
            dbh = dsum * gp
            outs[1][...] = (rs * (dbh - bh * jnp.mean(dbh * bh, axis=-1, keepdims=True))).astype(BF16)
            o = 2

        @pl.when(i == 0)
        def _():
            for t in range(n_dg):
                outs[o + t][...] = jnp.zeros((8, D), F32)

        for t in range(n_dg):
            outs[o + t][0:1, :] += dgs[t]

    row = pl.BlockSpec((tr, D), lambda i: (i, 0))
    gain = pl.BlockSpec((1, D), lambda i: (0, 0))
    acc = pl.BlockSpec((8, D), lambda i: (0, 0))
    args, in_specs = [dres], [row]
    if npre:
        args.append(stream)
        in_specs.append(row)
        for gv, dy in pre:
            args += [gv, dy]
            in_specs += [gain, row]
    if has_post:
        args += [post[0], post[1]]
        in_specs += [row, gain]
    out_specs = [row] + ([row] if has_post else []) + [acc] * n_dg
    out_shape = ([jax.ShapeDtypeStruct((L, D), F32)] + ([jax.ShapeDtypeStruct((L, D), BF16)] if has_post else [])
                 + [jax.ShapeDtypeStruct((8, D), F32)] * n_dg)
    return pl.pallas_call(
        body, name=name, grid=(L // tr,), in_specs=in_specs, out_specs=out_specs, out_shape=out_shape,
        compiler_params=_params("arbitrary"),
    )(*args)


def _conv_fwd(name, p, taps):
    L, D3 = p.shape
    D = D3 // 3
    nblk = L // BLK

    def body(p_ref, prev_ref, w_ref, y_ref, u_scr):
        i = pl.program_id(0)
        for cg in range(D // BLK):
            lo = cg * BLK
            b = p_ref[:, lo:lo + BLK]
            u = p_ref[:, D + lo:D + lo + BLK] * p_ref[:, 2 * D + lo:2 * D + lo + BLK]
            up = prev_ref[:, D + lo:D + lo + BLK] * prev_ref[:, 2 * D + lo:2 * D + lo + BLK]
            u_scr[0:8, :] = jnp.where(i > 0, up, 0.0)
            u_scr[8:8 + BLK, :] = u
            conv = (w_ref[0:1, lo:lo + BLK] * u_scr[6:6 + BLK, :] + w_ref[1:2, lo:lo + BLK] * u_scr[7:7 + BLK, :]
                    + w_ref[2:3, lo:lo + BLK] * u)
            y_ref[:, lo:lo + BLK] = (b * conv).astype(BF16)

    return pl.pallas_call(
        body, name=name, grid=(nblk,),
        in_specs=[pl.BlockSpec((BLK, D3), lambda i: (i, 0)),
                  pl.BlockSpec((8, D3), lambda i: (jnp.maximum(i * (BLK // 8) - 1, 0), 0)),
                  pl.BlockSpec((8, D), lambda i: (0, 0))],
        out_specs=pl.BlockSpec((BLK, D), lambda i: (i, 0)),
        out_shape=jax.ShapeDtypeStruct((L, D), BF16),
        scratch_shapes=[pltpu.VMEM((BLK + 8, BLK), F32)],
        compiler_params=_params("parallel"),
    )(p, p, taps)


def _conv_bwd(name, p, dy, taps):
    L, D3 = p.shape
    D = D3 // 3
    nblk = L // BLK
    last8 = L // 8 - 1

    def body(p_ref, prev_ref, next_ref, dy_ref, dyn_ref, w_ref, dp_ref, dw_ref, u_scr, d_scr):
        i = pl.program_id(0)

        @pl.when(i == 0)
        def _():
            dw_ref[...] = jnp.zeros((8, D), F32)

        for cg in range(D // BLK):
            lo = cg * BLK
            b = p_ref[:, lo:lo + BLK]
            c = p_ref[:, D + lo:D + lo + BLK]
            xv = p_ref[:, 2 * D + lo:2 * D + lo + BLK]
            u = c * xv
            up = prev_ref[:, D + lo:D + lo + BLK] * prev_ref[:, 2 * D + lo:2 * D + lo + BLK]
            u_scr[0:8, :] = jnp.where(i > 0, up, 0.0)
            u_scr[8:8 + BLK, :] = u
            u2 = u_scr[6:6 + BLK, :]
            u1 = u_scr[7:7 + BLK, :]
            w0, w1, w2 = w_ref[0:1, lo:lo + BLK], w_ref[1:2, lo:lo + BLK], w_ref[2:3, lo:lo + BLK]
            conv = w0 * u2 + w1 * u1 + w2 * u
            dyv = dy_ref[:, lo:lo + BLK]
            dconv = dyv * b
            dnext = dyn_ref[:, lo:lo + BLK] * next_ref[:, lo:lo + BLK]
            d_scr[0:BLK, :] = dconv
            d_scr[BLK:BLK + 8, :] = jnp.where(i < nblk - 1, dnext, 0.0)
            du = w2 * dconv + w1 * d_scr[1:1 + BLK, :] + w0 * d_scr[2:2 + BLK, :]
            dp_ref[:, lo:lo + BLK] = (dyv * conv).astype(BF16)
            dp_ref[:, D + lo:D + lo + BLK] = (du * xv).astype(BF16)
            dp_ref[:, 2 * D + lo:2 * D + lo + BLK] = (du * c).astype(BF16)
            dw_ref[0:1, lo:lo + BLK] += jnp.sum(dconv * u2, axis=0, keepdims=True)
            dw_ref[1:2, lo:lo + BLK] += jnp.sum(dconv * u1, axis=0, keepdims=True)
            dw_ref[2:3, lo:lo + BLK] += jnp.sum(dconv * u, axis=0, keepdims=True)

    halo_prev = lambda i: (jnp.maximum(i * (BLK // 8) - 1, 0), 0)
    halo_next = lambda i: (jnp.minimum((i + 1) * (BLK // 8), last8), 0)
    return pl.pallas_call(
        body, name=name, grid=(nblk,),
        in_specs=[pl.BlockSpec((BLK, D3), lambda i: (i, 0)), pl.BlockSpec((8, D3), halo_prev),
                  pl.BlockSpec((8, D3), halo_next), pl.BlockSpec((BLK, D), lambda i: (i, 0)),
                  pl.BlockSpec((8, D), halo_next), pl.BlockSpec((8, D), lambda i: (0, 0))],
        out_specs=[pl.BlockSpec((BLK, D3), lambda i: (i, 0)), pl.BlockSpec((8, D), lambda i: (0, 0))],
        out_shape=[jax.ShapeDtypeStruct((L, D3), BF16), jax.ShapeDtypeStruct((8, D), F32)],
        scratch_shapes=[pltpu.VMEM((BLK + 8, BLK), F32), pltpu.VMEM((BLK + 8, BLK), F32)],
        compiler_params=_params("arbitrary"),
    )(p, p, p, dy, dy, taps)


KB_STEP = 2
FAR = 1 << 30


def _stack_heads(x):
    lane = lax.broadcasted_iota(jnp.int32, x.shape, 1)
    zero = jnp.zeros_like(x)
    return jnp.concatenate([jnp.where(lane < HEAD_DIM, x, zero), jnp.where(lane >= HEAD_DIM, x, zero)], axis=0)


def _unstack_heads(x2):
    lane = lax.broadcasted_iota(jnp.int32, (BLK, BLK), 1)
    return jnp.where(lane < HEAD_DIM, x2[0:BLK], x2[BLK:2 * BLK])


def _key_block(kb, u):
    kbu = kb - u
    off = pl.multiple_of(jnp.maximum(kbu, 0) * BLK, BLK)
    return off, jnp.where(kbu >= 0, kbu * BLK, FAR)


def _attn_iotas(i):
    lane = lax.broadcasted_iota(jnp.int32, (2 * BLK, BLK), 1)
    row = lax.broadcasted_iota(jnp.int32, (2 * BLK, BLK), 0)
    krow = lax.broadcasted_iota(jnp.int32, (BLK, BLK), 0)
    klane = lax.broadcasted_iota(jnp.int32, (BLK, BLK), 1)
    return lane, i * BLK + (row & (BLK - 1)), krow, klane


def _score_block(q2, kblk, kbase, qpos, lane, rr, tri):
    z = lax.dot_general(q2, kblk, NT_DIMS, preferred_element_type=F32)
    kpos = kbase + lane
    vis = (kpos < qpos) & (kpos >= PAD)
    sp = jnp.log(1.0 + jnp.exp(-jnp.abs(z)))
    lb = jnp.minimum(z, 0.0) - sp
    lom = jnp.where(vis, lb - z, 0.0)
    hi = lom.astype(BF16)
    lo = (lom - hi.astype(F32)).astype(BF16)
    cs = jnp.dot(hi, tri, preferred_element_type=F32) + jnp.dot(lo, tri, preferred_element_type=F32)
    a = jnp.where(vis, jnp.exp(lb + rr + cs), 0.0)
    return vis, lb, lom, a


def _attn_fwd(name, q, k, v):
    L, D = q.shape

    def body(q_ref, k_ref, v_ref, of_ref, ob_ref, acc_ref, r_ref):
        i = pl.program_id(1)
        lane, qpos, krow, klane = _attn_iotas(i)
        tri = (krow > klane).astype(BF16)
        q2 = _stack_heads(q_ref[...])
        acc_ref[...] = jnp.zeros((2 * BLK, BLK), F32)
        r_ref[...] = jnp.zeros((2 * BLK, 1), F32)

        def step(carry):
            kb, _ = carry
            rr = r_ref[...]
            upd = None
            for u in range(KB_STEP):
                off, kbase = _key_block(kb, u)
                _, _, lom, a = _score_block(q2, k_ref[pl.ds(off, BLK), :], kbase, qpos, lane, rr, tri)
                vblk = v_ref[pl.ds(off, BLK), :]
                ahi = a.astype(BF16)
                alo = (a - ahi.astype(F32)).astype(BF16)
                part = jnp.dot(ahi, vblk, preferred_element_type=F32) + jnp.dot(alo, vblk, preferred_element_type=F32)
                upd = part if upd is None else upd + part
                rr = rr + jnp.sum(lom, axis=1, keepdims=True)
            acc_ref[...] += upd
            r_ref[...] = rr
            return kb - KB_STEP, (jnp.max(rr) < LOG_ZERO).astype(jnp.int32)

        lax.while_loop(lambda c: (c[0] >= 0) & (c[1] == 0), step, (i, jnp.int32(0)))
        o = _unstack_heads(acc_ref[...])
        of_ref[...] = o
        ob_ref[...] = o.astype(BF16)

    qspec = pl.BlockSpec((BLK, BLK), lambda h, i: (i, h))
    kvspec = pl.BlockSpec((L, BLK), lambda h, i: (0, h))
    return pl.pallas_call(
        body, name=name, grid=(D // BLK, L // BLK),
        in_specs=[qspec, kvspec, kvspec], out_specs=[qspec, qspec],
        out_shape=[jax.ShapeDtypeStruct((L, D), F32), jax.ShapeDtypeStruct((L, D), BF16)],
        scratch_shapes=[pltpu.VMEM((2 * BLK, BLK), F32), pltpu.VMEM((2 * BLK, 1), F32)],
        compiler_params=_params("parallel", "arbitrary"),
    )(q, k, v)


def _attn_bwd(name, q, k, v, o, do):
    L, D = q.shape

    def body(q_ref, k_ref, v_ref, o_ref, do_ref, dq_ref, dk_ref, dv_ref, acc_ref, r_ref, c_ref):
        i = pl.program_id(1)

        @pl.when(i == 0)
        def _():
            dk_ref[...] = jnp.zeros((L, BLK), F32)
            dv_ref[...] = jnp.zeros((L, BLK), F32)

        lane, qpos, krow, klane = _attn_iotas(i)
        tri = (krow > klane).astype(BF16)
        tri_incl = (krow >= klane).astype(BF16)
        q2 = _stack_heads(q_ref[...])
        do2 = _stack_heads(do_ref[...].astype(BF16))
        total = jnp.sum(do2.astype(F32) * jnp.concatenate([o_ref[...]] * 2, axis=0), axis=1, keepdims=True)
        acc_ref[...] = jnp.zeros((2 * BLK, BLK), F32)
        r_ref[...] = jnp.zeros((2 * BLK, 1), F32)
        c_ref[...] = jnp.zeros((2 * BLK, 1), F32)

        def step(carry):
            kb, _ = carry
            rr = r_ref[...]
            cc = c_ref[...]
            upd = None
            for u in range(KB_STEP):
                off, kbase = _key_block(kb, u)
                kblk = k_ref[pl.ds(off, BLK), :]
                vis, lb, lom, a = _score_block(q2, kblk, kbase, qpos, lane, rr, tri)
                da = lax.dot_general(do2, v_ref[pl.ds(off, BLK), :], NT_DIMS, preferred_element_type=F32)
                g = da * a
                ghi = g.astype(BF16)
                glo = (g - ghi.astype(F32)).astype(BF16)
                sfx = (jnp.dot(ghi, tri_incl, preferred_element_type=F32)
                       + jnp.dot(glo, tri_incl, preferred_element_type=F32))
                before = total - cc - sfx
                beta = jnp.exp(lb)
                dz = jnp.where(vis, g * (1.0 - beta) - beta * before, 0.0).astype(BF16)
                part = jnp.dot(dz, kblk, preferred_element_type=F32)
                upd = part if upd is None else upd + part
                dk_ref[pl.ds(off, BLK), :] += lax.dot_general(dz, q2, TN_DIMS, preferred_element_type=F32)
                dv_ref[pl.ds(off, BLK), :] += lax.dot_general(a.astype(BF16), do2, TN_DIMS,
                                                              preferred_element_type=F32)
                cc = cc + jnp.sum(g, axis=1, keepdims=True)
                rr = rr + jnp.sum(lom, axis=1, keepdims=True)
            acc_ref[...] += upd
            c_ref[...] = cc
            r_ref[...] = rr
            return kb - KB_STEP, (jnp.max(rr) < LOG_ZERO).astype(jnp.int32)

        lax.while_loop(lambda c: (c[0] >= 0) & (c[1] == 0), step, (i, jnp.int32(0)))
        dq_ref[...] = (_unstack_heads(acc_ref[...]) * ATTN_SCALE).astype(BF16)

    qspec = pl.BlockSpec((BLK, BLK), lambda h, i: (i, h))
    kvspec = pl.BlockSpec((L, BLK), lambda h, i: (0, h))
    return pl.pallas_call(
        body, name=name, grid=(D // BLK, L // BLK),
        in_specs=[qspec, kvspec, kvspec, qspec, qspec], out_specs=[qspec, kvspec, kvspec],
        out_shape=[jax.ShapeDtypeStruct((L, D), BF16), jax.ShapeDtypeStruct((L, D), F32),
                   jax.ShapeDtypeStruct((L, D), F32)],
        scratch_shapes=[pltpu.VMEM((2 * BLK, BLK), F32), pltpu.VMEM((2 * BLK, 1), F32),
                        pltpu.VMEM((2 * BLK, 1), F32)],
        compiler_params=_params("parallel", "arbitrary"),
    )(q, k, v, o, do)


def _loss_and_grad(name, h, target):
    L, D = h.shape
    first = (PAD + N_META) // BLK

    def body(h_ref, t_ref, sq_ref, dh_ref):
        i = pl.program_id(0)

        @pl.when(i == 0)
        def _():
            sq_ref[...] = jnp.zeros((8, BLK), F32)

        @pl.when(i < first)
        def _():
            dh_ref[...] = jnp.zeros((BLK, D), F32)

        @pl.when(i >= first)
        def _():
            err = h_ref[...] - t_ref[...]
            sq_ref[...] += jnp.sum(err * err)
            dh_ref[...] = err * (1.0 / D)

    return pl.pallas_call(
        body, name=name, grid=(L // BLK,),
        in_specs=[pl.BlockSpec((BLK, D), lambda i: (i, 0)),
                  pl.BlockSpec((BLK, D), lambda i: (jnp.maximum(i - first, 0), 0))],
        out_specs=[pl.BlockSpec((8, BLK), lambda i: (0, 0)), pl.BlockSpec((BLK, D), lambda i: (i, 0))],
        out_shape=[jax.ShapeDtypeStruct((8, BLK), F32), jax.ShapeDtypeStruct((L, D), F32)],
        compiler_params=_params("arbitrary"),
    )(h, target)


def _add_cast(name, a, b):
    L, D = a.shape
    tr = _row_tile(L, 640)

    def body(a_ref, b_ref, o_ref):
        o_ref[...] = (a_ref[...] + b_ref[...]).astype(BF16)

    row = pl.BlockSpec((tr, D), lambda i: (i, 0))
    return pl.pallas_call(body, name=name, grid=(L // tr,), in_specs=[row, row], out_specs=row,
                          out_shape=jax.ShapeDtypeStruct((L, D), BF16), compiler_params=_params("parallel"))(a, b)


def _adamw(w, g, m, v):
    m = ADAM_B1 * m + (1.0 - ADAM_B1) * g
    v = ADAM_B2 * v + (1.0 - ADAM_B2) * (g * g)
    m_hat = m / (1.0 - ADAM_B1 ** ADAM_STEP)
    v_hat = v / (1.0 - ADAM_B2 ** ADAM_STEP)
    delta = -ADAM_LR * (m_hat / (jnp.sqrt(v_hat) + ADAM_EPS) + ADAM_WD * w)
    return delta, m, v


def _adam_small(name, w, g, m, v):
    def body(w_ref, g_ref, m_ref, v_ref, d_ref, mo_ref, vo_ref):
        d, mn, vn = _adamw(w_ref[...], g_ref[...], m_ref[...], v_ref[...])
        d_ref[...] = d
        mo_ref[...] = mn
        vo_ref[...] = vn

    return pl.pallas_call(body, name=name, out_shape=[jax.ShapeDtypeStruct(w.shape, F32)] * 3)(w, g, m, v)


def _sum_slots(name, slots):
    def body(s_ref, o_ref):
        acc = s_ref[0]
        for d in range(1, N_DEV):
            acc = acc + s_ref[d]
        o_ref[...] = acc

    return pl.pallas_call(body, name=name, out_shape=jax.ShapeDtypeStruct(slots.shape[1:], F32))(slots)


def _pair_sum(name, g, recv, c_idx):
    _, R, C = g.shape
    tr = _row_tile(R, 512)

    def body(sel_ref, g_ref, r_ref, o_ref):
        o_ref[...] = (g_ref[...] + r_ref[...]).astype(BF16)

    return pl.pallas_call(
        body, name=name,
        grid_spec=pltpu.PrefetchScalarGridSpec(
            num_scalar_prefetch=1, grid=(4, R // tr),
            in_specs=[pl.BlockSpec((None, tr, C), lambda k, i, sel: (2 * k + sel[0], i, 0)),
                      pl.BlockSpec((None, tr, C), lambda k, i, sel: (k, i, 0))],
            out_specs=pl.BlockSpec((None, tr, C), lambda k, i, sel: (k, i, 0))),
        out_shape=jax.ShapeDtypeStruct((4, R, C), BF16),
        compiler_params=_params("parallel", "parallel"),
    )(c_idx, g, recv)


def _final_sum_adam(name, part, recv, chip_idx, w, m, v):
    R, C = w.shape
    tr = _row_tile(R, 512)

    def body(sel_ref, p_ref, r_ref, w_ref, m_ref, v_ref, g_ref, d_ref, mo_ref, vo_ref):
        g = ((p_ref[...].astype(F32) + r_ref[0].astype(F32)) + r_ref[1].astype(F32)) + r_ref[2].astype(F32)
        d, mn, vn = _adamw(w_ref[...], g, m_ref[...], v_ref[...])
        g_ref[...] = g
        d_ref[...] = d
        mo_ref[...] = mn
        vo_ref[...] = vn

    row = pl.BlockSpec((tr, C), lambda i, sel: (i, 0))
    return pl.pallas_call(
        body, name=name,
        grid_spec=pltpu.PrefetchScalarGridSpec(
            num_scalar_prefetch=1, grid=(R // tr,),
            in_specs=[pl.BlockSpec((None, tr, C), lambda i, sel: (sel[0], i, 0)),
                      pl.BlockSpec((3, tr, C), lambda i, sel: (0, i, 0)), row, row, row],
            out_specs=[row] * 4),
        out_shape=[jax.ShapeDtypeStruct((R, C), F32)] * 4,
        compiler_params=_params("parallel"),
    )(chip_idx, part, recv, w, m, v)


def _position():
    return lax.axis_index("x"), lax.axis_index("y"), lax.axis_index("c")


def _all_gather(name, shards):
    n = len(shards)

    def body(*refs):
        x_refs, o_refs = refs[:n], refs[n:2 * n]
        send_sems, recv_sems, local_sems = refs[2 * n:]
        x, y, c = _position()
        me, sibling = (x, y, c), (x, y, 1 - c)
        chips = [(1 - x, y), (x, 1 - y), (1 - x, 1 - y)]

        def copy(a, sem, block, to, src=None):
            dst = o_refs[a].at[4 * block[0] + 2 * block[1] + block[2]]
            return pltpu.make_async_remote_copy(
                src_ref=dst if src is None else src, dst_ref=dst,
                send_sem=send_sems.at[a, sem], recv_sem=recv_sems.at[a, sem], device_id=to, device_id_type=MESH)

        mine = [pltpu.make_async_copy(x_refs[a], o_refs[a].at[4 * x + 2 * y + c], local_sems.at[a]) for a in range(n)]
        for cp in mine:
            cp.start()
        first = []
        for a in range(n):
            first.append(copy(a, 0, me, sibling, src=x_refs[a]))
            first += [copy(a, 1 + j, me, (*chip, c), src=x_refs[a]) for j, chip in enumerate(chips)]
        for cp in first:
            cp.start()
        passed = []
        for j, chip in enumerate(chips):
            for a in range(n):
                copy(a, 1 + j, (*chip, c), me).wait_recv()
                fwd = copy(a, 4 + j, (*chip, c), sibling)
                fwd.start()
                passed.append(fwd)
        for a in range(n):
            copy(a, 0, sibling, me).wait_recv()
        for j, chip in enumerate(chips):
            for a in range(n):
                copy(a, 4 + j, (*chip, 1 - c), me).wait_recv()
        for cp in first + passed:
            cp.wait_send()
        for cp in mine:
            cp.wait()

    return pl.pallas_call(
        body, name=name, in_specs=[ANY] * n, out_specs=[ANY] * n,
        out_shape=[jax.ShapeDtypeStruct((N_DEV,) + s.shape, s.dtype) for s in shards],
        scratch_shapes=[pltpu.SemaphoreType.DMA((n, 7)), pltpu.SemaphoreType.DMA((n, 7)),
                        pltpu.SemaphoreType.DMA((n,))],
        compiler_params=pltpu.CompilerParams(has_side_effects=True),
    )(*shards)


def _exchange_sibling(name, grads):
    n = len(grads)

    def body(*refs):
        g_refs, r_refs = refs[:n], refs[n:2 * n]
        send_sems, recv_sems = refs[2 * n:]
        x, y, c = _position()
        copies = []
        for a in range(n):
            for k in range(4):
                copies.append(pltpu.make_async_remote_copy(
                    src_ref=g_refs[a].at[2 * k + (1 - c)], dst_ref=r_refs[a].at[k],
                    send_sem=send_sems.at[a, k], recv_sem=recv_sems.at[a, k],
                    device_id=(x, y, 1 - c), device_id_type=MESH))
        for cp in copies:
            cp.start()
        for cp in copies:
            cp.wait()

    return pl.pallas_call(
        body, name=name, in_specs=[ANY] * n, out_specs=[ANY] * n,
        out_shape=[jax.ShapeDtypeStruct((4,) + g.shape[1:], g.dtype) for g in grads],
        scratch_shapes=[pltpu.SemaphoreType.DMA((n, 4)), pltpu.SemaphoreType.DMA((n, 4))],
        compiler_params=pltpu.CompilerParams(has_side_effects=True),
    )(*grads)


def _exchange_chips(name, parts):
    n = len(parts)

    def body(*refs):
        p_refs, r_refs = refs[:n], refs[n:2 * n]
        send_sems, recv_sems = refs[2 * n:]
        x, y, c = _position()
        chips = [(1 - x, y), (x, 1 - y), (1 - x, 1 - y)]
        copies = []
        for a in range(n):
            for j, chip in enumerate(chips):
                copies.append(pltpu.make_async_remote_copy(
                    src_ref=p_refs[a].at[2 * chip[0] + chip[1]], dst_ref=r_refs[a].at[j],
                    send_sem=send_sems.at[a, j], recv_sem=recv_sems.at[a, j],
                    device_id=(*chip, c), device_id_type=MESH))
        for cp in copies:
            cp.start()
        for cp in copies:
            cp.wait()

    return pl.pallas_call(
        body, name=name, in_specs=[ANY] * n, out_specs=[ANY] * n,
        out_shape=[jax.ShapeDtypeStruct((3,) + p.shape[1:], p.dtype) for p in parts],
        scratch_shapes=[pltpu.SemaphoreType.DMA((n, 3)), pltpu.SemaphoreType.DMA((n, 3))],
        compiler_params=pltpu.CompilerParams(has_side_effects=True),
    )(*parts)


def kernel(x, meta_tokens, norm_gains, conv_in_proj, conv_w, conv_out_proj, kv_norm, w_k, w_v, w_q, w_o, mlp_w1, mlp_w2, loss_target, m_meta_tokens, m_norm_gains, m_conv_in_proj, m_conv_w, m_conv_out_proj, m_kv_norm, m_w_k, m_w_v, m_w_q, m_w_o, m_mlp_w1, m_mlp_w2, v_meta_tokens, v_norm_gains, v_conv_in_proj, v_conv_w, v_conv_out_proj, v_kv_norm, v_w_k, v_w_v, v_w_q, v_w_o, v_mlp_w1, v_mlp_w2):
    xi, target = x[0], loss_target[0]
    S, D = xi.shape
    L = PAD + N_META + S
    dsh = D // N_DEV
    px, py, pc = _position()
    dev = 4 * px + 2 * py + pc
    tm_big = _row_tile(L, 1664)
    tm_mid = _row_tile(L, 640)

    def pack_small(meta, gains, taps):
        return jnp.concatenate([meta, gains.reshape(DEPTH * 4, dsh), taps.reshape(N_A * 3, dsh),
                                jnp.zeros((2, dsh), F32)], axis=0)

    big_w = [conv_in_proj, conv_out_proj, w_k[None], w_v[None], w_q, w_o, mlp_w1, mlp_w2]
    big_m = [m_conv_in_proj, m_conv_out_proj, m_w_k[None], m_w_v[None], m_w_q, m_w_o, m_mlp_w1, m_mlp_w2]
    big_v = [v_conv_in_proj, v_conv_out_proj, v_w_k[None], v_w_v[None], v_w_q, v_w_o, v_mlp_w1, v_mlp_w2]
    small_w = pack_small(meta_tokens, norm_gains, conv_w)
    gathered = _all_gather("gather_weights", [small_w] + [w.astype(BF16) for w in big_w])
    small_full = gathered[0].transpose(1, 0, 2).reshape(40, D)
    win_g, wout_g, wk_g, wv_g, wq_g, wo_g, w1_g, w2_g = gathered[1:]
    meta_full = small_full[0:N_META]
    gain = lambda layer, n: small_full[N_META + 4 * layer + n][None]
    taps = [jnp.concatenate([small_full[32 + 3 * l:35 + 3 * l], jnp.zeros((5, D), F32)], axis=0) for l in range(N_A)]
    kvn = kv_norm[None]

    h = jnp.concatenate([jnp.zeros((PAD, D), F32), meta_full, xi], axis=0)
    n1 = _norm_fwd("norm_in", h, gain(0, 0))
    saved = []
    hk = k = v = None
    for layer in range(DEPTH):
        s = {"h0": h, "n1": n1}
        if layer < N_A:
            s["p"] = _mm_nn(f"conv_in{layer}", n1, win_g, layer, "col", [F32], tm_big)[0]
            s["y"] = _conv_fwd(f"conv{layer}", s["p"], taps[layer])
            mix = _mm_nn(f"conv_out{layer}", s["y"], wout_g, layer, "row", [F32], tm_big, tn=D)[0]
        else:
            j = layer - N_A
            if j == 0:
                k = _mm_nn("k_proj", hk, wk_g, 0, "row", [BF16], tm_big, tn=D)[0]
                v = _mm_nn("v_proj", hk, wv_g, 0, "row", [BF16], tm_big, tn=D)[0]
            s["q"] = _mm_nn(f"q_proj{j}", n1, wq_g, j, "row", [BF16], tm_big, tn=D,
                            epi=lambda acc: (acc * ATTN_SCALE,))[0]
            s["o"], s["ob"] = _attn_fwd(f"attn{j}", s["q"], k, v)
            mix = _mm_nn(f"o_proj{j}", s["ob"], wo_g, j, "row", [F32], tm_big, tn=D)[0]
        s["mix"] = mix
        s["h1"], s["n3"] = _res_norm(f"mix_norm{layer}", h, mix, gain(layer, 1), [gain(layer, 2)])
        s["r"], s["act"] = _mm_nn(f"mlp_up{layer}", s["n3"], w1_g, layer, "col", [BF16, BF16], tm_big,
                                  epi=lambda acc: (jnp.maximum(acc, 0.0), jnp.square(jnp.maximum(acc, 0.0))))
        s["ff"] = _mm_nn(f"mlp_down{layer}", s["act"], w2_g, layer, "row", [F32], tm_mid, tn=D)[0]
        pre = [] if layer == DEPTH - 1 else [gain(layer + 1, 0)] + ([kvn] if layer == N_A - 1 else [])
        outs = _res_norm(f"mlp_norm{layer}", s["h1"], s["ff"], gain(layer, 3), pre)
        h = outs[0]
        if pre:
            n1 = outs[1]
        if layer == N_A - 1:
            hk = outs[2]
        saved.append(s)

    sq, dh = _loss_and_grad("loss", h, target)
    loss = lax.psum(0.5 * sq[0, 0] / D, ("x", "y", "c"))

    g_gain = [[None] * 4 for _ in range(DEPTH)]
    g_taps = [None] * N_A
    g_w = {n: None for n in ("win", "wout", "wq", "wo", "w1", "w2")}
    dk_parts, dv_parts = [], []
    g_kvn = None
    s = saved[DEPTH - 1]
    dh, dff, g_gain[DEPTH - 1][3] = _norm_bwd("bwd_top", dh, post=(s["ff"], gain(DEPTH - 1, 3)))
    for layer in reversed(range(DEPTH)):
        s = saved[layer]
        da1 = _mm_nt(f"mlp_down_bwd{layer}", dff, w2_g, layer, "row", [BF16], tm_big,
                     epi=lambda acc, r: (acc * (2.0 * r.astype(F32)),), extras=(s["r"],))[0]
        g_w["w2"] = _mm_tn(f"mlp_w2_grad{layer}", s["act"], dff, "row", tm_big, layer=layer, layers=DEPTH,
                           into=g_w["w2"])
        g_w["w1"] = _mm_tn(f"mlp_w1_grad{layer}", s["n3"], da1, "col", tm_big, layer=layer, layers=DEPTH,
                           into=g_w["w1"])
        dn3 = _mm_nt(f"mlp_up_bwd{layer}", da1, w1_g, layer, "col", [F32], tm_mid, tn=D)[0]
        dh, dmix, g_gain[layer][2], g_gain[layer][1] = _norm_bwd(
            f"bwd_mid{layer}", dh, stream=s["h1"], pre=[(gain(layer, 2), dn3)], post=(s["mix"], gain(layer, 1)))
        pre = []
        if layer < N_A:
            dy = _mm_nt(f"conv_out_bwd{layer}", dmix, wout_g, layer, "row", [F32], tm_big, group=4)[0]
            g_w["wout"] = _mm_tn(f"conv_out_grad{layer}", s["y"], dmix, "row", tm_big, group=4, layer=layer,
                                 layers=N_A, into=g_w["wout"])
            dp, g_taps[layer] = _conv_bwd(f"conv_bwd{layer}", s["p"], dy, taps[layer])
            g_w["win"] = _mm_tn(f"conv_in_grad{layer}", s["n1"], dp, "col", tm_big, layer=layer, layers=N_A,
                                into=g_w["win"])
            dn1 = _mm_nt(f"conv_in_bwd{layer}", dp, win_g, layer, "col", [F32], tm_mid, tn=D)[0]
        else:
            j = layer - N_A
            do = _mm_nt(f"o_proj_bwd{j}", dmix, wo_g, j, "row", [F32], tm_big, group=4)[0]
            g_w["wo"] = _mm_tn(f"o_proj_grad{j}", s["ob"], dmix, "row", tm_big, group=4, layer=j,
                               layers=DEPTH - N_A, into=g_w["wo"])
            dq, dk_j, dv_j = _attn_bwd(f"attn_bwd{j}", s["q"], k, v, s["o"], do)
            dk_parts.append(dk_j)
            dv_parts.append(dv_j)
            g_w["wq"] = _mm_tn(f"q_proj_grad{j}", s["n1"], dq, "row", tm_big, group=4, layer=j,
                               layers=DEPTH - N_A, into=g_w["wq"])
            dn1 = _mm_nt(f"q_proj_bwd{j}", dq, wq_g, j, "row", [F32], tm_big, group=4)[0]
            if j == 0:
                dkb = _add_cast("dk_sum", dk_parts[0], dk_parts[1])
                dvb = _add_cast("dv_sum", dv_parts[0], dv_parts[1])
                g_wk = _mm_tn("k_proj_grad", hk, dkb, "row", tm_big, group=4)
                g_wv = _mm_tn("v_proj_grad", hk, dvb, "row", tm_big, group=4)
                dhk_k = _mm_nt("k_proj_bwd", dkb, wk_g, 0, "row", [F32], tm_big, group=4)[0]
                dhk = _mm_nt("v_proj_bwd", dvb, wv_g, 0, "row", [F32], tm_big, group=4,
                             epi=lambda acc, other: (acc + other,), extras=(dhk_k,))[0]
                pre = [(kvn, dhk)]
        pre = [(gain(layer, 0), dn1)] + pre
        if layer > 0:
            sp = saved[layer - 1]
            outs = _norm_bwd(f"bwd_in{layer}", dh, stream=s["h0"], pre=pre, post=(sp["ff"], gain(layer - 1, 3)))
            dh, dff = outs[0], outs[1]
            g_gain[layer][0] = outs[2]
            if len(pre) == 2:
                g_kvn = outs[3]
            g_gain[layer - 1][3] = outs[-1]
        else:
            dh, g_gain[0][0] = _norm_bwd("bwd_in0", dh, stream=s["h0"], pre=pre)

    grad_x = dh[PAD + N_META:][None]
    g_meta = dh[PAD:PAD + N_META]

    small_g = jnp.concatenate(
        [g_meta] + [g_gain[l][n][0:1] for l in range(DEPTH) for n in range(4)]
        + [g_taps[l][0:3] for l in range(N_A)] + [g_kvn[0:1], jnp.zeros((1, D), F32)], axis=0)
    small_sum = _sum_slots("small_grad_sum", _all_gather("gather_small_grads", [small_g])[0])
    small_mine = lax.dynamic_slice_in_dim(small_sum, dev * dsh, dsh, axis=1)
    small_m = pack_small(m_meta_tokens, m_norm_gains, m_conv_w)
    small_v = pack_small(v_meta_tokens, v_norm_gains, v_conv_w)
    small_d, small_mn, small_vn = _adam_small("adam_small", small_w, small_mine, small_m, small_v)
    pad8 = lambda a: jnp.concatenate([a[None], jnp.zeros((7, D), F32)], axis=0)
    g_kv = small_sum[38]
    kv_d, kv_mn, kv_vn = _adam_small("adam_kv_norm", pad8(kv_norm), pad8(g_kv), pad8(m_kv_norm), pad8(v_kv_norm))

    def unpack_small(a):
        return (a[0:N_META], a[N_META:N_META + 16].reshape(DEPTH, 4, dsh), a[32:38].reshape(N_A, 3, dsh))

    big_g = [g_w["win"], g_w["wout"], g_wk, g_wv, g_w["wq"], g_w["wo"], g_w["w1"], g_w["w2"]]
    flat = lambda a, lead: a.reshape(lead + (-1, a.shape[-1]))
    big_g = [flat(g, (N_DEV,)) for g in big_g]
    from_sibling = _exchange_sibling("grads_to_sibling", big_g)
    c_idx = jnp.reshape(pc, (1,)).astype(jnp.int32)
    chip_idx = jnp.reshape(2 * px + py, (1,)).astype(jnp.int32)
    names = ["conv_in", "conv_out", "w_k", "w_v", "w_q", "w_o", "mlp_w1", "mlp_w2"]
    parts = [_pair_sum(f"pair_sum_{nm}", g, r, c_idx) for nm, g, r in zip(names, big_g, from_sibling)]
    from_chips = _exchange_chips("grads_to_chips", parts)
    big_out = []
    for nm, part, recv, w, m, v2 in zip(names, parts, from_chips, big_w, big_m, big_v):
        res = _final_sum_adam(f"adam_{nm}", part, recv, chip_idx, flat(w, ()), flat(m, ()), flat(v2, ()))
        shape = w.shape[1:] if nm in ("w_k", "w_v") else w.shape
        big_out.append([r.reshape(shape) for r in res])

    def assemble(kind, small_parts, kv_part):
        meta_p, gains_p, taps_p = small_parts
        b = [o[kind] for o in big_out]
        return [meta_p, gains_p, b[0], taps_p, b[1], kv_part, b[2], b[3], b[4], b[5], b[6], b[7]]

    grads = assemble(0, unpack_small(small_mine), g_kv)
    deltas = assemble(1, unpack_small(small_d), kv_d[0])
    new_m = assemble(2, unpack_small(small_mn), kv_mn[0])
    new_v = assemble(3, unpack_small(small_vn), kv_vn[0])
    return (loss, grad_x, *grads, *deltas, *new_m, *new_v)
```

```python
import functools

import jax
import jax.numpy as jnp
from jax import lax
from jax.experimental import pallas as pl
from jax.experimental.pallas import tpu as pltpu

F32 = jnp.float32
BF16 = jnp.bfloat16
MESH = pl.DeviceIdType.MESH

N_DEV = 8
N_META = 16
BLK = 128
PAD = (-N_META) % BLK
HEAD_DIM = 64
DEPTH = 4
N_A = 2
RMS_EPS = 1e-6
ATTN_SCALE = HEAD_DIM ** -0.5
LOG_ZERO = -105.0
ADAM_LR, ADAM_B1, ADAM_B2, ADAM_EPS, ADAM_WD, ADAM_STEP = 0.001, 0.9, 0.999, 1e-08, 0.01, 10
VMEM_LIMIT = 56 * 2 ** 20

NT_DIMS = (((1,), (1,)), ((), ()))
TN_DIMS = (((0,), (0,)), ((), ()))
ANY = pl.BlockSpec(memory_space=pl.ANY)


def _params(*sem):
    return pltpu.CompilerParams(dimension_semantics=sem, vmem_limit_bytes=VMEM_LIMIT)


def _row_tile(rows, pref):
    best = None
    for t in range(16, min(rows, pref) + 1, 16):
        if rows % t == 0:
            best = t
    assert best is not None, (rows, pref)
    return best


def _mm_nn(name, a, w, layer, layout, out_dtypes, tm, tn=None, epi=None, extras=()):
    M, K = a.shape
    if layout == "col":
        nb = w.shape[3]
        N, tn = N_DEV * nb, nb
        w_spec = pl.BlockSpec((None, None, K, nb), lambda i, j: (j, layer, 0, 0))
    else:
        kb, N = w.shape[2], w.shape[3]
        assert N_DEV * kb == K
        w_spec = pl.BlockSpec((N_DEV, None, kb, tn), lambda i, j: (0, layer, 0, j))
    ne = len(extras)

    def body(a_ref, w_ref, *rest):
        wv = w_ref[...]
        if layout == "row":
            wv = wv.reshape(K, tn)
        acc = jnp.dot(a_ref[...], wv, preferred_element_type=F32)
        vals = epi(acc, *[r[...] for r in rest[:ne]]) if epi else (acc,)
        for o_ref, val in zip(rest[ne:], vals):
            o_ref[...] = val.astype(o_ref.dtype)

    tile = pl.BlockSpec((tm, tn), lambda i, j: (i, j))
    return pl.pallas_call(
        body, name=name, grid=(M // tm, N // tn),
        in_specs=[pl.BlockSpec((tm, K), lambda i, j: (i, 0)), w_spec] + [tile] * ne,
        out_specs=[tile] * len(out_dtypes),
        out_shape=[jax.ShapeDtypeStruct((M, N), d) for d in out_dtypes],
        compiler_params=_params("parallel", "parallel"),
    )(a, w, *extras)


def _mm_nt(name, a, w, layer, layout, out_dtypes, tm, tn=None, group=1, epi=None, extras=()):
    M, Nc = a.shape
    ne = len(extras)
    if layout == "row":
        kb = w.shape[2]
        No, tno = N_DEV * kb, group * kb
        w_spec = pl.BlockSpec((group, None, kb, Nc), lambda i, j: (j, layer, 0, 0))
        a_spec = pl.BlockSpec((tm, Nc), lambda i, j: (i, 0))
        grid = (M // tm, N_DEV // group)
        scratch = []

        def body(a_ref, w_ref, *rest):
            acc = lax.dot_general(a_ref[...], w_ref[...].reshape(tno, Nc), NT_DIMS, preferred_element_type=F32)
            vals = epi(acc, *[r[...] for r in rest[:ne]]) if epi else (acc,)
            for o_ref, val in zip(rest[ne:], vals):
                o_ref[...] = val.astype(o_ref.dtype)
    else:
        No, nb = w.shape[2], w.shape[3]
        assert N_DEV * nb == Nc
        tno = tn
        w_spec = pl.BlockSpec((N_DEV, None, tn, nb), lambda i, j: (0, layer, j, 0))
        a_spec = pl.BlockSpec((tm, Nc), lambda i, j: (i, 0))
        grid = (M // tm, No // tn)
        scratch = [pltpu.VMEM((tm, tn), F32)]

        def body(a_ref, w_ref, *rest):
            acc_ref = rest[-1]
            for d in range(N_DEV):
                part = lax.dot_general(a_ref[:, d * nb:(d + 1) * nb], w_ref[d], NT_DIMS,
                                       preferred_element_type=F32)
                if d == 0:
                    acc_ref[...] = part
                else:
                    acc_ref[...] += part
            acc = acc_ref[...]
            vals = epi(acc, *[r[...] for r in rest[:ne]]) if epi else (acc,)
            for o_ref, val in zip(rest[ne:-1], vals):
                o_ref[...] = val.astype(o_ref.dtype)

    tile = pl.BlockSpec((tm, tno), lambda i, j: (i, j))
    return pl.pallas_call(
        body, name=name, grid=grid,
        in_specs=[a_spec, w_spec] + [tile] * ne,
        out_specs=[tile] * len(out_dtypes),
        out_shape=[jax.ShapeDtypeStruct((M, No), d) for d in out_dtypes],
        scratch_shapes=scratch,
        compiler_params=_params("parallel", "parallel"),
    )(a, w, *extras)


def _mm_tn(name, a, g, layout, tl, group=1, layer=0, layers=1, into=None):
    L, Ka = a.shape
    N = g.shape[1]
    nl = L // tl
    if layout == "col":
        nb = N // N_DEV
        grid = (N_DEV // group, nl)
        in_specs = [pl.BlockSpec((tl, Ka), lambda j, k: (k, 0)), pl.BlockSpec((tl, group * nb), lambda j, k: (k, j))]
        out_spec = pl.BlockSpec((group, None, Ka, nb), lambda j, k: (j, layer, 0, 0))
        out_shape = jax.ShapeDtypeStruct((N_DEV, layers, Ka, nb), F32)

        def body(a_ref, g_ref, *rest):
            o_ref = rest[-1]
            av = a_ref[...]
            for t in range(group):
                part = lax.dot_general(av, g_ref[:, t * nb:(t + 1) * nb], TN_DIMS, preferred_element_type=F32)

                @pl.when(pl.program_id(1) == 0)
                def _():
                    o_ref[t] = part

                @pl.when(pl.program_id(1) > 0)
                def _():
                    o_ref[t] += part
    else:
        kb = Ka // N_DEV
        grid = (N_DEV // group, nl)
        in_specs = [pl.BlockSpec((tl, group * kb), lambda j, k: (k, j)), pl.BlockSpec((tl, N), lambda j, k: (k, 0))]
        out_spec = pl.BlockSpec((group, None, kb, N), lambda j, k: (j, layer, 0, 0))
        out_shape = jax.ShapeDtypeStruct((N_DEV, layers, kb, N), F32)

        def body(a_ref, g_ref, *rest):
            o_ref = rest[-1]
            whole = lax.dot_general(a_ref[...], g_ref[...], TN_DIMS, preferred_element_type=F32)
            for t in range(group):
                part = whole[t * kb:(t + 1) * kb]

                @pl.when(pl.program_id(1) == 0)
                def _():
                    o_ref[t] = part

                @pl.when(pl.program_id(1) > 0)
                def _():
                    o_ref[t] += part

    if into is None:
        return pl.pallas_call(
            body, name=name, grid=grid, in_specs=in_specs, out_specs=out_spec, out_shape=out_shape,
            compiler_params=_params("parallel", "arbitrary"),
        )(a, g)
    return pl.pallas_call(
        body, name=name, grid=grid, in_specs=in_specs + [ANY], out_specs=out_spec, out_shape=out_shape,
        input_output_aliases={2: 0}, compiler_params=_params("parallel", "arbitrary"),
    )(a, g, into)


def _rstd(x):
    return lax.rsqrt(jnp.mean(x * x, axis=-1, keepdims=True) + RMS_EPS)


def _norm_fwd(name, x, gain):
    L, D = x.shape
    tr = _row_tile(L, 640)

    def body(x_ref, g_ref, o_ref):
        xv = x_ref[...]
        o_ref[...] = (xv * _rstd(xv) * g_ref[...]).astype(BF16)

    return pl.pallas_call(
        body, name=name, grid=(L // tr,),
        in_specs=[pl.BlockSpec((tr, D), lambda i: (i, 0)), pl.BlockSpec((1, D), lambda i: (0, 0))],
        out_specs=pl.BlockSpec((tr, D), lambda i: (i, 0)),
        out_shape=jax.ShapeDtypeStruct((L, D), BF16),
        compiler_params=_params("parallel"),
    )(x, gain)


def _res_norm(name, h, branch, g_post, g_pre):
    L, D = h.shape
    tr = _row_tile(L, 640)
    npre = len(g_pre)

    def body(h_ref, b_ref, gp_ref, *rest):
        bv = b_ref[...]
        hn = h_ref[...] + bv * _rstd(bv) * gp_ref[...]
        rest[npre][...] = hn
        if npre:
            xh = hn * _rstd(hn)
            for t in range(npre):
                rest[npre + 1 + t][...] = (xh * rest[t][...]).astype(BF16)

    row = pl.BlockSpec((tr, D), lambda i: (i, 0))
    gain = pl.BlockSpec((1, D), lambda i: (0, 0))
    return pl.pallas_call(
        body, name=name, grid=(L // tr,),
        in_specs=[row, row, gain] + [gain] * npre,
        out_specs=[row] * (1 + npre),
        out_shape=[jax.ShapeDtypeStruct((L, D), F32)] + [jax.ShapeDtypeStruct((L, D), BF16)] * npre,
        compiler_params=_params("parallel"),
    )(h, branch, g_post, *g_pre)


def _norm_bwd(name, dres, stream=None, pre=(), post=None):
    L, D = dres.shape
    tr = _row_tile(L, 320)
    npre = len(pre)
    has_post = post is not None
    n_in = 1 + (1 + 2 * npre if npre else 0) + (2 if has_post else 0)
    n_dg = npre + (1 if has_post else 0)

    def body(*refs):
        ins, outs = refs[:n_in], refs[n_in:]
        i = pl.program_id(0)
        dsum = ins[0][...]
        dgs = []
        pos = 1
        if npre:
            xv = ins[pos][...]
            pos += 1
            rs = _rstd(xv)
            xh = xv * rs
            dxh = None
            for t in range(npre):
                gv, dy = ins[pos][...], ins[pos + 1][...]
                pos += 2
                dgs.append(jnp.sum(dy * xh, axis=0, keepdims=True))
                term = dy * gv
                dxh = term if dxh is None else dxh + term
            dsum = dsum + rs * (dxh - xh * jnp.mean(dxh * xh, axis=-1, keepdims=True))
        outs[0][...] = dsum
        o = 1
        if has_post:
            bv, gp = ins[pos][...], ins[pos + 1][...]
            rs = _rstd(bv)
            bh = bv * rs
            dgs.append(jnp.sum(dsum * bh, axis=0, keepdims=True))
            dbh = dsum * gp
            outs[1][...] = (rs * (dbh - bh * jnp.mean(dbh * bh, axis=-1, keepdims=True))).astype(BF16)
            o = 2

        @pl.when(i == 0)
        def _():
            for t in range(n_dg):
                outs[o + t][...] = jnp.zeros((8, D), F32)

        for t in range(n_dg):
            outs[o + t][0:1, :] += dgs[t]

    row = pl.BlockSpec((tr, D), lambda i: (i, 0))
    gain = pl.BlockSpec((1, D), lambda i: (0, 0))
    acc = pl.BlockSpec((8, D), lambda i: (0, 0))
    args, in_specs = [dres], [row]
    if npre:
        args.append(stream)
        in_specs.append(row)
        for gv, dy in pre:
            args += [gv, dy]
            in_specs += [gain, row]
    if has_post:
        args += [post[0], post[1]]
        in_specs += [row, gain]
    out_specs = [row] + ([row] if has_post else []) + [acc] * n_dg
    out_shape = ([jax.ShapeDtypeStruct((L, D), F32)] + ([jax.ShapeDtypeStruct((L, D), BF16)] if has_post else [])
                 + [jax.ShapeDtypeStruct((8, D), F32)] * n_dg)
    return pl.pallas_call(
        body, name=name, grid=(L // tr,), in_specs=in_specs, out_specs=out_specs, out_shape=out_shape,
        compiler_params=_params("arbitrary"),
    )(*args)


def _conv_fwd(name, p, taps):
    L, D3 = p.shape
    D = D3 // 3
    nblk = L // BLK

    def body(p_ref, prev_ref, w_ref, y_ref, u_scr):
        i = pl.program_id(0)
        for cg in range(D // BLK):
            lo = cg * BLK
            b = p_ref[:, lo:lo + BLK]
            u = p_ref[:, D + lo:D + lo + BLK] * p_ref[:, 2 * D + lo:2 * D + lo + BLK]
            up = prev_ref[:, D + lo:D + lo + BLK] * prev_ref[:, 2 * D + lo:2 * D + lo + BLK]
            u_scr[0:8, :] = jnp.where(i > 0, up, 0.0)
            u_scr[8:8 + BLK, :] = u
            conv = (w_ref[0:1, lo:lo + BLK] * u_scr[6:6 + BLK, :] + w_ref[1:2, lo:lo + BLK] * u_scr[7:7 + BLK, :]
                    + w_ref[2:3, lo:lo + BLK] * u)
            y_ref[:, lo:lo + BLK] = (b * conv).astype(BF16)

    return pl.pallas_call(
        body, name=name, grid=(nblk,),
        in_specs=[pl.BlockSpec((BLK, D3), lambda i: (i, 0)),
                  pl.BlockSpec((8, D3), lambda i: (jnp.maximum(i * (BLK // 8) - 1, 0), 0)),
                  pl.BlockSpec((8, D), lambda i: (0, 0))],
        out_specs=pl.BlockSpec((BLK, D), lambda i: (i, 0)),
        out_shape=jax.ShapeDtypeStruct((L, D), BF16),
        scratch_shapes=[pltpu.VMEM((BLK + 8, BLK), F32)],
        compiler_params=_params("parallel"),
    )(p, p, taps)


def _conv_bwd(name, p, dy, taps):
    L, D3 = p.shape
    D = D3 // 3
    nblk = L // BLK
    last8 = L // 8 - 1

    def body(p_ref, prev_ref, next_ref, dy_ref, dyn_ref, w_ref, dp_ref, dw_ref, u_scr, d_scr):
        i = pl.program_id(0)

        @pl.when(i == 0)
        def _():
            dw_ref[...] = jnp.zeros((8, D), F32)

        for cg in range(D // BLK):
            lo = cg * BLK
            b = p_ref[:, lo:lo + BLK]
            c = p_ref[:, D + lo:D + lo + BLK]
            xv = p_ref[:, 2 * D + lo:2 * D + lo + BLK]
            u = c * xv
            up = prev_ref[:, D + lo:D + lo + BLK] * prev_ref[:, 2 * D + lo:2 * D + lo + BLK]
            u_scr[0:8, :] = jnp.where(i > 0, up, 0.0)
            u_scr[8:8 + BLK, :] = u
            u2 = u_scr[6:6 + BLK, :]
            u1 = u_scr[7:7 + BLK, :]
            w0, w1, w2 = w_ref[0:1, lo:lo + BLK], w_ref[1:2, lo:lo + BLK], w_ref[2:3, lo:lo + BLK]
            conv = w0 * u2 + w1 * u1 + w2 * u
            dyv = dy_ref[:, lo:lo + BLK]
            dconv = dyv * b
            dnext = dyn_ref[:, lo:lo + BLK] * next_ref[:, lo:lo + BLK]
            d_scr[0:BLK, :] = dconv
            d_scr[BLK:BLK + 8, :] = jnp.where(i < nblk - 1, dnext, 0.0)
            du = w2 * dconv + w1 * d_scr[1:1 + BLK, :] + w0 * d_scr[2:2 + BLK, :]
            dp_ref[:, lo:lo + BLK] = (dyv * conv).astype(BF16)
            dp_ref[:, D + lo:D + lo + BLK] = (du * xv).astype(BF16)
            dp_ref[:, 2 * D + lo:2 * D + lo + BLK] = (du * c).astype(BF16)
            dw_ref[0:1, lo:lo + BLK] += jnp.sum(dconv * u2, axis=0, keepdims=True)
            dw_ref[1:2, lo:lo + BLK] += jnp.sum(dconv * u1, axis=0, keepdims=True)
            dw_ref[2:3, lo:lo + BLK] += jnp.sum(dconv * u, axis=0, keepdims=True)

    halo_prev = lambda i: (jnp.maximum(i * (BLK // 8) - 1, 0), 0)
    halo_next = lambda i: (jnp.minimum((i + 1) * (BLK // 8), last8), 0)
    return pl.pallas_call(
        body, name=name, grid=(nblk,),
        in_specs=[pl.BlockSpec((BLK, D3), lambda i: (i, 0)), pl.BlockSpec((8, D3), halo_prev),
                  pl.BlockSpec((8, D3), halo_next), pl.BlockSpec((BLK, D), lambda i: (i, 0)),
                  pl.BlockSpec((8, D), halo_next), pl.BlockSpec((8, D), lambda i: (0, 0))],
        out_specs=[pl.BlockSpec((BLK, D3), lambda i: (i, 0)), pl.BlockSpec((8, D), lambda i: (0, 0))],
        out_shape=[jax.ShapeDtypeStruct((L, D3), BF16), jax.ShapeDtypeStruct((8, D), F32)],
        scratch_shapes=[pltpu.VMEM((BLK + 8, BLK), F32), pltpu.VMEM((BLK + 8, BLK), F32)],
        compiler_params=_params("arbitrary"),
    )(p, p, p, dy, dy, taps)


KB_STEP = 3
FAR = 1 << 30


def _stack_heads(x):
    lane = lax.broadcasted_iota(jnp.int32, x.shape, 1)
    zero = jnp.zeros_like(x)
    return jnp.concatenate([jnp.where(lane < HEAD_DIM, x, zero), jnp.where(lane >= HEAD_DIM, x, zero)], axis=0)


def _unstack_heads(x2):
    lane = lax.broadcasted_iota(jnp.int32, (BLK, BLK), 1)
    return jnp.where(lane < HEAD_DIM, x2[0:BLK], x2[BLK:2 * BLK])


def _key_block(kb, u):
    kbu = kb - u
    off = pl.multiple_of(jnp.maximum(kbu, 0) * BLK, BLK)
    return off, jnp.where(kbu >= 0, kbu * BLK, FAR)


def _attn_iotas(i):
    lane = lax.broadcasted_iota(jnp.int32, (2 * BLK, BLK), 1)
    row = lax.broadcasted_iota(jnp.int32, (2 * BLK, BLK), 0)
    krow = lax.broadcasted_iota(jnp.int32, (BLK, BLK), 0)
    klane = lax.broadcasted_iota(jnp.int32, (BLK, BLK), 1)
    return lane, i * BLK + (row & (BLK - 1)), krow, klane


def _score_block(q2, kblk, kbase, qpos, lane, rr, tri):
    z = lax.dot_general(q2, kblk, NT_DIMS, preferred_element_type=F32)
    kpos = kbase + lane
    vis = (kpos < qpos) & (kpos >= PAD)
    sp = jnp.log(1.0 + jnp.exp(-jnp.abs(z)))
    lb = jnp.minimum(z, 0.0) - sp
    lom = jnp.where(vis, lb - z, 0.0)
    hi = lom.astype(BF16)
    lo = (lom - hi.astype(F32)).astype(BF16)
    cs = jnp.dot(hi, tri, preferred_element_type=F32) + jnp.dot(lo, tri, preferred_element_type=F32)
    a = jnp.where(vis, jnp.exp(lb + rr + cs), 0.0)
    return vis, lb, lom, a


def _attn_fwd(name, q, k, v):
    L, D = q.shape

    def body(q_ref, k_ref, v_ref, of_ref, ob_ref, acc_ref, r_ref):
        i = pl.program_id(1)
        lane, qpos, krow, klane = _attn_iotas(i)
        tri = (krow > klane).astype(BF16)
        q2 = _stack_heads(q_ref[...])
        acc_ref[...] = jnp.zeros((2 * BLK, BLK), F32)
        r_ref[...] = jnp.zeros((2 * BLK, 1), F32)

        def step(carry):
            kb, _ = carry
            rr = r_ref[...]
            upd = None
            for u in range(KB_STEP):
                off, kbase = _key_block(kb, u)
                _, _, lom, a = _score_block(q2, k_ref[pl.ds(off, BLK), :], kbase, qpos, lane, rr, tri)
                part = jnp.dot(a.astype(BF16), v_ref[pl.ds(off, BLK), :], preferred_element_type=F32)
                upd = part if upd is None else upd + part
                rr = rr + jnp.sum(lom, axis=1, keepdims=True)
            acc_ref[...] += upd
            r_ref[...] = rr
            return kb - KB_STEP, (jnp.max(rr) < LOG_ZERO).astype(jnp.int32)

        lax.while_loop(lambda c: (c[0] >= 0) & (c[1] == 0), step, (i, jnp.int32(0)))
        o = _unstack_heads(acc_ref[...])
        of_ref[...] = o
        ob_ref[...] = o.astype(BF16)

    qspec = pl.BlockSpec((BLK, BLK), lambda h, i: (i, h))
    kvspec = pl.BlockSpec((L, BLK), lambda h, i: (0, h))
    return pl.pallas_call(
        body, name=name, grid=(D // BLK, L // BLK),
        in_specs=[qspec, kvspec, kvspec], out_specs=[qspec, qspec],
        out_shape=[jax.ShapeDtypeStruct((L, D), F32), jax.ShapeDtypeStruct((L, D), BF16)],
        scratch_shapes=[pltpu.VMEM((2 * BLK, BLK), F32), pltpu.VMEM((2 * BLK, 1), F32)],
        compiler_params=_params("parallel", "arbitrary"),
    )(q, k, v)


def _attn_bwd(name, q, k, v, o, do):
    L, D = q.shape

    def body(q_ref, k_ref, v_ref, o_ref, do_ref, dq_ref, dk_ref, dv_ref, acc_ref, r_ref, c_ref):
        i = pl.program_id(1)

        @pl.when(i == 0)
        def _():
            dk_ref[...] = jnp.zeros((L, BLK), F32)
            dv_ref[...] = jnp.zeros((L, BLK), F32)

        lane, qpos, krow, klane = _attn_iotas(i)
        tri = (krow > klane).astype(BF16)
        tri_incl = (krow >= klane).astype(BF16)
        q2 = _stack_heads(q_ref[...])
        do2 = _stack_heads(do_ref[...].astype(BF16))
        total = jnp.sum(do2.astype(F32) * jnp.concatenate([o_ref[...]] * 2, axis=0), axis=1, keepdims=True)
        acc_ref[...] = jnp.zeros((2 * BLK, BLK), F32)
        r_ref[...] = jnp.zeros((2 * BLK, 1), F32)
        c_ref[...] = jnp.zeros((2 * BLK, 1), F32)

        def step(carry):
            kb, _ = carry
            rr = r_ref[...]
            cc = c_ref[...]
            upd = None
            for u in range(KB_STEP):
                off, kbase = _key_block(kb, u)
                kblk = k_ref[pl.ds(off, BLK), :]
                vis, lb, lom, a = _score_block(q2, kblk, kbase, qpos, lane, rr, tri)
                da = lax.dot_general(do2, v_ref[pl.ds(off, BLK), :], NT_DIMS, preferred_element_type=F32)
                ab = a.astype(BF16)
                g = da * ab.astype(F32)
                ghi = g.astype(BF16)
                glo = (g - ghi.astype(F32)).astype(BF16)
                sfx = (jnp.dot(ghi, tri_incl, preferred_element_type=F32)
                       + jnp.dot(glo, tri_incl, preferred_element_type=F32))
                before = total - cc - sfx
                beta = jnp.exp(lb)
                dz = jnp.where(vis, g * (1.0 - beta) - beta * before, 0.0).astype(BF16)
                part = jnp.dot(dz, kblk, preferred_element_type=F32)
                upd = part if upd is None else upd + part
                dk_ref[pl.ds(off, BLK), :] += lax.dot_general(dz, q2, TN_DIMS, preferred_element_type=F32)
                dv_ref[pl.ds(off, BLK), :] += lax.dot_general(ab, do2, TN_DIMS, preferred_element_type=F32)
                cc = cc + jnp.sum(g, axis=1, keepdims=True)
                rr = rr + jnp.sum(lom, axis=1, keepdims=True)
            acc_ref[...] += upd
            c_ref[...] = cc
            r_ref[...] = rr
            return kb - KB_STEP, (jnp.max(rr) < LOG_ZERO).astype(jnp.int32)

        lax.while_loop(lambda c: (c[0] >= 0) & (c[1] == 0), step, (i, jnp.int32(0)))
        dq_ref[...] = (_unstack_heads(acc_ref[...]) * ATTN_SCALE).astype(BF16)

    qspec = pl.BlockSpec((BLK, BLK), lambda h, i: (i, h))
    kvspec = pl.BlockSpec((L, BLK), lambda h, i: (0, h))
    return pl.pallas_call(
        body, name=name, grid=(D // BLK, L // BLK),
        in_specs=[qspec, kvspec, kvspec, qspec, qspec], out_specs=[qspec, kvspec, kvspec],
        out_shape=[jax.ShapeDtypeStruct((L, D), BF16), jax.ShapeDtypeStruct((L, D), F32),
                   jax.ShapeDtypeStruct((L, D), F32)],
        scratch_shapes=[pltpu.VMEM((2 * BLK, BLK), F32), pltpu.VMEM((2 * BLK, 1), F32),
                        pltpu.VMEM((2 * BLK, 1), F32)],
        compiler_params=_params("parallel", "arbitrary"),
    )(q, k, v, o, do)


def _loss_and_grad(name, h, target):
    L, D = h.shape
    first = (PAD + N_META) // BLK

    def body(h_ref, t_ref, sq_ref, dh_ref):
        i = pl.program_id(0)

        @pl.when(i == 0)
        def _():
            sq_ref[...] = jnp.zeros((8, BLK), F32)

        @pl.when(i < first)
        def _():
            dh_ref[...] = jnp.zeros((BLK, D), F32)

        @pl.when(i >= first)
        def _():
            err = h_ref[...] - t_ref[...]
            sq_ref[...] += jnp.sum(err * err)
            dh_ref[...] = err * (1.0 / D)

    return pl.pallas_call(
        body, name=name, grid=(L // BLK,),
        in_specs=[pl.BlockSpec((BLK, D), lambda i: (i, 0)),
                  pl.BlockSpec((BLK, D), lambda i: (jnp.maximum(i - first, 0), 0))],
        out_specs=[pl.BlockSpec((8, BLK), lambda i: (0, 0)), pl.BlockSpec((BLK, D), lambda i: (i, 0))],
        out_shape=[jax.ShapeDtypeStruct((8, BLK), F32), jax.ShapeDtypeStruct((L, D), F32)],
        compiler_params=_params("arbitrary"),
    )(h, target)


def _add_cast(name, a, b):
    L, D = a.shape
    tr = _row_tile(L, 640)

    def body(a_ref, b_ref, o_ref):
        o_ref[...] = (a_ref[...] + b_ref[...]).astype(BF16)

    row = pl.BlockSpec((tr, D), lambda i: (i, 0))
    return pl.pallas_call(body, name=name, grid=(L // tr,), in_specs=[row, row], out_specs=row,
                          out_shape=jax.ShapeDtypeStruct((L, D), BF16), compiler_params=_params("parallel"))(a, b)


def _adamw(w, g, m, v):
    m = ADAM_B1 * m + (1.0 - ADAM_B1) * g
    v = ADAM_B2 * v + (1.0 - ADAM_B2) * (g * g)
    m_hat = m / (1.0 - ADAM_B1 ** ADAM_STEP)
    v_hat = v / (1.0 - ADAM_B2 ** ADAM_STEP)
    delta = -ADAM_LR * (m_hat / (jnp.sqrt(v_hat) + ADAM_EPS) + ADAM_WD * w)
    return delta, m, v


def _adam_small(name, w, g, m, v):
    def body(w_ref, g_ref, m_ref, v_ref, d_ref, mo_ref, vo_ref):
        d, mn, vn = _adamw(w_ref[...], g_ref[...], m_ref[...], v_ref[...])
        d_ref[...] = d
        mo_ref[...] = mn
        vo_ref[...] = vn

    return pl.pallas_call(body, name=name, out_shape=[jax.ShapeDtypeStruct(w.shape, F32)] * 3)(w, g, m, v)


def _sum_slots(name, slots):
    def body(s_ref, o_ref):
        acc = s_ref[0]
        for d in range(1, N_DEV):
            acc = acc + s_ref[d]
        o_ref[...] = acc

    return pl.pallas_call(body, name=name, out_shape=jax.ShapeDtypeStruct(slots.shape[1:], F32))(slots)


def _pair_sum(name, g, recv, c_idx):
    _, R, C = g.shape
    tr = _row_tile(R, 512)

    def body(sel_ref, g_ref, r_ref, o_ref):
        o_ref[...] = (g_ref[...] + r_ref[...]).astype(BF16)

    return pl.pallas_call(
        body, name=name,
        grid_spec=pltpu.PrefetchScalarGridSpec(
            num_scalar_prefetch=1, grid=(4, R // tr),
            in_specs=[pl.BlockSpec((None, tr, C), lambda k, i, sel: (2 * k + sel[0], i, 0)),
                      pl.BlockSpec((None, tr, C), lambda k, i, sel: (k, i, 0))],
            out_specs=pl.BlockSpec((None, tr, C), lambda k, i, sel: (k, i, 0))),
        out_shape=jax.ShapeDtypeStruct((4, R, C), BF16),
        compiler_params=_params("parallel", "parallel"),
    )(c_idx, g, recv)


def _final_sum_adam(name, part, recv, chip_idx, w, m, v):
    R, C = w.shape
    tr = _row_tile(R, 512)

    def body(sel_ref, p_ref, r_ref, w_ref, m_ref, v_ref, g_ref, d_ref, mo_ref, vo_ref):
        g = ((p_ref[...].astype(F32) + r_ref[0].astype(F32)) + r_ref[1].astype(F32)) + r_ref[2].astype(F32)
        d, mn, vn = _adamw(w_ref[...], g, m_ref[...], v_ref[...])
        g_ref[...] = g
        d_ref[...] = d
        mo_ref[...] = mn
        vo_ref[...] = vn

    row = pl.BlockSpec((tr, C), lambda i, sel: (i, 0))
    return pl.pallas_call(
        body, name=name,
        grid_spec=pltpu.PrefetchScalarGridSpec(
            num_scalar_prefetch=1, grid=(R // tr,),
            in_specs=[pl.BlockSpec((None, tr, C), lambda i, sel: (sel[0], i, 0)),
                      pl.BlockSpec((3, tr, C), lambda i, sel: (0, i, 0)), row, row, row],
            out_specs=[row] * 4),
        out_shape=[jax.ShapeDtypeStruct((R, C), F32)] * 4,
        compiler_params=_params("parallel"),
    )(chip_idx, part, recv, w, m, v)


def _position():
    return lax.axis_index("x"), lax.axis_index("y"), lax.axis_index("c")


def _all_gather(name, shards):
    n = len(shards)

    def body(*refs):
        x_refs, o_refs = refs[:n], refs[n:2 * n]
        send_sems, recv_sems, local_sems = refs[2 * n:]
        x, y, c = _position()
        me, sibling = (x, y, c), (x, y, 1 - c)
        chips = [(1 - x, y), (x, 1 - y), (1 - x, 1 - y)]

        def copy(a, sem, block, to, src=None):
            dst = o_refs[a].at[4 * block[0] + 2 * block[1] + block[2]]
            return pltpu.make_async_remote_copy(
                src_ref=dst if src is None else src, dst_ref=dst,
                send_sem=send_sems.at[a, sem], recv_sem=recv_sems.at[a, sem], device_id=to, device_id_type=MESH)

        mine = [pltpu.make_async_copy(x_refs[a], o_refs[a].at[4 * x + 2 * y + c], local_sems.at[a]) for a in range(n)]
        for cp in mine:
            cp.start()
        first = []
        for a in range(n):
            first.append(copy(a, 0, me, sibling, src=x_refs[a]))
            first += [copy(a, 1 + j, me, (*chip, c), src=x_refs[a]) for j, chip in enumerate(chips)]
        for cp in first:
            cp.start()
        passed = []
        for j, chip in enumerate(chips):
            for a in range(n):
                copy(a, 1 + j, (*chip, c), me).wait_recv()
                fwd = copy(a, 4 + j, (*chip, c), sibling)
                fwd.start()
                passed.append(fwd)
        for a in range(n):
            copy(a, 0, sibling, me).wait_recv()
        for j, chip in enumerate(chips):
            for a in range(n):
                copy(a, 4 + j, (*chip, 1 - c), me).wait_recv()
        for cp in first + passed:
            cp.wait_send()
        for cp in mine:
            cp.wait()

    return pl.pallas_call(
        body, name=name, in_specs=[ANY] * n, out_specs=[ANY] * n,
        out_shape=[jax.ShapeDtypeStruct((N_DEV,) + s.shape, s.dtype) for s in shards],
        scratch_shapes=[pltpu.SemaphoreType.DMA((n, 7)), pltpu.SemaphoreType.DMA((n, 7)),
                        pltpu.SemaphoreType.DMA((n,))],
        compiler_params=pltpu.CompilerParams(has_side_effects=True),
    )(*shards)


def _exchange_sibling(name, grads):
    n = len(grads)

    def body(*refs):
        g_refs, r_refs = refs[:n], refs[n:2 * n]
        send_sems, recv_sems = refs[2 * n:]
        x, y, c = _position()
        copies = []
        for a in range(n):
            for k in range(4):
                copies.append(pltpu.make_async_remote_copy(
                    src_ref=g_refs[a].at[2 * k + (1 - c)], dst_ref=r_refs[a].at[k],
                    send_sem=send_sems.at[a, k], recv_sem=recv_sems.at[a, k],
                    device_id=(x, y, 1 - c), device_id_type=MESH))
        for cp in copies:
            cp.start()
        for cp in copies:
            cp.wait()

    return pl.pallas_call(
        body, name=name, in_specs=[ANY] * n, out_specs=[ANY] * n,
        out_shape=[jax.ShapeDtypeStruct((4,) + g.shape[1:], g.dtype) for g in grads],
        scratch_shapes=[pltpu.SemaphoreType.DMA((n, 4)), pltpu.SemaphoreType.DMA((n, 4))],
        compiler_params=pltpu.CompilerParams(has_side_effects=True),
    )(*grads)


def _exchange_chips(name, parts):
    n = len(parts)

    def body(*refs):
        p_refs, r_refs = refs[:n], refs[n:2 * n]
        send_sems, recv_sems = refs[2 * n:]
        x, y, c = _position()
        chips = [(1 - x, y), (x, 1 - y), (1 - x, 1 - y)]
        copies = []
        for a in range(n):
            for j, chip in enumerate(chips):
                copies.append(pltpu.make_async_remote_copy(
                    src_ref=p_refs[a].at[2 * chip[0] + chip[1]], dst_ref=r_refs[a].at[j],
                    send_sem=send_sems.at[a, j], recv_sem=recv_sems.at[a, j],
                    device_id=(*chip, c), device_id_type=MESH))
        for cp in copies:
            cp.start()
        for cp in copies:
            cp.wait()

    return pl.pallas_call(
        body, name=name, in_specs=[ANY] * n, out_specs=[ANY] * n,
        out_shape=[jax.ShapeDtypeStruct((3,) + p.shape[1:], p.dtype) for p in parts],
        scratch_shapes=[pltpu.SemaphoreType.DMA((n, 3)), pltpu.SemaphoreType.DMA((n, 3))],
        compiler_params=pltpu.CompilerParams(has_side_effects=True),
    )(*parts)


def kernel(x, meta_tokens, norm_gains, conv_in_proj, conv_w, conv_out_proj, kv_norm, w_k, w_v, w_q, w_o, mlp_w1, mlp_w2, loss_target, m_meta_tokens, m_norm_gains, m_conv_in_proj, m_conv_w, m_conv_out_proj, m_kv_norm, m_w_k, m_w_v, m_w_q, m_w_o, m_mlp_w1, m_mlp_w2, v_meta_tokens, v_norm_gains, v_conv_in_proj, v_conv_w, v_conv_out_proj, v_kv_norm, v_w_k, v_w_v, v_w_q, v_w_o, v_mlp_w1, v_mlp_w2):
    xi, target = x[0], loss_target[0]
    S, D = xi.shape
    L = PAD + N_META + S
    dsh = D // N_DEV
    px, py, pc = _position()
    dev = 4 * px + 2 * py + pc
    tm_big = _row_tile(L, 1664)
    tm_mid = _row_tile(L, 640)

    def pack_small(meta, gains, taps):
        return jnp.concatenate([meta, gains.reshape(DEPTH * 4, dsh), taps.reshape(N_A * 3, dsh),
                                jnp.zeros((2, dsh), F32)], axis=0)

    big_w = [conv_in_proj, conv_out_proj, w_k[None], w_v[None], w_q, w_o, mlp_w1, mlp_w2]
    big_m = [m_conv_in_proj, m_conv_out_proj, m_w_k[None], m_w_v[None], m_w_q, m_w_o, m_mlp_w1, m_mlp_w2]
    big_v = [v_conv_in_proj, v_conv_out_proj, v_w_k[None], v_w_v[None], v_w_q, v_w_o, v_mlp_w1, v_mlp_w2]
    small_w = pack_small(meta_tokens, norm_gains, conv_w)
    gathered = _all_gather("gather_weights", [small_w] + [w.astype(BF16) for w in big_w])
    small_full = gathered[0].transpose(1, 0, 2).reshape(40, D)
    win_g, wout_g, wk_g, wv_g, wq_g, wo_g, w1_g, w2_g = gathered[1:]
    meta_full = small_full[0:N_META]
    gain = lambda layer, n: small_full[N_META + 4 * layer + n][None]
    taps = [jnp.concatenate([small_full[32 + 3 * l:35 + 3 * l], jnp.zeros((5, D), F32)], axis=0) for l in range(N_A)]
    kvn = kv_norm[None]

    h = jnp.concatenate([jnp.zeros((PAD, D), F32), meta_full, xi], axis=0)
    n1 = _norm_fwd("norm_in", h, gain(0, 0))
    saved = []
    hk = k = v = None
    for layer in range(DEPTH):
        s = {"h0": h, "n1": n1}
        if layer < N_A:
            s["p"] = _mm_nn(f"conv_in{layer}", n1, win_g, layer, "col", [F32], tm_big)[0]
            s["y"] = _conv_fwd(f"conv{layer}", s["p"], taps[layer])
            mix = _mm_nn(f"conv_out{layer}", s["y"], wout_g, layer, "row", [F32], tm_big, tn=D)[0]
        else:
            j = layer - N_A
            if j == 0:
                k = _mm_nn("k_proj", hk, wk_g, 0, "row", [BF16], tm_big, tn=D)[0]
                v = _mm_nn("v_proj", hk, wv_g, 0, "row", [BF16], tm_big, tn=D)[0]
            s["q"] = _mm_nn(f"q_proj{j}", n1, wq_g, j, "row", [BF16], tm_big, tn=D,
                            epi=lambda acc: (acc * ATTN_SCALE,))[0]
            s["o"], s["ob"] = _attn_fwd(f"attn{j}", s["q"], k, v)
            mix = _mm_nn(f"o_proj{j}", s["ob"], wo_g, j, "row", [F32], tm_big, tn=D)[0]
        s["mix"] = mix
        s["h1"], s["n3"] = _res_norm(f"mix_norm{layer}", h, mix, gain(layer, 1), [gain(layer, 2)])
        s["act"] = _mm_nn(f"mlp_up{layer}", s["n3"], w1_g, layer, "col", [BF16], tm_big,
                          epi=lambda acc: (jnp.square(jnp.maximum(acc, 0.0)),))[0]
        s["ff"] = _mm_nn(f"mlp_down{layer}", s["act"], w2_g, layer, "row", [F32], tm_mid, tn=D)[0]
        pre = [] if layer == DEPTH - 1 else [gain(layer + 1, 0)] + ([kvn] if layer == N_A - 1 else [])
        outs = _res_norm(f"mlp_norm{layer}", s["h1"], s["ff"], gain(layer, 3), pre)
        h = outs[0]
        if pre:
            n1 = outs[1]
        if layer == N_A - 1:
            hk = outs[2]
        saved.append(s)

    sq, dh = _loss_and_grad("loss", h, target)
    loss = lax.psum(0.5 * sq[0, 0] / D, ("x", "y", "c"))

    g_gain = [[None] * 4 for _ in range(DEPTH)]
    g_taps = [None] * N_A
    g_w = {n: None for n in ("win", "wout", "wq", "wo", "w1", "w2")}
    dk_parts, dv_parts = [], []
    g_kvn = None
    s = saved[DEPTH - 1]
    dh, dff, g_gain[DEPTH - 1][3] = _norm_bwd("bwd_top", dh, post=(s["ff"], gain(DEPTH - 1, 3)))
    for layer in reversed(range(DEPTH)):
        s = saved[layer]
        da1 = _mm_nt(f"mlp_down_bwd{layer}", dff, w2_g, layer, "row", [BF16], tm_big,
                     epi=lambda acc, act: (acc * (2.0 * jnp.sqrt(act.astype(F32))),), extras=(s["act"],))[0]
        g_w["w2"] = _mm_tn(f"mlp_w2_grad{layer}", s["act"], dff, "row", tm_mid, group=4, layer=layer,
                           layers=DEPTH, into=g_w["w2"])
        g_w["w1"] = _mm_tn(f"mlp_w1_grad{layer}", s["n3"], da1, "col", tm_big, group=4, layer=layer,
                           layers=DEPTH, into=g_w["w1"])
        dn3 = _mm_nt(f"mlp_up_bwd{layer}", da1, w1_g, layer, "col", [F32], tm_mid, tn=D)[0]
        dh, dmix, g_gain[layer][2], g_gain[layer][1] = _norm_bwd(
            f"bwd_mid{layer}", dh, stream=s["h1"], pre=[(gain(layer, 2), dn3)], post=(s["mix"], gain(layer, 1)))
        pre = []
        if layer < N_A:
            dy = _mm_nt(f"conv_out_bwd{layer}", dmix, wout_g, layer, "row", [F32], tm_big, group=4)[0]
            g_w["wout"] = _mm_tn(f"conv_out_grad{layer}", s["y"], dmix, "row", tm_big, group=4, layer=layer,
                                 layers=N_A, into=g_w["wout"])
            dp, g_taps[layer] = _conv_bwd(f"conv_bwd{layer}", s["p"], dy, taps[layer])
            g_w["win"] = _mm_tn(f"conv_in_grad{layer}", s["n1"], dp, "col", tm_big, group=4, layer=layer, layers=N_A,
                                into=g_w["win"])
            dn1 = _mm_nt(f"conv_in_bwd{layer}", dp, win_g, layer, "col", [F32], tm_mid, tn=D)[0]
        else:
            j = layer - N_A
            do = _mm_nt(f"o_proj_bwd{j}", dmix, wo_g, j, "row", [F32], tm_big, group=4)[0]
            g_w["wo"] = _mm_tn(f"o_proj_grad{j}", s["ob"], dmix, "row", tm_big, group=4, layer=j,
                               layers=DEPTH - N_A, into=g_w["wo"])
            dq, dk_j, dv_j = _attn_bwd(f"attn_bwd{j}", s["q"], k, v, s["o"], do)
            dk_parts.append(dk_j)
            dv_parts.append(dv_j)
            g_w["wq"] = _mm_tn(f"q_proj_grad{j}", s["n1"], dq, "row", tm_big, group=4, layer=j,
                               layers=DEPTH - N_A, into=g_w["wq"])
            dn1 = _mm_nt(f"q_proj_bwd{j}", dq, wq_g, j, "row", [F32], tm_big, group=4)[0]
            if j == 0:
                dkb = _add_cast("dk_sum", dk_parts[0], dk_parts[1])
                dvb = _add_cast("dv_sum", dv_parts[0], dv_parts[1])
                g_wk = _mm_tn("k_proj_grad", hk, dkb, "row", tm_big, group=4)
                g_wv = _mm_tn("v_proj_grad", hk, dvb, "row", tm_big, group=4)
                dhk_k = _mm_nt("k_proj_bwd", dkb, wk_g, 0, "row", [F32], tm_big, group=4)[0]
                dhk = _mm_nt("v_proj_bwd", dvb, wv_g, 0, "row", [F32], tm_big, group=4,
                             epi=lambda acc, other: (acc + other,), extras=(dhk_k,))[0]
                pre = [(kvn, dhk)]
        pre = [(gain(layer, 0), dn1)] + pre
        if layer > 0:
            sp = saved[layer - 1]
            outs = _norm_bwd(f"bwd_in{layer}", dh, stream=s["h0"], pre=pre, post=(sp["ff"], gain(layer - 1, 3)))
            dh, dff = outs[0], outs[1]
            g_gain[layer][0] = outs[2]
            if len(pre) == 2:
                g_kvn = outs[3]
            g_gain[layer - 1][3] = outs[-1]
        else:
            dh, g_gain[0][0] = _norm_bwd("bwd_in0", dh, stream=s["h0"], pre=pre)

    grad_x = dh[PAD + N_META:][None]
    g_meta = dh[PAD:PAD + N_META]

    small_g = jnp.concatenate(
        [g_meta] + [g_gain[l][n][0:1] for l in range(DEPTH) for n in range(4)]
        + [g_taps[l][0:3] for l in range(N_A)] + [g_kvn[0:1], jnp.zeros((1, D), F32)], axis=0)
    small_sum = _sum_slots("small_grad_sum", _all_gather("gather_small_grads", [small_g])[0])
    small_mine = lax.dynamic_slice_in_dim(small_sum, dev * dsh, dsh, axis=1)
    small_m = pack_small(m_meta_tokens, m_norm_gains, m_conv_w)
    small_v = pack_small(v_meta_tokens, v_norm_gains, v_conv_w)
    small_d, small_mn, small_vn = _adam_small("adam_small", small_w, small_mine, small_m, small_v)
    pad8 = lambda a: jnp.concatenate([a[None], jnp.zeros((7, D), F32)], axis=0)
    g_kv = small_sum[38]
    kv_d, kv_mn, kv_vn = _adam_small("adam_kv_norm", pad8(kv_norm), pad8(g_kv), pad8(m_kv_norm), pad8(v_kv_norm))

    def unpack_small(a):
        return (a[0:N_META], a[N_META:N_META + 16].reshape(DEPTH, 4, dsh), a[32:38].reshape(N_A, 3, dsh))

    big_g = [g_w["win"], g_w["wout"], g_wk, g_wv, g_w["wq"], g_w["wo"], g_w["w1"], g_w["w2"]]
    flat = lambda a, lead: a.reshape(lead + (-1, a.shape[-1]))
    big_g = [flat(g, (N_DEV,)) for g in big_g]
    from_sibling = _exchange_sibling("grads_to_sibling", big_g)
    c_idx = jnp.reshape(pc, (1,)).astype(jnp.int32)
    chip_idx = jnp.reshape(2 * px + py, (1,)).astype(jnp.int32)
    names = ["conv_in", "conv_out", "w_k", "w_v", "w_q", "w_o", "mlp_w1", "mlp_w2"]
    parts = [_pair_sum(f"pair_sum_{nm}", g, r, c_idx) for nm, g, r in zip(names, big_g, from_sibling)]
    from_chips = _exchange_chips("grads_to_chips", parts)
    big_out = []
    for nm, part, recv, w, m, v2 in zip(names, parts, from_chips, big_w, big_m, big_v):
        res = _final_sum_adam(f"adam_{nm}", part, recv, chip_idx, flat(w, ()), flat(m, ()), flat(v2, ()))
        shape = w.shape[1:] if nm in ("w_k", "w_v") else w.shape
        big_out.append([r.reshape(shape) for r in res])

    def assemble(kind, small_parts, kv_part):
        meta_p, gains_p, taps_p = small_parts
        b = [o[kind] for o in big_out]
        return [meta_p, gains_p, b[0], taps_p, b[1], kv_part, b[2], b[3], b[4], b[5], b[6], b[7]]

    grads = assemble(0, unpack_small(small_mine), g_kv)
    deltas = assemble(1, unpack_small(small_d), kv_d[0])
    new_m = assemble(2, unpack_small(small_mn), kv_mn[0])
    new_v = assemble(3, unpack_small(small_vn), kv_vn[0])
    return (loss, grad_x, *grads, *deltas, *new_m, *new_v)
```

```python
import functools

import jax
import jax.numpy as jnp
from jax import lax
from jax.experimental import pallas as pl
from jax.experimental.pallas import tpu as pltpu

F32 = jnp.float32
BF16 = jnp.bfloat16
MESH = pl.DeviceIdType.MESH

N_DEV = 8
N_META = 16
BLK = 128
PAD = (-N_META) % BLK
HEAD_DIM = 64
DEPTH = 4
N_A = 2
RMS_EPS = 1e-6
ATTN_SCALE = HEAD_DIM ** -0.5
LOG_ZERO = -105.0
ADAM_LR, ADAM_B1, ADAM_B2, ADAM_EPS, ADAM_WD, ADAM_STEP = 0.001, 0.9, 0.999, 1e-08, 0.01, 10
VMEM_LIMIT = 56 * 2 ** 20

NT_DIMS = (((1,), (1,)), ((), ()))
TN_DIMS = (((0,), (0,)), ((), ()))
ANY = pl.BlockSpec(memory_space=pl.ANY)


def _params(*sem):
    return pltpu.CompilerParams(dimension_semantics=sem, vmem_limit_bytes=VMEM_LIMIT)


def _row_tile(rows, pref):
    best = None
    for t in range(16, min(rows, pref) + 1, 16):
        if rows % t == 0:
            best = t
    assert best is not None, (rows, pref)
    return best


def _mm_nn(name, a, w, layer, layout, out_dtypes, tm, tn=None, epi=None, extras=()):
    M, K = a.shape
    if layout == "col":
        nb = w.shape[3]
        N, tn = N_DEV * nb, nb
        w_spec = pl.BlockSpec((None, None, K, nb), lambda i, j: (j, layer, 0, 0))
    else:
        kb, N = w.shape[2], w.shape[3]
        assert N_DEV * kb == K
        w_spec = pl.BlockSpec((N_DEV, None, kb, tn), lambda i, j: (0, layer, 0, j))
    ne = len(extras)

    def body(a_ref, w_ref, *rest):
        wv = w_ref[...]
        if layout == "row":
            wv = wv.reshape(K, tn)
        acc = jnp.dot(a_ref[...], wv, preferred_element_type=F32)
        vals = epi(acc, *[r[...] for r in rest[:ne]]) if epi else (acc,)
        for o_ref, val in zip(rest[ne:], vals):
            o_ref[...] = val.astype(o_ref.dtype)

    tile = pl.BlockSpec((tm, tn), lambda i, j: (i, j))
    return pl.pallas_call(
        body, name=name, grid=(M // tm, N // tn),
        in_specs=[pl.BlockSpec((tm, K), lambda i, j: (i, 0)), w_spec] + [tile] * ne,
        out_specs=[tile] * len(out_dtypes),
        out_shape=[jax.ShapeDtypeStruct((M, N), d) for d in out_dtypes],
        compiler_params=_params("parallel", "parallel"),
    )(a, w, *extras)


def _mm_nt(name, a, w, layer, layout, out_dtypes, tm, tn=None, group=1, epi=None, extras=()):
    M, Nc = a.shape
    ne = len(extras)
    if layout == "row":
        kb = w.shape[2]
        No, tno = N_DEV * kb, group * kb
        w_spec = pl.BlockSpec((group, None, kb, Nc), lambda i, j: (j, layer, 0, 0))
        a_spec = pl.BlockSpec((tm, Nc), lambda i, j: (i, 0))
        grid = (M // tm, N_DEV // group)
        scratch = []

        def body(a_ref, w_ref, *rest):
            acc = lax.dot_general(a_ref[...], w_ref[...].reshape(tno, Nc), NT_DIMS, preferred_element_type=F32)
            vals = epi(acc, *[r[...] for r in rest[:ne]]) if epi else (acc,)
            for o_ref, val in zip(rest[ne:], vals):
                o_ref[...] = val.astype(o_ref.dtype)
    else:
        No, nb = w.shape[2], w.shape[3]
        assert N_DEV * nb == Nc
        tno = tn
        w_spec = pl.BlockSpec((N_DEV, None, tn, nb), lambda i, j: (0, layer, j, 0))
        a_spec = pl.BlockSpec((tm, Nc), lambda i, j: (i, 0))
        grid = (M // tm, No // tn)
        scratch = [pltpu.VMEM((tm, tn), F32)]

        def body(a_ref, w_ref, *rest):
            acc_ref = rest[-1]
            for d in range(N_DEV):
                part = lax.dot_general(a_ref[:, d * nb:(d + 1) * nb], w_ref[d], NT_DIMS,
                                       preferred_element_type=F32)
                if d == 0:
                    acc_ref[...] = part
                else:
                    acc_ref[...] += part
            acc = acc_ref[...]
            vals = epi(acc, *[r[...] for r in rest[:ne]]) if epi else (acc,)
            for o_ref, val in zip(rest[ne:-1], vals):
                o_ref[...] = val.astype(o_ref.dtype)

    tile = pl.BlockSpec((tm, tno), lambda i, j: (i, j))
    return pl.pallas_call(
        body, name=name, grid=grid,
        in_specs=[a_spec, w_spec] + [tile] * ne,
        out_specs=[tile] * len(out_dtypes),
        out_shape=[jax.ShapeDtypeStruct((M, No), d) for d in out_dtypes],
        scratch_shapes=scratch,
        compiler_params=_params("parallel", "parallel"),
    )(a, w, *extras)


def _mm_tn(name, a, g, layout, tl, group=1, layer=0, layers=1, into=None):
    L, Ka = a.shape
    N = g.shape[1]
    nl = L // tl
    if layout == "col":
        nb = N // N_DEV
        grid = (N_DEV // group, nl)
        in_specs = [pl.BlockSpec((tl, Ka), lambda j, k: (k, 0)), pl.BlockSpec((tl, group * nb), lambda j, k: (k, j))]
        out_spec = pl.BlockSpec((group, None, Ka, nb), lambda j, k: (j, layer, 0, 0))
        out_shape = jax.ShapeDtypeStruct((N_DEV, layers, Ka, nb), F32)

        def body(a_ref, g_ref, *rest):
            o_ref = rest[-1]
            av = a_ref[...]
            for t in range(group):
                part = lax.dot_general(av, g_ref[:, t * nb:(t + 1) * nb], TN_DIMS, preferred_element_type=F32)

                @pl.when(pl.program_id(1) == 0)
                def _():
                    o_ref[t] = part

                @pl.when(pl.program_id(1) > 0)
                def _():
                    o_ref[t] += part
    else:
        kb = Ka // N_DEV
        grid = (N_DEV // group, nl)
        in_specs = [pl.BlockSpec((tl, group * kb), lambda j, k: (k, j)), pl.BlockSpec((tl, N), lambda j, k: (k, 0))]
        out_spec = pl.BlockSpec((group, None, kb, N), lambda j, k: (j, layer, 0, 0))
        out_shape = jax.ShapeDtypeStruct((N_DEV, layers, kb, N), F32)

        def body(a_ref, g_ref, *rest):
            o_ref = rest[-1]
            whole = lax.dot_general(a_ref[...], g_ref[...], TN_DIMS, preferred_element_type=F32)
            for t in range(group):
                part = whole[t * kb:(t + 1) * kb]

                @pl.when(pl.program_id(1) == 0)
                def _():
                    o_ref[t] = part

                @pl.when(pl.program_id(1) > 0)
                def _():
                    o_ref[t] += part

    if into is None:
        return pl.pallas_call(
            body, name=name, grid=grid, in_specs=in_specs, out_specs=out_spec, out_shape=out_shape,
            compiler_params=_params("parallel", "arbitrary"),
        )(a, g)
    return pl.pallas_call(
        body, name=name, grid=grid, in_specs=in_specs + [ANY], out_specs=out_spec, out_shape=out_shape,
        input_output_aliases={2: 0}, compiler_params=_params("parallel", "arbitrary"),
    )(a, g, into)


def _rstd(x):
    return lax.rsqrt(jnp.mean(x * x, axis=-1, keepdims=True) + RMS_EPS)


def _norm_fwd(name, x, gain):
    L, D = x.shape
    tr = _row_tile(L, 640)

    def body(x_ref, g_ref, o_ref):
        xv = x_ref[...]
        o_ref[...] = (xv * _rstd(xv) * g_ref[...]).astype(BF16)

    return pl.pallas_call(
        body, name=name, grid=(L // tr,),
        in_specs=[pl.BlockSpec((tr, D), lambda i: (i, 0)), pl.BlockSpec((1, D), lambda i: (0, 0))],
        out_specs=pl.BlockSpec((tr, D), lambda i: (i, 0)),
        out_shape=jax.ShapeDtypeStruct((L, D), BF16),
        compiler_params=_params("parallel"),
    )(x, gain)


def _res_norm(name, h, branch, g_post, g_pre):
    L, D = h.shape
    tr = _row_tile(L, 640)
    npre = len(g_pre)

    def body(h_ref, b_ref, gp_ref, *rest):
        bv = b_ref[...]
        hn = h_ref[...] + bv * _rstd(bv) * gp_ref[...]
        rest[npre][...] = hn
        if npre:
            xh = hn * _rstd(hn)
            for t in range(npre):
                rest[npre + 1 + t][...] = (xh * rest[t][...]).astype(BF16)

    row = pl.BlockSpec((tr, D), lambda i: (i, 0))
    gain = pl.BlockSpec((1, D), lambda i: (0, 0))
    return pl.pallas_call(
        body, name=name, grid=(L // tr,),
        in_specs=[row, row, gain] + [gain] * npre,
        out_specs=[row] * (1 + npre),
        out_shape=[jax.ShapeDtypeStruct((L, D), F32)] + [jax.ShapeDtypeStruct((L, D), BF16)] * npre,
        compiler_params=_params("parallel"),
    )(h, branch, g_post, *g_pre)


def _norm_bwd(name, dres, stream=None, pre=(), post=None):
    L, D = dres.shape
    tr = _row_tile(L, 320)
    npre = len(pre)
    has_post = post is not None
    n_in = 1 + (1 + 2 * npre if npre else 0) + (2 if has_post else 0)
    n_dg = npre + (1 if has_post else 0)

    def body(*refs):
        ins, outs = refs[:n_in], refs[n_in:]
        i = pl.program_id(0)
        dsum = ins[0][...]
        dgs = []
        pos = 1
        if npre:
            xv = ins[pos][...]
            pos += 1
            rs = _rstd(xv)
            xh = xv * rs
            dxh = None
            for t in range(npre):
                gv, dy = ins[pos][...], ins[pos + 1][...]
                pos += 2
                dgs.append(jnp.sum(dy * xh, axis=0, keepdims=True))
                term = dy * gv
                dxh = term if dxh is None else dxh + term
            dsum = dsum + rs * (dxh - xh * jnp.mean(dxh * xh, axis=-1, keepdims=True))
        outs[0][...] = dsum
        o = 1
        if has_post:
            bv, gp = ins[pos][...], ins[pos + 1][...]
            rs = _rstd(bv)
            bh = bv * rs
            dgs.append(jnp.sum(dsum * bh, axis=0, keepdims=True))
            dbh = dsum * gp
            outs[1][...] = (rs * (dbh - bh * jnp.mean(dbh * bh, axis=-1, keepdims=True))).astype(BF16)
            o = 2

        @pl.when(i == 0)
        def _():
            for t in range(n_dg):
                outs[o + t][...] = jnp.zeros((8, D), F32)

        for t in range(n_dg):
            outs[o + t][0:1, :] += dgs[t]

    row = pl.BlockSpec((tr, D), lambda i: (i, 0))
    gain = pl.BlockSpec((1, D), lambda i: (0, 0))
    acc = pl.BlockSpec((8, D), lambda i: (0, 0))
    args, in_specs = [dres], [row]
    if npre:
        args.append(stream)
        in_specs.append(row)
        for gv, dy in pre:
            args += [gv, dy]
            in_specs += [gain, row]
    if has_post:
        args += [post[0], post[1]]
        in_specs += [row, gain]
    out_specs = [row] + ([row] if has_post else []) + [acc] * n_dg
    out_shape = ([jax.ShapeDtypeStruct((L, D), F32)] + ([jax.ShapeDtypeStruct((L, D), BF16)] if has_post else [])
                 + [jax.ShapeDtypeStruct((8, D), F32)] * n_dg)
    return pl.pallas_call(
        body, name=name, grid=(L // tr,), in_specs=in_specs, out_specs=out_specs, out_shape=out_shape,
        compiler_params=_params("arbitrary"),
    )(*args)


def _conv_fwd(name, p, taps):
    L, D3 = p.shape
    D = D3 // 3
    nblk = L // BLK

    def body(p_ref, prev_ref, w_ref, y_ref, u_scr):
        i = pl.program_id(0)
        for cg in range(D // BLK):
            lo = cg * BLK
            b = p_ref[:, lo:lo + BLK]
            u = p_ref[:, D + lo:D + lo + BLK] * p_ref[:, 2 * D + lo:2 * D + lo + BLK]
            up = prev_ref[:, D + lo:D + lo + BLK] * prev_ref[:, 2 * D + lo:2 * D + lo + BLK]
            u_scr[0:8, :] = jnp.where(i > 0, up, 0.0)
            u_scr[8:8 + BLK, :] = u
            conv = (w_ref[0:1, lo:lo + BLK] * u_scr[6:6 + BLK, :] + w_ref[1:2, lo:lo + BLK] * u_scr[7:7 + BLK, :]
                    + w_ref[2:3, lo:lo + BLK] * u)
            y_ref[:, lo:lo + BLK] = (b * conv).astype(BF16)

    return pl.pallas_call(
        body, name=name, grid=(nblk,),
        in_specs=[pl.BlockSpec((BLK, D3), lambda i: (i, 0)),
                  pl.BlockSpec((8, D3), lambda i: (jnp.maximum(i * (BLK // 8) - 1, 0), 0)),
                  pl.BlockSpec((8, D), lambda i: (0, 0))],
        out_specs=pl.BlockSpec((BLK, D), lambda i: (i, 0)),
        out_shape=jax.ShapeDtypeStruct((L, D), BF16),
        scratch_shapes=[pltpu.VMEM((BLK + 8, BLK), F32)],
        compiler_params=_params("parallel"),
    )(p, p, taps)


def _conv_bwd(name, p, dy, taps):
    L, D3 = p.shape
    D = D3 // 3
    nblk = L // BLK
    last8 = L // 8 - 1

    def body(p_ref, prev_ref, next_ref, dy_ref, dyn_ref, w_ref, dp_ref, dw_ref, u_scr, d_scr):
        i = pl.program_id(0)

        @pl.when(i == 0)
        def _():
            dw_ref[...] = jnp.zeros((8, D), F32)

        for cg in range(D // BLK):
            lo = cg * BLK
            b = p_ref[:, lo:lo + BLK]
            c = p_ref[:, D + lo:D + lo + BLK]
            xv = p_ref[:, 2 * D + lo:2 * D + lo + BLK]
            u = c * xv
            up = prev_ref[:, D + lo:D + lo + BLK] * prev_ref[:, 2 * D + lo:2 * D + lo + BLK]
            u_scr[0:8, :] = jnp.where(i > 0, up, 0.0)
            u_scr[8:8 + BLK, :] = u
            u2 = u_scr[6:6 + BLK, :]
            u1 = u_scr[7:7 + BLK, :]
            w0, w1, w2 = w_ref[0:1, lo:lo + BLK], w_ref[1:2, lo:lo + BLK], w_ref[2:3, lo:lo + BLK]
            conv = w0 * u2 + w1 * u1 + w2 * u
            dyv = dy_ref[:, lo:lo + BLK]
            dconv = dyv * b
            dnext = dyn_ref[:, lo:lo + BLK] * next_ref[:, lo:lo + BLK]
            d_scr[0:BLK, :] = dconv
            d_scr[BLK:BLK + 8, :] = jnp.where(i < nblk - 1, dnext, 0.0)
            du = w2 * dconv + w1 * d_scr[1:1 + BLK, :] + w0 * d_scr[2:2 + BLK, :]
            dp_ref[:, lo:lo + BLK] = (dyv * conv).astype(BF16)
            dp_ref[:, D + lo:D + lo + BLK] = (du * xv).astype(BF16)
            dp_ref[:, 2 * D + lo:2 * D + lo + BLK] = (du * c).astype(BF16)
            dw_ref[0:1, lo:lo + BLK] += jnp.sum(dconv * u2, axis=0, keepdims=True)
            dw_ref[1:2, lo:lo + BLK] += jnp.sum(dconv * u1, axis=0, keepdims=True)
            dw_ref[2:3, lo:lo + BLK] += jnp.sum(dconv * u, axis=0, keepdims=True)

    halo_prev = lambda i: (jnp.maximum(i * (BLK // 8) - 1, 0), 0)
    halo_next = lambda i: (jnp.minimum((i + 1) * (BLK // 8), last8), 0)
    return pl.pallas_call(
        body, name=name, grid=(nblk,),
        in_specs=[pl.BlockSpec((BLK, D3), lambda i: (i, 0)), pl.BlockSpec((8, D3), halo_prev),
                  pl.BlockSpec((8, D3), halo_next), pl.BlockSpec((BLK, D), lambda i: (i, 0)),
                  pl.BlockSpec((8, D), halo_next), pl.BlockSpec((8, D), lambda i: (0, 0))],
        out_specs=[pl.BlockSpec((BLK, D3), lambda i: (i, 0)), pl.BlockSpec((8, D), lambda i: (0, 0))],
        out_shape=[jax.ShapeDtypeStruct((L, D3), BF16), jax.ShapeDtypeStruct((8, D), F32)],
        scratch_shapes=[pltpu.VMEM((BLK + 8, BLK), F32), pltpu.VMEM((BLK + 8, BLK), F32)],
        compiler_params=_params("arbitrary"),
    )(p, p, p, dy, dy, taps)


KB_STEP = 3
FAR = 1 << 30


def _stack_heads(x):
    lane = lax.broadcasted_iota(jnp.int32, x.shape, 1)
    zero = jnp.zeros_like(x)
    return jnp.concatenate([jnp.where(lane < HEAD_DIM, x, zero), jnp.where(lane >= HEAD_DIM, x, zero)], axis=0)


def _unstack_heads(x2):
    lane = lax.broadcasted_iota(jnp.int32, (BLK, BLK), 1)
    return jnp.where(lane < HEAD_DIM, x2[0:BLK], x2[BLK:2 * BLK])


def _key_block(kb, u):
    kbu = kb - u
    off = pl.multiple_of(jnp.maximum(kbu, 0) * BLK, BLK)
    return off, jnp.where(kbu >= 0, kbu * BLK, FAR)


def _attn_iotas(i):
    lane = lax.broadcasted_iota(jnp.int32, (2 * BLK, BLK), 1)
    row = lax.broadcasted_iota(jnp.int32, (2 * BLK, BLK), 0)
    krow = lax.broadcasted_iota(jnp.int32, (BLK, BLK), 0)
    klane = lax.broadcasted_iota(jnp.int32, (BLK, BLK), 1)
    return lane, i * BLK + (row & (BLK - 1)), krow, klane


def _score_block(q2, kblk, kbase, qpos, lane, rr, tri):
    z = lax.dot_general(q2, kblk, NT_DIMS, preferred_element_type=F32)
    kpos = kbase + lane
    vis = (kpos < qpos) & (kpos >= PAD)
    sp = jnp.log(1.0 + jnp.exp(-jnp.abs(z)))
    lb = jnp.minimum(z, 0.0) - sp
    lom = jnp.where(vis, lb - z, 0.0)
    hi = lom.astype(BF16)
    lo = (lom - hi.astype(F32)).astype(BF16)
    cs = jnp.dot(hi, tri, preferred_element_type=F32) + jnp.dot(lo, tri, preferred_element_type=F32)
    a = jnp.where(vis, jnp.exp(lb + rr + cs), 0.0)
    return vis, lb, lom, a


def _attn_fwd(name, q, k, v):
    L, D = q.shape

    def body(q_ref, k_ref, v_ref, of_ref, ob_ref, aw_ref, bw_ref, fl_ref, acc_ref, r_ref):
        i = pl.program_id(1)
        lane, qpos, krow, klane = _attn_iotas(i)
        tri = (krow > klane).astype(BF16)
        q2 = _stack_heads(q_ref[...])
        acc_ref[...] = jnp.zeros((2 * BLK, BLK), F32)
        r_ref[...] = jnp.zeros((2 * BLK, 1), F32)

        def blocks(kb, save):
            rr = r_ref[...]
            upd = None
            for u in range(KB_STEP):
                off, kbase = _key_block(kb, u)
                vis, lb, lom, a = _score_block(q2, k_ref[pl.ds(off, BLK), :], kbase, qpos, lane, rr, tri)
                ab = a.astype(BF16)
                if save:
                    aw_ref[:, u * BLK:(u + 1) * BLK] = ab
                    bw_ref[:, u * BLK:(u + 1) * BLK] = jnp.where(vis, jnp.exp(lb), 0.0).astype(BF16)
                part = jnp.dot(ab, v_ref[pl.ds(off, BLK), :], preferred_element_type=F32)
                upd = part if upd is None else upd + part
                rr = rr + jnp.sum(lom, axis=1, keepdims=True)
            acc_ref[...] += upd
            r_ref[...] = rr
            return (jnp.max(rr) < LOG_ZERO).astype(jnp.int32)

        done = blocks(i, True)
        kb_end, _ = lax.while_loop(lambda c: (c[0] >= 0) & (c[1] == 0),
                                   lambda c: (c[0] - KB_STEP, blocks(c[0], False)), (i - KB_STEP, done))
        fl_ref[...] = jnp.zeros((8, BLK), F32) + (kb_end == i - KB_STEP).astype(F32)
        o = _unstack_heads(acc_ref[...])
        of_ref[...] = o
        ob_ref[...] = o.astype(BF16)

    qspec = pl.BlockSpec((BLK, BLK), lambda h, i: (i, h))
    kvspec = pl.BlockSpec((L, BLK), lambda h, i: (0, h))
    wspec = pl.BlockSpec((None, None, 2 * BLK, KB_STEP * BLK), lambda h, i: (h, i, 0, 0))
    fspec = pl.BlockSpec((None, None, 8, BLK), lambda h, i: (h, i, 0, 0))
    nh, nq = D // BLK, L // BLK
    return pl.pallas_call(
        body, name=name, grid=(nh, nq),
        in_specs=[qspec, kvspec, kvspec], out_specs=[qspec, qspec, wspec, wspec, fspec],
        out_shape=[jax.ShapeDtypeStruct((L, D), F32), jax.ShapeDtypeStruct((L, D), BF16),
                   jax.ShapeDtypeStruct((nh, nq, 2 * BLK, KB_STEP * BLK), BF16),
                   jax.ShapeDtypeStruct((nh, nq, 2 * BLK, KB_STEP * BLK), BF16),
                   jax.ShapeDtypeStruct((nh, nq, 8, BLK), F32)],
        scratch_shapes=[pltpu.VMEM((2 * BLK, BLK), F32), pltpu.VMEM((2 * BLK, 1), F32)],
        compiler_params=_params("parallel", "arbitrary"),
    )(q, k, v)


def _attn_bwd(name, q, k, v, o, do, a_win, b_win, flag):
    L, D = q.shape

    def body(q_ref, k_ref, v_ref, o_ref, do_ref, aw_ref, bw_ref, fl_ref, dq_ref, dk_ref, dv_ref,
             acc_ref, r_ref, c_ref):
        i = pl.program_id(1)

        @pl.when(i == 0)
        def _():
            dk_ref[...] = jnp.zeros((L, BLK), F32)
            dv_ref[...] = jnp.zeros((L, BLK), F32)

        krow = lax.broadcasted_iota(jnp.int32, (BLK, BLK), 0)
        klane = lax.broadcasted_iota(jnp.int32, (BLK, BLK), 1)
        tri_incl = (krow >= klane).astype(BF16)
        q2 = _stack_heads(q_ref[...])
        do2 = _stack_heads(do_ref[...].astype(BF16))
        total = jnp.sum(do2.astype(F32) * jnp.concatenate([o_ref[...]] * 2, axis=0), axis=1, keepdims=True)

        def grads(off, kblk, ab, beta, cc):
            da = lax.dot_general(do2, v_ref[pl.ds(off, BLK), :], NT_DIMS, preferred_element_type=F32)
            g = da * ab.astype(F32)
            ghi = g.astype(BF16)
            glo = (g - ghi.astype(F32)).astype(BF16)
            sfx = (jnp.dot(ghi, tri_incl, preferred_element_type=F32)
                   + jnp.dot(glo, tri_incl, preferred_element_type=F32))
            before = total - cc - sfx
            dz = (g * (1.0 - beta) - beta * before).astype(BF16)
            dk_ref[pl.ds(off, BLK), :] += lax.dot_general(dz, q2, TN_DIMS, preferred_element_type=F32)
            dv_ref[pl.ds(off, BLK), :] += lax.dot_general(ab, do2, TN_DIMS, preferred_element_type=F32)
            return jnp.dot(dz, kblk, preferred_element_type=F32), cc + jnp.sum(g, axis=1, keepdims=True)

        fast = jnp.max(fl_ref[...]) > 0.5

        @pl.when(fast)
        def _():
            cc = jnp.zeros((2 * BLK, 1), F32)
            upd = None
            for u in range(KB_STEP):
                off, _ = _key_block(i, u)
                cols = slice(u * BLK, (u + 1) * BLK)
                part, cc = grads(off, k_ref[pl.ds(off, BLK), :], aw_ref[:, cols], bw_ref[:, cols].astype(F32), cc)
                upd = part if upd is None else upd + part
            dq_ref[...] = (_unstack_heads(upd) * ATTN_SCALE).astype(BF16)

        @pl.when(jnp.logical_not(fast))
        def _():
            lane, qpos, _, _ = _attn_iotas(i)
            tri = (krow > klane).astype(BF16)
            acc_ref[...] = jnp.zeros((2 * BLK, BLK), F32)
            r_ref[...] = jnp.zeros((2 * BLK, 1), F32)
            c_ref[...] = jnp.zeros((2 * BLK, 1), F32)

            def step(carry):
                kb, _ = carry
                rr = r_ref[...]
                cc = c_ref[...]
                upd = None
                for u in range(KB_STEP):
                    off, kbase = _key_block(kb, u)
                    kblk = k_ref[pl.ds(off, BLK), :]
                    vis, lb, lom, a = _score_block(q2, kblk, kbase, qpos, lane, rr, tri)
                    part, cc = grads(off, kblk, a.astype(BF16), jnp.where(vis, jnp.exp(lb), 0.0), cc)
                    upd = part if upd is None else upd + part
                    rr = rr + jnp.sum(lom, axis=1, keepdims=True)
                acc_ref[...] += upd
                c_ref[...] = cc
                r_ref[...] = rr
                return kb - KB_STEP, (jnp.max(rr) < LOG_ZERO).astype(jnp.int32)

            lax.while_loop(lambda c: (c[0] >= 0) & (c[1] == 0), step, (i, jnp.int32(0)))
            dq_ref[...] = (_unstack_heads(acc_ref[...]) * ATTN_SCALE).astype(BF16)

    qspec = pl.BlockSpec((BLK, BLK), lambda h, i: (i, h))
    kvspec = pl.BlockSpec((L, BLK), lambda h, i: (0, h))
    wspec = pl.BlockSpec((None, None, 2 * BLK, KB_STEP * BLK), lambda h, i: (h, i, 0, 0))
    fspec = pl.BlockSpec((None, None, 8, BLK), lambda h, i: (h, i, 0, 0))
    return pl.pallas_call(
        body, name=name, grid=(D // BLK, L // BLK),
        in_specs=[qspec, kvspec, kvspec, qspec, qspec, wspec, wspec, fspec], out_specs=[qspec, kvspec, kvspec],
        out_shape=[jax.ShapeDtypeStruct((L, D), BF16), jax.ShapeDtypeStruct((L, D), F32),
                   jax.ShapeDtypeStruct((L, D), F32)],
        scratch_shapes=[pltpu.VMEM((2 * BLK, BLK), F32), pltpu.VMEM((2 * BLK, 1), F32),
                        pltpu.VMEM((2 * BLK, 1), F32)],
        compiler_params=_params("parallel", "arbitrary"),
    )(q, k, v, o, do, a_win, b_win, flag)


def _loss_and_grad(name, h, target):
    L, D = h.shape
    first = (PAD + N_META) // BLK

    def body(h_ref, t_ref, sq_ref, dh_ref):
        i = pl.program_id(0)

        @pl.when(i == 0)
        def _():
            sq_ref[...] = jnp.zeros((8, BLK), F32)

        @pl.when(i < first)
        def _():
            dh_ref[...] = jnp.zeros((BLK, D), F32)

        @pl.when(i >= first)
        def _():
            err = h_ref[...] - t_ref[...]
            sq_ref[...] += jnp.sum(err * err)
            dh_ref[...] = err * (1.0 / D)

    return pl.pallas_call(
        body, name=name, grid=(L // BLK,),
        in_specs=[pl.BlockSpec((BLK, D), lambda i: (i, 0)),
                  pl.BlockSpec((BLK, D), lambda i: (jnp.maximum(i - first, 0), 0))],
        out_specs=[pl.BlockSpec((8, BLK), lambda i: (0, 0)), pl.BlockSpec((BLK, D), lambda i: (i, 0))],
        out_shape=[jax.ShapeDtypeStruct((8, BLK), F32), jax.ShapeDtypeStruct((L, D), F32)],
        compiler_params=_params("arbitrary"),
    )(h, target)


def _add_cast(name, a, b):
    L, D = a.shape
    tr = _row_tile(L, 640)

    def body(a_ref, b_ref, o_ref):
        o_ref[...] = (a_ref[...] + b_ref[...]).astype(BF16)

    row = pl.BlockSpec((tr, D), lambda i: (i, 0))
    return pl.pallas_call(body, name=name, grid=(L // tr,), in_specs=[row, row], out_specs=row,
                          out_shape=jax.ShapeDtypeStruct((L, D), BF16), compiler_params=_params("parallel"))(a, b)


def _adamw(w, g, m, v):
    m = ADAM_B1 * m + (1.0 - ADAM_B1) * g
    v = ADAM_B2 * v + (1.0 - ADAM_B2) * (g * g)
    m_hat = m / (1.0 - ADAM_B1 ** ADAM_STEP)
    v_hat = v / (1.0 - ADAM_B2 ** ADAM_STEP)
    delta = -ADAM_LR * (m_hat / (jnp.sqrt(v_hat) + ADAM_EPS) + ADAM_WD * w)
    return delta, m, v


def _adam_small(name, w, g, m, v):
    def body(w_ref, g_ref, m_ref, v_ref, d_ref, mo_ref, vo_ref):
        d, mn, vn = _adamw(w_ref[...], g_ref[...], m_ref[...], v_ref[...])
        d_ref[...] = d
        mo_ref[...] = mn
        vo_ref[...] = vn

    return pl.pallas_call(body, name=name, out_shape=[jax.ShapeDtypeStruct(w.shape, F32)] * 3)(w, g, m, v)


def _sum_slots(name, slots):
    def body(s_ref, o_ref):
        acc = s_ref[0]
        for d in range(1, N_DEV):
            acc = acc + s_ref[d]
        o_ref[...] = acc

    return pl.pallas_call(body, name=name, out_shape=jax.ShapeDtypeStruct(slots.shape[1:], F32))(slots)


def _pair_sum(name, g, recv, c_idx):
    _, R, C = g.shape
    tr = _row_tile(R, 512)

    def body(sel_ref, g_ref, r_ref, o_ref):
        o_ref[...] = (g_ref[...] + r_ref[...]).astype(BF16)

    return pl.pallas_call(
        body, name=name,
        grid_spec=pltpu.PrefetchScalarGridSpec(
            num_scalar_prefetch=1, grid=(4, R // tr),
            in_specs=[pl.BlockSpec((None, tr, C), lambda k, i, sel: (2 * k + sel[0], i, 0)),
                      pl.BlockSpec((None, tr, C), lambda k, i, sel: (k, i, 0))],
            out_specs=pl.BlockSpec((None, tr, C), lambda k, i, sel: (k, i, 0))),
        out_shape=jax.ShapeDtypeStruct((4, R, C), BF16),
        compiler_params=_params("parallel", "parallel"),
    )(c_idx, g, recv)


def _final_sum_adam(name, part, recv, chip_idx, w, m, v):
    R, C = w.shape
    tr = _row_tile(R, 512)

    def body(sel_ref, p_ref, r_ref, w_ref, m_ref, v_ref, g_ref, d_ref, mo_ref, vo_ref):
        g = ((p_ref[...].astype(F32) + r_ref[0].astype(F32)) + r_ref[1].astype(F32)) + r_ref[2].astype(F32)
        d, mn, vn = _adamw(w_ref[...], g, m_ref[...], v_ref[...])
        g_ref[...] = g
        d_ref[...] = d
        mo_ref[...] = mn
        vo_ref[...] = vn

    row = pl.BlockSpec((tr, C), lambda i, sel: (i, 0))
    return pl.pallas_call(
        body, name=name,
        grid_spec=pltpu.PrefetchScalarGridSpec(
            num_scalar_prefetch=1, grid=(R // tr,),
            in_specs=[pl.BlockSpec((None, tr, C), lambda i, sel: (sel[0], i, 0)),
                      pl.BlockSpec((3, tr, C), lambda i, sel: (0, i, 0)), row, row, row],
            out_specs=[row] * 4),
        out_shape=[jax.ShapeDtypeStruct((R, C), F32)] * 4,
        compiler_params=_params("parallel"),
    )(chip_idx, part, recv, w, m, v)


def _position():
    return lax.axis_index("x"), lax.axis_index("y"), lax.axis_index("c")


def _all_gather(name, shards):
    n = len(shards)

    def body(*refs):
        x_refs, o_refs = refs[:n], refs[n:2 * n]
        send_sems, recv_sems, local_sems = refs[2 * n:]
        x, y, c = _position()
        me, sibling = (x, y, c), (x, y, 1 - c)
        chips = [(1 - x, y), (x, 1 - y), (1 - x, 1 - y)]

        def copy(a, sem, block, to, src=None):
            dst = o_refs[a].at[4 * block[0] + 2 * block[1] + block[2]]
            return pltpu.make_async_remote_copy(
                src_ref=dst if src is None else src, dst_ref=dst,
                send_sem=send_sems.at[a, sem], recv_sem=recv_sems.at[a, sem], device_id=to, device_id_type=MESH)

        mine = [pltpu.make_async_copy(x_refs[a], o_refs[a].at[4 * x + 2 * y + c], local_sems.at[a]) for a in range(n)]
        for cp in mine:
            cp.start()
        first = []
        for a in range(n):
            first.append(copy(a, 0, me, sibling, src=x_refs[a]))
            first += [copy(a, 1 + j, me, (*chip, c), src=x_refs[a]) for j, chip in enumerate(chips)]
        for cp in first:
            cp.start()
        passed = []
        for j, chip in enumerate(chips):
            for a in range(n):
                copy(a, 1 + j, (*chip, c), me).wait_recv()
                fwd = copy(a, 4 + j, (*chip, c), sibling)
                fwd.start()
                passed.append(fwd)
        for a in range(n):
            copy(a, 0, sibling, me).wait_recv()
        for j, chip in enumerate(chips):
            for a in range(n):
                copy(a, 4 + j, (*chip, 1 - c), me).wait_recv()
        for cp in first + passed:
            cp.wait_send()
        for cp in mine:
            cp.wait()

    return pl.pallas_call(
        body, name=name, in_specs=[ANY] * n, out_specs=[ANY] * n,
        out_shape=[jax.ShapeDtypeStruct((N_DEV,) + s.shape, s.dtype) for s in shards],
        scratch_shapes=[pltpu.SemaphoreType.DMA((n, 7)), pltpu.SemaphoreType.DMA((n, 7)),
                        pltpu.SemaphoreType.DMA((n,))],
        compiler_params=pltpu.CompilerParams(has_side_effects=True),
    )(*shards)


def _exchange_sibling(name, grads):
    n = len(grads)

    def body(*refs):
        g_refs, r_refs = refs[:n], refs[n:2 * n]
        send_sems, recv_sems = refs[2 * n:]
        x, y, c = _position()
        copies = []
        for a in range(n):
            for k in range(4):
                copies.append(pltpu.make_async_remote_copy(
                    src_ref=g_refs[a].at[2 * k + (1 - c)], dst_ref=r_refs[a].at[k],
                    send_sem=send_sems.at[a, k], recv_sem=recv_sems.at[a, k],
                    device_id=(x, y, 1 - c), device_id_type=MESH))
        for cp in copies:
            cp.start()
        for cp in copies:
            cp.wait()

    return pl.pallas_call(
        body, name=name, in_specs=[ANY] * n, out_specs=[ANY] * n,
        out_shape=[jax.ShapeDtypeStruct((4,) + g.shape[1:], g.dtype) for g in grads],
        scratch_shapes=[pltpu.SemaphoreType.DMA((n, 4)), pltpu.SemaphoreType.DMA((n, 4))],
        compiler_params=pltpu.CompilerParams(has_side_effects=True),
    )(*grads)


def _exchange_chips(name, parts):
    n = len(parts)

    def body(*refs):
        p_refs, r_refs = refs[:n], refs[n:2 * n]
        send_sems, recv_sems = refs[2 * n:]
        x, y, c = _position()
        chips = [(1 - x, y), (x, 1 - y), (1 - x, 1 - y)]
        copies = []
        for a in range(n):
            for j, chip in enumerate(chips):
                copies.append(pltpu.make_async_remote_copy(
                    src_ref=p_refs[a].at[2 * chip[0] + chip[1]], dst_ref=r_refs[a].at[j],
                    send_sem=send_sems.at[a, j], recv_sem=recv_sems.at[a, j],
                    device_id=(*chip, c), device_id_type=MESH))
        for cp in copies:
            cp.start()
        for cp in copies:
            cp.wait()

    return pl.pallas_call(
        body, name=name, in_specs=[ANY] * n, out_specs=[ANY] * n,
        out_shape=[jax.ShapeDtypeStruct((3,) + p.shape[1:], p.dtype) for p in parts],
        scratch_shapes=[pltpu.SemaphoreType.DMA((n, 3)), pltpu.SemaphoreType.DMA((n, 3))],
        compiler_params=pltpu.CompilerParams(has_side_effects=True),
    )(*parts)


def kernel(x, meta_tokens, norm_gains, conv_in_proj, conv_w, conv_out_proj, kv_norm, w_k, w_v, w_q, w_o, mlp_w1, mlp_w2, loss_target, m_meta_tokens, m_norm_gains, m_conv_in_proj, m_conv_w, m_conv_out_proj, m_kv_norm, m_w_k, m_w_v, m_w_q, m_w_o, m_mlp_w1, m_mlp_w2, v_meta_tokens, v_norm_gains, v_conv_in_proj, v_conv_w, v_conv_out_proj, v_kv_norm, v_w_k, v_w_v, v_w_q, v_w_o, v_mlp_w1, v_mlp_w2):
    xi, target = x[0], loss_target[0]
    S, D = xi.shape
    L = PAD + N_META + S
    dsh = D // N_DEV
    px, py, pc = _position()
    dev = 4 * px + 2 * py + pc
    tm_big = _row_tile(L, 1664)
    tm_mid = _row_tile(L, 640)

    def pack_small(meta, gains, taps):
        return jnp.concatenate([meta, gains.reshape(DEPTH * 4, dsh), taps.reshape(N_A * 3, dsh),
                                jnp.zeros((2, dsh), F32)], axis=0)

    big_w = [conv_in_proj, conv_out_proj, w_k[None], w_v[None], w_q, w_o, mlp_w1, mlp_w2]
    big_m = [m_conv_in_proj, m_conv_out_proj, m_w_k[None], m_w_v[None], m_w_q, m_w_o, m_mlp_w1, m_mlp_w2]
    big_v = [v_conv_in_proj, v_conv_out_proj, v_w_k[None], v_w_v[None], v_w_q, v_w_o, v_mlp_w1, v_mlp_w2]
    small_w = pack_small(meta_tokens, norm_gains, conv_w)
    gathered = _all_gather("gather_weights", [small_w] + [w.astype(BF16) for w in big_w])
    small_full = gathered[0].transpose(1, 0, 2).reshape(40, D)
    win_g, wout_g, wk_g, wv_g, wq_g, wo_g, w1_g, w2_g = gathered[1:]
    meta_full = small_full[0:N_META]
    gain = lambda layer, n: small_full[N_META + 4 * layer + n][None]
    taps = [jnp.concatenate([small_full[32 + 3 * l:35 + 3 * l], jnp.zeros((5, D), F32)], axis=0) for l in range(N_A)]
    kvn = kv_norm[None]

    h = jnp.concatenate([jnp.zeros((PAD, D), F32), meta_full, xi], axis=0)
    n1 = _norm_fwd("norm_in", h, gain(0, 0))
    saved = []
    hk = k = v = None
    for layer in range(DEPTH):
        s = {"h0": h, "n1": n1}
        if layer < N_A:
            s["p"] = _mm_nn(f"conv_in{layer}", n1, win_g, layer, "col", [F32], tm_big)[0]
            s["y"] = _conv_fwd(f"conv{layer}", s["p"], taps[layer])
            mix = _mm_nn(f"conv_out{layer}", s["y"], wout_g, layer, "row", [F32], tm_big, tn=D)[0]
        else:
            j = layer - N_A
            if j == 0:
                k = _mm_nn("k_proj", hk, wk_g, 0, "row", [BF16], tm_big, tn=D)[0]
                v = _mm_nn("v_proj", hk, wv_g, 0, "row", [BF16], tm_big, tn=D)[0]
            s["q"] = _mm_nn(f"q_proj{j}", n1, wq_g, j, "row", [BF16], tm_big, tn=D,
                            epi=lambda acc: (acc * ATTN_SCALE,))[0]
            s["o"], s["ob"], s["aw"], s["bw"], s["fl"] = _attn_fwd(f"attn{j}", s["q"], k, v)
            mix = _mm_nn(f"o_proj{j}", s["ob"], wo_g, j, "row", [F32], tm_big, tn=D)[0]
        s["mix"] = mix
        s["h1"], s["n3"] = _res_norm(f"mix_norm{layer}", h, mix, gain(layer, 1), [gain(layer, 2)])
        s["act"] = _mm_nn(f"mlp_up{layer}", s["n3"], w1_g, layer, "col", [BF16], tm_big,
                          epi=lambda acc: (jnp.square(jnp.maximum(acc, 0.0)),))[0]
        s["ff"] = _mm_nn(f"mlp_down{layer}", s["act"], w2_g, layer, "row", [F32], tm_mid, tn=D)[0]
        pre = [] if layer == DEPTH - 1 else [gain(layer + 1, 0)] + ([kvn] if layer == N_A - 1 else [])
        outs = _res_norm(f"mlp_norm{layer}", s["h1"], s["ff"], gain(layer, 3), pre)
        h = outs[0]
        if pre:
            n1 = outs[1]
        if layer == N_A - 1:
            hk = outs[2]
        saved.append(s)

    sq, dh = _loss_and_grad("loss", h, target)
    loss = lax.psum(0.5 * sq[0, 0] / D, ("x", "y", "c"))

    g_gain = [[None] * 4 for _ in range(DEPTH)]
    g_taps = [None] * N_A
    g_w = {n: None for n in ("win", "wout", "wq", "wo", "w1", "w2")}
    dk_parts, dv_parts = [], []
    g_kvn = None
    s = saved[DEPTH - 1]
    dh, dff, g_gain[DEPTH - 1][3] = _norm_bwd("bwd_top", dh, post=(s["ff"], gain(DEPTH - 1, 3)))
    for layer in reversed(range(DEPTH)):
        s = saved[layer]
        def relu_grad(acc, act):
            act = act.astype(F32)
            return (acc * (2.0 * act * lax.rsqrt(jnp.maximum(act, 1e-30))),)

        da1 = _mm_nt(f"mlp_down_bwd{layer}", dff, w2_g, layer, "row", [BF16], tm_big, epi=relu_grad,
                     extras=(s["act"],))[0]
        g_w["w2"] = _mm_tn(f"mlp_w2_grad{layer}", s["act"], dff, "row", tm_big, group=2, layer=layer,
                           layers=DEPTH, into=g_w["w2"])
        g_w["w1"] = _mm_tn(f"mlp_w1_grad{layer}", s["n3"], da1, "col", tm_big, group=4, layer=layer,
                           layers=DEPTH, into=g_w["w1"])
        dn3 = _mm_nt(f"mlp_up_bwd{layer}", da1, w1_g, layer, "col", [F32], tm_mid, tn=D)[0]
        dh, dmix, g_gain[layer][2], g_gain[layer][1] = _norm_bwd(
            f"bwd_mid{layer}", dh, stream=s["h1"], pre=[(gain(layer, 2), dn3)], post=(s["mix"], gain(layer, 1)))
        pre = []
        if layer < N_A:
            dy = _mm_nt(f"conv_out_bwd{layer}", dmix, wout_g, layer, "row", [F32], tm_big, group=4)[0]
            g_w["wout"] = _mm_tn(f"conv_out_grad{layer}", s["y"], dmix, "row", tm_big, group=4, layer=layer,
                                 layers=N_A, into=g_w["wout"])
            dp, g_taps[layer] = _conv_bwd(f"conv_bwd{layer}", s["p"], dy, taps[layer])
            g_w["win"] = _mm_tn(f"conv_in_grad{layer}", s["n1"], dp, "col", tm_big, group=4, layer=layer, layers=N_A,
                                into=g_w["win"])
            dn1 = _mm_nt(f"conv_in_bwd{layer}", dp, win_g, layer, "col", [F32], tm_mid, tn=D)[0]
        else:
            j = layer - N_A
            do = _mm_nt(f"o_proj_bwd{j}", dmix, wo_g, j, "row", [F32], tm_big, group=4)[0]
            g_w["wo"] = _mm_tn(f"o_proj_grad{j}", s["ob"], dmix, "row", tm_big, group=4, layer=j,
                               layers=DEPTH - N_A, into=g_w["wo"])
            dq, dk_j, dv_j = _attn_bwd(f"attn_bwd{j}", s["q"], k, v, s["o"], do, s["aw"], s["bw"], s["fl"])
            dk_parts.append(dk_j)
            dv_parts.append(dv_j)
            g_w["wq"] = _mm_tn(f"q_proj_grad{j}", s["n1"], dq, "row", tm_big, group=4, layer=j,
                               layers=DEPTH - N_A, into=g_w["wq"])
            dn1 = _mm_nt(f"q_proj_bwd{j}", dq, wq_g, j, "row", [F32], tm_big, group=4)[0]
            if j == 0:
                dkb = _add_cast("dk_sum", dk_parts[0], dk_parts[1])
                dvb = _add_cast("dv_sum", dv_parts[0], dv_parts[1])
                g_wk = _mm_tn("k_proj_grad", hk, dkb, "row", tm_big, group=4)
                g_wv = _mm_tn("v_proj_grad", hk, dvb, "row", tm_big, group=4)
                dhk_k = _mm_nt("k_proj_bwd", dkb, wk_g, 0, "row", [F32], tm_big, group=4)[0]
                dhk = _mm_nt("v_proj_bwd", dvb, wv_g, 0, "row", [F32], tm_big, group=4,
                             epi=lambda acc, other: (acc + other,), extras=(dhk_k,))[0]
                pre = [(kvn, dhk)]
        pre = [(gain(layer, 0), dn1)] + pre
        if layer > 0:
            sp = saved[layer - 1]
            outs = _norm_bwd(f"bwd_in{layer}", dh, stream=s["h0"], pre=pre, post=(sp["ff"], gain(layer - 1, 3)))
            dh, dff = outs[0], outs[1]
            g_gain[layer][0] = outs[2]
            if len(pre) == 2:
                g_kvn = outs[3]
            g_gain[layer - 1][3] = outs[-1]
        else:
            dh, g_gain[0][0] = _norm_bwd("bwd_in0", dh, stream=s["h0"], pre=pre)

    grad_x = dh[PAD + N_META:][None]
    g_meta = dh[PAD:PAD + N_META]

    small_g = jnp.concatenate(
        [g_meta] + [g_gain[l][n][0:1] for l in range(DEPTH) for n in range(4)]
        + [g_taps[l][0:3] for l in range(N_A)] + [g_kvn[0:1], jnp.zeros((1, D), F32)], axis=0)
    small_sum = _sum_slots("small_grad_sum", _all_gather("gather_small_grads", [small_g])[0])
    small_mine = lax.dynamic_slice_in_dim(small_sum, dev * dsh, dsh, axis=1)
    small_m = pack_small(m_meta_tokens, m_norm_gains, m_conv_w)
    small_v = pack_small(v_meta_tokens, v_norm_gains, v_conv_w)
    small_d, small_mn, small_vn = _adam_small("adam_small", small_w, small_mine, small_m, small_v)
    pad8 = lambda a: jnp.concatenate([a[None], jnp.zeros((7, D), F32)], axis=0)
    g_kv = small_sum[38]
    kv_d, kv_mn, kv_vn = _adam_small("adam_kv_norm", pad8(kv_norm), pad8(g_kv), pad8(m_kv_norm), pad8(v_kv_norm))

    def unpack_small(a):
        return (a[0:N_META], a[N_META:N_META + 16].reshape(DEPTH, 4, dsh), a[32:38].reshape(N_A, 3, dsh))

    big_g = [g_w["win"], g_w["wout"], g_wk, g_wv, g_w["wq"], g_w["wo"], g_w["w1"], g_w["w2"]]
    flat = lambda a, lead: a.reshape(lead + (-1, a.shape[-1]))
    big_g = [flat(g, (N_DEV,)) for g in big_g]
    from_sibling = _exchange_sibling("grads_to_sibling", big_g)
    c_idx = jnp.reshape(pc, (1,)).astype(jnp.int32)
    chip_idx = jnp.reshape(2 * px + py, (1,)).astype(jnp.int32)
    names = ["conv_in", "conv_out", "w_k", "w_v", "w_q", "w_o", "mlp_w1", "mlp_w2"]
    parts = [_pair_sum(f"pair_sum_{nm}", g, r, c_idx) for nm, g, r in zip(names, big_g, from_sibling)]
    from_chips = _exchange_chips("grads_to_chips", parts)
    big_out = []
    for nm, part, recv, w, m, v2 in zip(names, parts, from_chips, big_w, big_m, big_v):
        res = _final_sum_adam(f"adam_{nm}", part, recv, chip_idx, flat(w, ()), flat(m, ()), flat(v2, ()))
        shape = w.shape[1:] if nm in ("w_k", "w_v") else w.shape
        big_out.append([r.reshape(shape) for r in res])

    def assemble(kind, small_parts, kv_part):
        meta_p, gains_p, taps_p = small_parts
        b = [o[kind] for o in big_out]
        return [meta_p, gains_p, b[0], taps_p, b[1], kv_part, b[2], b[3], b[4], b[5], b[6], b[7]]

    grads = assemble(0, unpack_small(small_mine), g_kv)
    deltas = assemble(1, unpack_small(small_d), kv_d[0])
    new_m = assemble(2, unpack_small(small_mn), kv_mn[0])
    new_v = assemble(3, unpack_small(small_vn), kv_vn[0])
    return (loss, grad_x, *grads, *deltas, *new_m, *new_v)
```

```python
import functools

import jax
import jax.numpy as jnp
from jax import lax
from jax.experimental import pallas as pl
from jax.experimental.pallas import tpu as pltpu

F32 = jnp.float32
BF16 = jnp.bfloat16
MESH = pl.DeviceIdType.MESH

N_DEV = 8
N_META = 16
BLK = 128
PAD = (-N_META) % BLK
HEAD_DIM = 64
DEPTH = 4
N_A = 2
RMS_EPS = 1e-6
ATTN_SCALE = HEAD_DIM ** -0.5
LOG_ZERO = -105.0
ADAM_LR, ADAM_B1, ADAM_B2, ADAM_EPS, ADAM_WD, ADAM_STEP = 0.001, 0.9, 0.999, 1e-08, 0.01, 10
VMEM_LIMIT = 56 * 2 ** 20

NT_DIMS = (((1,), (1,)), ((), ()))
TN_DIMS = (((0,), (0,)), ((), ()))
ANY = pl.BlockSpec(memory_space=pl.ANY)


def _params(*sem):
    return pltpu.CompilerParams(dimension_semantics=sem, vmem_limit_bytes=VMEM_LIMIT)


def _row_tile(rows, pref):
    best = None
    for t in range(16, min(rows, pref) + 1, 16):
        if rows % t == 0:
            best = t
    assert best is not None, (rows, pref)
    return best


def _row_chunks(rows, parts=4):
    if rows % (16 * parts):
        return [slice(0, rows)]
    step = rows // parts
    return [slice(r, r + step) for r in range(0, rows, step)]


def _mm_nn(name, a, w, layer, layout, out_dtypes, tm, tn=None, epi=None, extras=(), gather=None):
    M, K = a.shape
    if layout == "col":
        nb = w.shape[3]
        N, tn = N_DEV * nb, nb
        w_spec = pl.BlockSpec((None, None, K, nb), lambda i, j: (j, layer, 0, 0))
    else:
        kb, N = w.shape[2], w.shape[3]
        assert N_DEV * kb == K
        w_spec = pl.BlockSpec((N_DEV, None, kb, tn), lambda i, j: (0, layer, 0, j))
    ne, no = len(extras), len(out_dtypes)
    g_in = gather.operands() if gather else []
    ngi, ngo = len(g_in), (gather.n if gather else 0)
    nj = N // tn
    steps = (M // tm) * nj

    def body(a_ref, w_ref, *rest):
        outs = rest[ne + ngi:ne + ngi + no]
        if gather:
            start, forward, finish = gather.bind(rest[ne:ne + ngi], rest[ne + ngi + no:ne + ngi + no + ngo],
                                                 rest[ne + ngi + no + ngo:])
            step = pl.program_id(0) * nj + pl.program_id(1)
            pl.when(step == 0)(start)
        wv = w_ref[...]
        if layout == "row":
            wv = wv.reshape(K, tn)
        for rows in _row_chunks(tm):
            acc = jnp.dot(a_ref[rows, :], wv, preferred_element_type=F32)
            vals = epi(acc, *[r[rows, :] for r in rest[:ne]]) if epi else (acc,)
            for o_ref, val in zip(outs, vals):
                o_ref[rows, :] = val.astype(o_ref.dtype)
        if gather:
            pl.when(step == steps // 2)(forward)
            pl.when(step == steps - 1)(finish)

    tile = pl.BlockSpec((tm, tn), lambda i, j: (i, j))
    return pl.pallas_call(
        body, name=name, grid=(M // tm, nj),
        in_specs=[pl.BlockSpec((tm, K), lambda i, j: (i, 0)), w_spec] + [tile] * ne + [ANY] * ngi,
        out_specs=[tile] * no + [ANY] * ngo,
        out_shape=[jax.ShapeDtypeStruct((M, N), d) for d in out_dtypes] + (gather.out_shapes() if gather else []),
        input_output_aliases=gather.aliases(2 + ne, no) if gather else {},
        scratch_shapes=gather.scratch() if gather else [],
        compiler_params=_params("arbitrary", "arbitrary") if gather else _params("parallel", "parallel"),
    )(a, w, *extras, *g_in)


def _mm_nt(name, a, w, layer, layout, out_dtypes, tm, tn=None, group=1, epi=None, extras=()):
    M, Nc = a.shape
    ne = len(extras)
    if layout == "row":
        kb = w.shape[2]
        No, tno = N_DEV * kb, group * kb
        w_spec = pl.BlockSpec((group, None, kb, Nc), lambda i, j: (j, layer, 0, 0))
        a_spec = pl.BlockSpec((tm, Nc), lambda i, j: (i, 0))
        grid = (M // tm, N_DEV // group)
        scratch = []

        def body(a_ref, w_ref, *rest):
            wv = w_ref[...].reshape(tno, Nc)
            for rows in _row_chunks(tm):
                acc = lax.dot_general(a_ref[rows, :], wv, NT_DIMS, preferred_element_type=F32)
                vals = epi(acc, *[r[rows, :] for r in rest[:ne]]) if epi else (acc,)
                for o_ref, val in zip(rest[ne:], vals):
                    o_ref[rows, :] = val.astype(o_ref.dtype)
    else:
        No, nb = w.shape[2], w.shape[3]
        assert N_DEV * nb == Nc
        tno = tn
        w_spec = pl.BlockSpec((N_DEV, None, tn, nb), lambda i, j: (0, layer, j, 0))
        a_spec = pl.BlockSpec((tm, Nc), lambda i, j: (i, 0))
        grid = (M // tm, No // tn)
        scratch = [pltpu.VMEM((tm, tn), F32)]

        def body(a_ref, w_ref, *rest):
            acc_ref = rest[-1]
            for d in range(N_DEV):
                part = lax.dot_general(a_ref[:, d * nb:(d + 1) * nb], w_ref[d], NT_DIMS,
                                       preferred_element_type=F32)
                if d == 0:
                    acc_ref[...] = part
                else:
                    acc_ref[...] += part
            acc = acc_ref[...]
            vals = epi(acc, *[r[...] for r in rest[:ne]]) if epi else (acc,)
            for o_ref, val in zip(rest[ne:-1], vals):
                o_ref[...] = val.astype(o_ref.dtype)

    tile = pl.BlockSpec((tm, tno), lambda i, j: (i, j))
    return pl.pallas_call(
        body, name=name, grid=grid,
        in_specs=[a_spec, w_spec] + [tile] * ne,
        out_specs=[tile] * len(out_dtypes),
        out_shape=[jax.ShapeDtypeStruct((M, No), d) for d in out_dtypes],
        scratch_shapes=scratch,
        compiler_params=_params("parallel", "parallel"),
    )(a, w, *extras)


def _mm_tn(name, a, g, layout, tl, group=1, layer=0, layers=1, into=None):
    L, Ka = a.shape
    N = g.shape[1]
    nl = L // tl
    if layout == "col":
        nb = N // N_DEV
        grid = (N_DEV // group, nl)
        in_specs = [pl.BlockSpec((tl, Ka), lambda j, k: (k, 0)), pl.BlockSpec((tl, group * nb), lambda j, k: (k, j))]
        out_spec = pl.BlockSpec((group, None, Ka, nb), lambda j, k: (j, layer, 0, 0))
        out_shape = jax.ShapeDtypeStruct((N_DEV, layers, Ka, nb), F32)

        def body(a_ref, g_ref, *rest):
            o_ref = rest[-1]

            @pl.when(pl.program_id(1) == 0)
            def _():
                o_ref[...] = jnp.zeros((group, Ka, nb), F32)

            av = a_ref[...]
            for t in range(group):
                o_ref[t] += lax.dot_general(av, g_ref[:, t * nb:(t + 1) * nb], TN_DIMS, preferred_element_type=F32)
    else:
        kb = Ka // N_DEV
        grid = (N_DEV // group, nl)
        in_specs = [pl.BlockSpec((tl, group * kb), lambda j, k: (k, j)), pl.BlockSpec((tl, N), lambda j, k: (k, 0))]
        out_spec = pl.BlockSpec((group, None, kb, N), lambda j, k: (j, layer, 0, 0))
        out_shape = jax.ShapeDtypeStruct((N_DEV, layers, kb, N), F32)

        def body(a_ref, g_ref, *rest):
            o_ref = rest[-1]

            @pl.when(pl.program_id(1) == 0)
            def _():
                o_ref[...] = jnp.zeros((group, kb, N), F32)

            gv = g_ref[...]
            if kb >= 2 * BLK:
                for t in range(group):
                    o_ref[t] += lax.dot_general(a_ref[:, t * kb:(t + 1) * kb], gv, TN_DIMS, preferred_element_type=F32)
            else:
                whole = lax.dot_general(a_ref[...], gv, TN_DIMS, preferred_element_type=F32)
                for t in range(group):
                    o_ref[t] += whole[t * kb:(t + 1) * kb]

    if into is None:
        return pl.pallas_call(
            body, name=name, grid=grid, in_specs=in_specs, out_specs=out_spec, out_shape=out_shape,
            compiler_params=_params("parallel", "arbitrary"),
        )(a, g)
    return pl.pallas_call(
        body, name=name, grid=grid, in_specs=in_specs + [ANY], out_specs=out_spec, out_shape=out_shape,
        input_output_aliases={2: 0}, compiler_params=_params("parallel", "arbitrary"),
    )(a, g, into)


def _rstd(x):
    return lax.rsqrt(jnp.mean(x * x, axis=-1, keepdims=True) + RMS_EPS)


def _norm_fwd(name, x, gain):
    L, D = x.shape
    tr = _row_tile(L, 640)

    def body(x_ref, g_ref, o_ref):
        xv = x_ref[...]
        o_ref[...] = (xv * _rstd(xv) * g_ref[...]).astype(BF16)

    return pl.pallas_call(
        body, name=name, grid=(L // tr,),
        in_specs=[pl.BlockSpec((tr, D), lambda i: (i, 0)), pl.BlockSpec((1, D), lambda i: (0, 0))],
        out_specs=pl.BlockSpec((tr, D), lambda i: (i, 0)),
        out_shape=jax.ShapeDtypeStruct((L, D), BF16),
        compiler_params=_params("parallel"),
    )(x, gain)


def _res_norm(name, h, branch, g_post, g_pre):
    L, D = h.shape
    tr = _row_tile(L, 640)
    npre = len(g_pre)

    def body(h_ref, b_ref, gp_ref, *rest):
        bv = b_ref[...]
        hn = h_ref[...] + bv * _rstd(bv) * gp_ref[...]
        rest[npre][...] = hn
        if npre:
            xh = hn * _rstd(hn)
            for t in range(npre):
                rest[npre + 1 + t][...] = (xh * rest[t][...]).astype(BF16)

    row = pl.BlockSpec((tr, D), lambda i: (i, 0))
    gain = pl.BlockSpec((1, D), lambda i: (0, 0))
    return pl.pallas_call(
        body, name=name, grid=(L // tr,),
        in_specs=[row, row, gain] + [gain] * npre,
        out_specs=[row] * (1 + npre),
        out_shape=[jax.ShapeDtypeStruct((L, D), F32)] + [jax.ShapeDtypeStruct((L, D), BF16)] * npre,
        compiler_params=_params("parallel"),
    )(h, branch, g_post, *g_pre)


def _norm_bwd(name, dres, stream=None, pre=(), post=None):
    L, D = dres.shape
    tr = _row_tile(L, 320)
    npre = len(pre)
    has_post = post is not None
    n_in = 1 + (1 + 2 * npre if npre else 0) + (2 if has_post else 0)
    n_dg = npre + (1 if has_post else 0)

    def body(*refs):
        ins, outs = refs[:n_in], refs[n_in:]
        i = pl.program_id(0)
        dsum = ins[0][...]
        dgs = []
        pos = 1
        if npre:
            xv = ins[pos][...]
            pos += 1
            rs = _rstd(xv)
            xh = xv * rs
            dxh = None
            for t in range(npre):
                gv, dy = ins[pos][...], ins[pos + 1][...]
                pos += 2
                dgs.append(jnp.sum(dy * xh, axis=0, keepdims=True))
                term = dy * gv
                dxh = term if dxh is None else dxh + term
            dsum = dsum + rs * (dxh - xh * jnp.mean(dxh * xh, axis=-1, keepdims=True))
        outs[0][...] = dsum
        o = 1
        if has_post:
            bv, gp = ins[pos][...], ins[pos + 1][...]
            rs = _rstd(bv)
            bh = bv * rs
            dgs.append(jnp.sum(dsum * bh, axis=0, keepdims=True))
            dbh = dsum * gp
            outs[1][...] = (rs * (dbh - bh * jnp.mean(dbh * bh, axis=-1, keepdims=True))).astype(BF16)
            o = 2

        @pl.when(i == 0)
        def _():
            for t in range(n_dg):
                outs[o + t][...] = jnp.zeros((8, D), F32)

        for t in range(n_dg):
            outs[o + t][0:1, :] += dgs[t]

    row = pl.BlockSpec((tr, D), lambda i: (i, 0))
    gain = pl.BlockSpec((1, D), lambda i: (0, 0))
    acc = pl.BlockSpec((8, D), lambda i: (0, 0))
    args, in_specs = [dres], [row]
    if npre:
        args.append(stream)
        in_specs.append(row)
        for gv, dy in pre:
            args += [gv, dy]
            in_specs += [gain, row]
    if has_post:
        args += [post[0], post[1]]
        in_specs += [row, gain]
    out_specs = [row] + ([row] if has_post else []) + [acc] * n_dg
    out_shape = ([jax.ShapeDtypeStruct((L, D), F32)] + ([jax.ShapeDtypeStruct((L, D), BF16)] if has_post else [])
                 + [jax.ShapeDtypeStruct((8, D), F32)] * n_dg)
    return pl.pallas_call(
        body, name=name, grid=(L // tr,), in_specs=in_specs, out_specs=out_specs, out_shape=out_shape,
        compiler_params=_params("arbitrary"),
    )(*args)


def _conv_rows(L):
    return max(t for t in range(BLK, 5 * BLK + 1, BLK) if L % t == 0)


def _conv_fwd(name, p, taps):
    L, D3 = p.shape
    D = D3 // 3
    tr = _conv_rows(L)

    def body(p_ref, prev_ref, w_ref, y_ref, u_scr):
        i = pl.program_id(0)
        for lo in range(0, D, BLK):
            cb, cc, cx = slice(lo, lo + BLK), slice(D + lo, D + lo + BLK), slice(2 * D + lo, 2 * D + lo + BLK)
            u_scr[0:8, :] = jnp.where(i > 0, prev_ref[:, cc] * prev_ref[:, cx], 0.0)
            for r0 in range(0, tr, BLK):
                u_scr[8 + r0:8 + r0 + BLK, :] = p_ref[r0:r0 + BLK, cc] * p_ref[r0:r0 + BLK, cx]
            for r0 in range(0, tr, BLK):
                conv = (w_ref[0:1, cb] * u_scr[6 + r0:6 + r0 + BLK, :] + w_ref[1:2, cb] * u_scr[7 + r0:7 + r0 + BLK, :]
                        + w_ref[2:3, cb] * u_scr[8 + r0:8 + r0 + BLK, :])
                y_ref[r0:r0 + BLK, cb] = (p_ref[r0:r0 + BLK, cb] * conv).astype(BF16)

    return pl.pallas_call(
        body, name=name, grid=(L // tr,),
        in_specs=[pl.BlockSpec((tr, D3), lambda i: (i, 0)),
                  pl.BlockSpec((8, D3), lambda i: (jnp.maximum(i * (tr // 8) - 1, 0), 0)),
                  pl.BlockSpec((8, D), lambda i: (0, 0))],
        out_specs=pl.BlockSpec((tr, D), lambda i: (i, 0)),
        out_shape=jax.ShapeDtypeStruct((L, D), BF16),
        scratch_shapes=[pltpu.VMEM((tr + 8, BLK), F32)],
        compiler_params=_params("parallel"),
    )(p, p, taps)


def _conv_bwd(name, p, dy, taps):
    L, D3 = p.shape
    D = D3 // 3
    tr = _conv_rows(L)
    nblk = L // tr
    last8 = L // 8 - 1

    def body(p_ref, prev_ref, next_ref, dy_ref, dyn_ref, w_ref, dp_ref, dw_ref, u_scr, d_scr):
        i = pl.program_id(0)

        @pl.when(i == 0)
        def _():
            dw_ref[...] = jnp.zeros((8, D), F32)

        for lo in range(0, D, BLK):
            cb, cc, cx = slice(lo, lo + BLK), slice(D + lo, D + lo + BLK), slice(2 * D + lo, 2 * D + lo + BLK)
            w0, w1, w2 = w_ref[0:1, cb], w_ref[1:2, cb], w_ref[2:3, cb]
            u_scr[0:8, :] = jnp.where(i > 0, prev_ref[:, cc] * prev_ref[:, cx], 0.0)
            d_scr[tr:tr + 8, :] = jnp.where(i < nblk - 1, dyn_ref[:, cb] * next_ref[:, cb], 0.0)
            for r0 in range(0, tr, BLK):
                rows = slice(r0, r0 + BLK)
                u_scr[8 + r0:8 + r0 + BLK, :] = p_ref[rows, cc] * p_ref[rows, cx]
                d_scr[rows, :] = dy_ref[rows, cb] * p_ref[rows, cb]
            dws = [None, None, None]
            for r0 in range(0, tr, BLK):
                rows = slice(r0, r0 + BLK)
                us = [u_scr[6 + r0:6 + r0 + BLK, :], u_scr[7 + r0:7 + r0 + BLK, :], u_scr[8 + r0:8 + r0 + BLK, :]]
                conv = w0 * us[0] + w1 * us[1] + w2 * us[2]
                dconv = d_scr[rows, :]
                du = w2 * dconv + w1 * d_scr[1 + r0:1 + r0 + BLK, :] + w0 * d_scr[2 + r0:2 + r0 + BLK, :]
                dp_ref[rows, cb] = (dy_ref[rows, cb] * conv).astype(BF16)
                dp_ref[rows, cc] = (du * p_ref[rows, cx]).astype(BF16)
                dp_ref[rows, cx] = (du * p_ref[rows, cc]).astype(BF16)
                for t in range(3):
                    part = jnp.sum(dconv * us[t], axis=0, keepdims=True)
                    dws[t] = part if dws[t] is None else dws[t] + part
            for t in range(3):
                dw_ref[t:t + 1, cb] += dws[t]

    halo_prev = lambda i: (jnp.maximum(i * (tr // 8) - 1, 0), 0)
    halo_next = lambda i: (jnp.minimum((i + 1) * (tr // 8), last8), 0)
    return pl.pallas_call(
        body, name=name, grid=(nblk,),
        in_specs=[pl.BlockSpec((tr, D3), lambda i: (i, 0)), pl.BlockSpec((8, D3), halo_prev),
                  pl.BlockSpec((8, D3), halo_next), pl.BlockSpec((tr, D), lambda i: (i, 0)),
                  pl.BlockSpec((8, D), halo_next), pl.BlockSpec((8, D), lambda i: (0, 0))],
        out_specs=[pl.BlockSpec((tr, D3), lambda i: (i, 0)), pl.BlockSpec((8, D), lambda i: (0, 0))],
        out_shape=[jax.ShapeDtypeStruct((L, D3), BF16), jax.ShapeDtypeStruct((8, D), F32)],
        scratch_shapes=[pltpu.VMEM((tr + 8, BLK), F32), pltpu.VMEM((tr + 8, BLK), F32)],
        compiler_params=_params("arbitrary"),
    )(p, p, p, dy, dy, taps)


KB_STEP = 3
FAR = 1 << 30


def _stack_heads(x):
    lane = lax.broadcasted_iota(jnp.int32, x.shape, 1)
    zero = jnp.zeros_like(x)
    return jnp.concatenate([jnp.where(lane < HEAD_DIM, x, zero), jnp.where(lane >= HEAD_DIM, x, zero)], axis=0)


def _unstack_heads(x2):
    lane = lax.broadcasted_iota(jnp.int32, (BLK, BLK), 1)
    return jnp.where(lane < HEAD_DIM, x2[0:BLK], x2[BLK:2 * BLK])


def _key_block(kb, u):
    kbu = kb - u
    off = pl.multiple_of(jnp.maximum(kbu, 0) * BLK, BLK)
    return off, jnp.where(kbu >= 0, kbu * BLK, FAR)


def _attn_iotas(i):
    lane = lax.broadcasted_iota(jnp.int32, (2 * BLK, BLK), 1)
    row = lax.broadcasted_iota(jnp.int32, (2 * BLK, BLK), 0)
    krow = lax.broadcasted_iota(jnp.int32, (BLK, BLK), 0)
    klane = lax.broadcasted_iota(jnp.int32, (BLK, BLK), 1)
    return lane, i * BLK + (row & (BLK - 1)), krow, klane


def _score_block(q2, kblk, kbase, qpos, lane, rr, tri):
    z = lax.dot_general(q2, kblk, NT_DIMS, preferred_element_type=F32)
    kpos = kbase + lane
    vis = (kpos < qpos) & (kpos >= PAD)
    sp = jnp.log(1.0 + jnp.exp(-jnp.abs(z)))
    lb = jnp.minimum(z, 0.0) - sp
    lom = jnp.where(vis, lb - z, 0.0)
    hi = lom.astype(BF16)
    lo = (lom - hi.astype(F32)).astype(BF16)
    cs = jnp.dot(hi, tri, preferred_element_type=F32) + jnp.dot(lo, tri, preferred_element_type=F32)
    a = jnp.where(vis, jnp.exp(lb + rr + cs), 0.0)
    return vis, lb, lom, a


def _attn_fwd(name, q, k, v, gather=None):
    L, D = q.shape
    nh, nq = D // BLK, L // BLK
    g_in = gather.operands() if gather else []
    ngi, ngo = len(g_in), (gather.n if gather else 0)

    def body(q_ref, k_ref, v_ref, *rest):
        of_ref, ob_ref, aw_ref, bw_ref, fl_ref = rest[ngi:ngi + 5]
        acc_ref, r_ref = rest[ngi + 5 + ngo:ngi + 7 + ngo]
        i = pl.program_id(1)
        if gather:
            start, forward, finish = gather.bind(rest[:ngi], rest[ngi + 5:ngi + 5 + ngo], rest[ngi + 7 + ngo:])
            step = pl.program_id(0) * nq + i
            pl.when(step == 0)(start)
            pl.when(step == nh * nq // 2)(forward)
        lane, qpos, krow, klane = _attn_iotas(i)
        tri = (krow > klane).astype(BF16)
        q2 = _stack_heads(q_ref[...])
        acc_ref[...] = jnp.zeros((2 * BLK, BLK), F32)
        r_ref[...] = jnp.zeros((2 * BLK, 1), F32)

        def blocks(kb, save):
            rr = r_ref[...]
            upd = None
            for u in range(KB_STEP):
                off, kbase = _key_block(kb, u)
                vis, lb, lom, a = _score_block(q2, k_ref[pl.ds(off, BLK), :], kbase, qpos, lane, rr, tri)
                ab = a.astype(BF16)
                if save:
                    aw_ref[:, u * BLK:(u + 1) * BLK] = ab
                    bw_ref[:, u * BLK:(u + 1) * BLK] = jnp.where(vis, jnp.exp(lb), 0.0).astype(BF16)
                part = jnp.dot(ab, v_ref[pl.ds(off, BLK), :], preferred_element_type=F32)
                upd = part if upd is None else upd + part
                rr = rr + jnp.sum(lom, axis=1, keepdims=True)
            acc_ref[...] += upd
            r_ref[...] = rr
            return (jnp.max(rr) < LOG_ZERO).astype(jnp.int32)

        done = blocks(i, True)
        kb_end, _ = lax.while_loop(lambda c: (c[0] >= 0) & (c[1] == 0),
                                   lambda c: (c[0] - KB_STEP, blocks(c[0], False)), (i - KB_STEP, done))
        fl_ref[...] = jnp.zeros((8, BLK), F32) + (kb_end == i - KB_STEP).astype(F32)
        o = _unstack_heads(acc_ref[...])
        of_ref[...] = o
        ob_ref[...] = o.astype(BF16)
        if gather:
            pl.when(step == nh * nq - 1)(finish)

    qspec = pl.BlockSpec((BLK, BLK), lambda h, i: (i, h))
    kvspec = pl.BlockSpec((L, BLK), lambda h, i: (0, h))
    wspec = pl.BlockSpec((None, None, 2 * BLK, KB_STEP * BLK), lambda h, i: (h, i, 0, 0))
    fspec = pl.BlockSpec((None, None, 8, BLK), lambda h, i: (h, i, 0, 0))
    return pl.pallas_call(
        body, name=name, grid=(nh, nq),
        in_specs=[qspec, kvspec, kvspec] + [ANY] * ngi, out_specs=[qspec, qspec, wspec, wspec, fspec] + [ANY] * ngo,
        out_shape=[jax.ShapeDtypeStruct((L, D), F32), jax.ShapeDtypeStruct((L, D), BF16),
                   jax.ShapeDtypeStruct((nh, nq, 2 * BLK, KB_STEP * BLK), BF16),
                   jax.ShapeDtypeStruct((nh, nq, 2 * BLK, KB_STEP * BLK), BF16),
                   jax.ShapeDtypeStruct((nh, nq, 8, BLK), F32)] + (gather.out_shapes() if gather else []),
        input_output_aliases=gather.aliases(3, 5) if gather else {},
        scratch_shapes=[pltpu.VMEM((2 * BLK, BLK), F32), pltpu.VMEM((2 * BLK, 1), F32)]
        + (gather.scratch() if gather else []),
        compiler_params=_params("arbitrary", "arbitrary") if gather else _params("parallel", "arbitrary"),
    )(q, k, v, *g_in)


def _attn_bwd(name, q, k, v, o, do, a_win, b_win, flag):
    L, D = q.shape

    def body(q_ref, k_ref, v_ref, o_ref, do_ref, aw_ref, bw_ref, fl_ref, dq_ref, dk_ref, dv_ref,
             acc_ref, r_ref, c_ref):
        i = pl.program_id(1)

        @pl.when(i == 0)
        def _():
            dk_ref[...] = jnp.zeros((L, BLK), F32)
            dv_ref[...] = jnp.zeros((L, BLK), F32)

        krow = lax.broadcasted_iota(jnp.int32, (BLK, BLK), 0)
        klane = lax.broadcasted_iota(jnp.int32, (BLK, BLK), 1)
        tri_incl = (krow >= klane).astype(BF16)
        q2 = _stack_heads(q_ref[...])
        do2 = _stack_heads(do_ref[...].astype(BF16))
        total = jnp.sum(do2.astype(F32) * jnp.concatenate([o_ref[...]] * 2, axis=0), axis=1, keepdims=True)

        def grads(off, kblk, ab, beta, cc):
            da = lax.dot_general(do2, v_ref[pl.ds(off, BLK), :], NT_DIMS, preferred_element_type=F32)
            g = da * ab.astype(F32)
            ghi = g.astype(BF16)
            glo = (g - ghi.astype(F32)).astype(BF16)
            sfx = (jnp.dot(ghi, tri_incl, preferred_element_type=F32)
                   + jnp.dot(glo, tri_incl, preferred_element_type=F32))
            before = total - cc - sfx
            dz = (g * (1.0 - beta) - beta * before).astype(BF16)
            dk_ref[pl.ds(off, BLK), :] += lax.dot_general(dz, q2, TN_DIMS, preferred_element_type=F32)
            dv_ref[pl.ds(off, BLK), :] += lax.dot_general(ab, do2, TN_DIMS, preferred_element_type=F32)
            return jnp.dot(dz, kblk, preferred_element_type=F32), cc + jnp.sum(g, axis=1, keepdims=True)

        fast = jnp.max(fl_ref[...]) > 0.5

        @pl.when(fast)
        def _():
            cc = jnp.zeros((2 * BLK, 1), F32)
            upd = None
            for u in range(KB_STEP):
                off, _ = _key_block(i, u)
                cols = slice(u * BLK, (u + 1) * BLK)
                part, cc = grads(off, k_ref[pl.ds(off, BLK), :], aw_ref[:, cols], bw_ref[:, cols].astype(F32), cc)
                upd = part if upd is None else upd + part
            dq_ref[...] = (_unstack_heads(upd) * ATTN_SCALE).astype(BF16)

        @pl.when(jnp.logical_not(fast))
        def _():
            lane, qpos, _, _ = _attn_iotas(i)
            tri = (krow > klane).astype(BF16)
            acc_ref[...] = jnp.zeros((2 * BLK, BLK), F32)
            r_ref[...] = jnp.zeros((2 * BLK, 1), F32)
            c_ref[...] = jnp.zeros((2 * BLK, 1), F32)

            def step(carry):
                kb, _ = carry
                rr = r_ref[...]
                cc = c_ref[...]
                upd = None
                for u in range(KB_STEP):
                    off, kbase = _key_block(kb, u)
                    kblk = k_ref[pl.ds(off, BLK), :]
                    vis, lb, lom, a = _score_block(q2, kblk, kbase, qpos, lane, rr, tri)
                    part, cc = grads(off, kblk, a.astype(BF16), jnp.where(vis, jnp.exp(lb), 0.0), cc)
                    upd = part if upd is None else upd + part
                    rr = rr + jnp.sum(lom, axis=1, keepdims=True)
                acc_ref[...] += upd
                c_ref[...] = cc
                r_ref[...] = rr
                return kb - KB_STEP, (jnp.max(rr) < LOG_ZERO).astype(jnp.int32)

            lax.while_loop(lambda c: (c[0] >= 0) & (c[1] == 0), step, (i, jnp.int32(0)))
            dq_ref[...] = (_unstack_heads(acc_ref[...]) * ATTN_SCALE).astype(BF16)

    qspec = pl.BlockSpec((BLK, BLK), lambda h, i: (i, h))
    kvspec = pl.BlockSpec((L, BLK), lambda h, i: (0, h))
    wspec = pl.BlockSpec((None, None, 2 * BLK, KB_STEP * BLK), lambda h, i: (h, i, 0, 0))
    fspec = pl.BlockSpec((None, None, 8, BLK), lambda h, i: (h, i, 0, 0))
    return pl.pallas_call(
        body, name=name, grid=(D // BLK, L // BLK),
        in_specs=[qspec, kvspec, kvspec, qspec, qspec, wspec, wspec, fspec], out_specs=[qspec, kvspec, kvspec],
        out_shape=[jax.ShapeDtypeStruct((L, D), BF16), jax.ShapeDtypeStruct((L, D), F32),
                   jax.ShapeDtypeStruct((L, D), F32)],
        scratch_shapes=[pltpu.VMEM((2 * BLK, BLK), F32), pltpu.VMEM((2 * BLK, 1), F32),
                        pltpu.VMEM((2 * BLK, 1), F32)],
        compiler_params=_params("parallel", "arbitrary"),
    )(q, k, v, o, do, a_win, b_win, flag)


def _loss_and_grad(name, h, target):
    L, D = h.shape
    first = (PAD + N_META) // BLK

    def body(h_ref, t_ref, sq_ref, dh_ref):
        i = pl.program_id(0)

        @pl.when(i == 0)
        def _():
            sq_ref[...] = jnp.zeros((8, BLK), F32)

        @pl.when(i < first)
        def _():
            dh_ref[...] = jnp.zeros((BLK, D), F32)

        @pl.when(i >= first)
        def _():
            err = h_ref[...] - t_ref[...]
            sq_ref[...] += jnp.sum(err * err)
            dh_ref[...] = err * (1.0 / D)

    return pl.pallas_call(
        body, name=name, grid=(L // BLK,),
        in_specs=[pl.BlockSpec((BLK, D), lambda i: (i, 0)),
                  pl.BlockSpec((BLK, D), lambda i: (jnp.maximum(i - first, 0), 0))],
        out_specs=[pl.BlockSpec((8, BLK), lambda i: (0, 0)), pl.BlockSpec((BLK, D), lambda i: (i, 0))],
        out_shape=[jax.ShapeDtypeStruct((8, BLK), F32), jax.ShapeDtypeStruct((L, D), F32)],
        compiler_params=_params("arbitrary"),
    )(h, target)


def _add_cast(name, a, b):
    L, D = a.shape
    tr = _row_tile(L, 640)

    def body(a_ref, b_ref, o_ref):
        o_ref[...] = (a_ref[...] + b_ref[...]).astype(BF16)

    row = pl.BlockSpec((tr, D), lambda i: (i, 0))
    return pl.pallas_call(body, name=name, grid=(L // tr,), in_specs=[row, row], out_specs=row,
                          out_shape=jax.ShapeDtypeStruct((L, D), BF16), compiler_params=_params("parallel"))(a, b)


def _adamw(w, g, m, v):
    m = ADAM_B1 * m + (1.0 - ADAM_B1) * g
    v = ADAM_B2 * v + (1.0 - ADAM_B2) * (g * g)
    m_hat = m / (1.0 - ADAM_B1 ** ADAM_STEP)
    v_hat = v / (1.0 - ADAM_B2 ** ADAM_STEP)
    delta = -ADAM_LR * (m_hat / (jnp.sqrt(v_hat) + ADAM_EPS) + ADAM_WD * w)
    return delta, m, v


def _adam_small(name, w, g, m, v):
    def body(w_ref, g_ref, m_ref, v_ref, d_ref, mo_ref, vo_ref):
        d, mn, vn = _adamw(w_ref[...], g_ref[...], m_ref[...], v_ref[...])
        d_ref[...] = d
        mo_ref[...] = mn
        vo_ref[...] = vn

    return pl.pallas_call(body, name=name, out_shape=[jax.ShapeDtypeStruct(w.shape, F32)] * 3)(w, g, m, v)


def _sum_slots(name, slots):
    def body(s_ref, o_ref):
        acc = s_ref[0]
        for d in range(1, N_DEV):
            acc = acc + s_ref[d]
        o_ref[...] = acc

    return pl.pallas_call(body, name=name, out_shape=jax.ShapeDtypeStruct(slots.shape[1:], F32))(slots)


def _pair_sum(name, g, recv, c_idx):
    _, R, C = g.shape
    tr = _row_tile(R, 512)

    def body(sel_ref, g_ref, r_ref, o_ref):
        o_ref[...] = (g_ref[...] + r_ref[...]).astype(BF16)

    return pl.pallas_call(
        body, name=name,
        grid_spec=pltpu.PrefetchScalarGridSpec(
            num_scalar_prefetch=1, grid=(4, R // tr),
            in_specs=[pl.BlockSpec((None, tr, C), lambda k, i, sel: (2 * k + sel[0], i, 0)),
                      pl.BlockSpec((None, tr, C), lambda k, i, sel: (k, i, 0))],
            out_specs=pl.BlockSpec((None, tr, C), lambda k, i, sel: (k, i, 0))),
        out_shape=jax.ShapeDtypeStruct((4, R, C), BF16),
        compiler_params=_params("parallel", "parallel"),
    )(c_idx, g, recv)


def _final_sum_adam(name, part, recv, chip_idx, w, m, v):
    R, C = w.shape
    tr = _row_tile(R, 512)

    def body(sel_ref, p_ref, r_ref, w_ref, m_ref, v_ref, g_ref, d_ref, mo_ref, vo_ref):
        g = ((p_ref[...].astype(F32) + r_ref[0].astype(F32)) + r_ref[1].astype(F32)) + r_ref[2].astype(F32)
        d, mn, vn = _adamw(w_ref[...], g, m_ref[...], v_ref[...])
        g_ref[...] = g
        d_ref[...] = d
        mo_ref[...] = mn
        vo_ref[...] = vn

    row = pl.BlockSpec((tr, C), lambda i, sel: (i, 0))
    return pl.pallas_call(
        body, name=name,
        grid_spec=pltpu.PrefetchScalarGridSpec(
            num_scalar_prefetch=1, grid=(R // tr,),
            in_specs=[pl.BlockSpec((None, tr, C), lambda i, sel: (sel[0], i, 0)),
                      pl.BlockSpec((3, tr, C), lambda i, sel: (0, i, 0)), row, row, row],
            out_specs=[row] * 4),
        out_shape=[jax.ShapeDtypeStruct((R, C), F32)] * 4,
        compiler_params=_params("parallel"),
    )(chip_idx, part, recv, w, m, v)


def _position():
    return lax.axis_index("x"), lax.axis_index("y"), lax.axis_index("c")


class _Gather:
    def __init__(self, items):
        self.items = items
        self.n = len(items)
        self.filled = [a for a, it in enumerate(items) if it[2] is not None]

    def operands(self):
        return [it[0] for it in self.items] + [self.items[a][2] for a in self.filled]

    def out_shapes(self):
        return [jax.ShapeDtypeStruct((N_DEV,) + it[0].shape, it[0].dtype) for it in self.items]

    def aliases(self, first_in, first_out):
        return {first_in + self.n + k: first_out + a for k, a in enumerate(self.filled)}

    def scratch(self):
        return [pltpu.SemaphoreType.DMA((self.n, 7)), pltpu.SemaphoreType.DMA((self.n, 7)),
                pltpu.SemaphoreType.DMA((self.n,))]

    def bind(self, in_refs, out_refs, sems):
        send_sems, recv_sems, local_sems = sems
        n, items = self.n, self.items
        x, y, c = _position()
        me, sibling = (x, y, c), (x, y, 1 - c)
        chips = [(1 - x, y), (x, 1 - y), (1 - x, 1 - y)]
        src = [in_refs[a].at[items[a][1]] for a in range(n)]

        def copy(a, sem, block, to, own=False):
            dst = out_refs[a].at[4 * block[0] + 2 * block[1] + block[2], items[a][1]]
            return pltpu.make_async_remote_copy(
                src_ref=src[a] if own else dst, dst_ref=dst,
                send_sem=send_sems.at[a, sem], recv_sem=recv_sems.at[a, sem], device_id=to, device_id_type=MESH)

        def local(a):
            return pltpu.make_async_copy(src[a], out_refs[a].at[4 * x + 2 * y + c, items[a][1]], local_sems.at[a])

        def first():
            return [cp for a in range(n) for cp in
                    [copy(a, 0, me, sibling, own=True)]
                    + [copy(a, 1 + j, me, (*chip, c), own=True) for j, chip in enumerate(chips)]]

        def start():
            for a in range(n):
                local(a).start()
            for cp in first():
                cp.start()

        def forward():
            for j, chip in enumerate(chips):
                for a in range(n):
                    copy(a, 1 + j, (*chip, c), me).wait_recv()
                    copy(a, 4 + j, (*chip, c), sibling).start()

        def finish():
            for a in range(n):
                copy(a, 0, sibling, me).wait_recv()
            for j, chip in enumerate(chips):
                for a in range(n):
                    copy(a, 4 + j, (*chip, 1 - c), me).wait_recv()
            for cp in first():
                cp.wait_send()
            for j, chip in enumerate(chips):
                for a in range(n):
                    copy(a, 4 + j, (*chip, c), sibling).wait_send()
            for a in range(n):
                local(a).wait()

        return start, forward, finish


def _all_gather(name, gather):
    n_in = len(gather.operands())

    def body(*refs):
        start, forward, finish = gather.bind(refs[:n_in], refs[n_in:n_in + gather.n], refs[n_in + gather.n:])
        start()
        forward()
        finish()

    return pl.pallas_call(
        body, name=name, in_specs=[ANY] * n_in, out_specs=[ANY] * gather.n, out_shape=gather.out_shapes(),
        input_output_aliases=gather.aliases(0, 0), scratch_shapes=gather.scratch(),
        compiler_params=pltpu.CompilerParams(has_side_effects=True),
    )(*gather.operands())


def _exchange_sibling(name, grads):
    n = len(grads)

    def body(*refs):
        g_refs, r_refs = refs[:n], refs[n:2 * n]
        send_sems, recv_sems = refs[2 * n:]
        x, y, c = _position()
        copies = []
        for a in range(n):
            for k in range(4):
                copies.append(pltpu.make_async_remote_copy(
                    src_ref=g_refs[a].at[2 * k + (1 - c)], dst_ref=r_refs[a].at[k],
                    send_sem=send_sems.at[a, k], recv_sem=recv_sems.at[a, k],
                    device_id=(x, y, 1 - c), device_id_type=MESH))
        for cp in copies:
            cp.start()
        for cp in copies:
            cp.wait()

    return pl.pallas_call(
        body, name=name, in_specs=[ANY] * n, out_specs=[ANY] * n,
        out_shape=[jax.ShapeDtypeStruct((4,) + g.shape[1:], g.dtype) for g in grads],
        scratch_shapes=[pltpu.SemaphoreType.DMA((n, 4)), pltpu.SemaphoreType.DMA((n, 4))],
        compiler_params=pltpu.CompilerParams(has_side_effects=True),
    )(*grads)


def _exchange_chips(name, parts):
    n = len(parts)

    def body(*refs):
        p_refs, r_refs = refs[:n], refs[n:2 * n]
        send_sems, recv_sems = refs[2 * n:]
        x, y, c = _position()
        chips = [(1 - x, y), (x, 1 - y), (1 - x, 1 - y)]
        copies = []
        for a in range(n):
            for j, chip in enumerate(chips):
                copies.append(pltpu.make_async_remote_copy(
                    src_ref=p_refs[a].at[2 * chip[0] + chip[1]], dst_ref=r_refs[a].at[j],
                    send_sem=send_sems.at[a, j], recv_sem=recv_sems.at[a, j],
                    device_id=(*chip, c), device_id_type=MESH))
        for cp in copies:
            cp.start()
        for cp in copies:
            cp.wait()

    return pl.pallas_call(
        body, name=name, in_specs=[ANY] * n, out_specs=[ANY] * n,
        out_shape=[jax.ShapeDtypeStruct((3,) + p.shape[1:], p.dtype) for p in parts],
        scratch_shapes=[pltpu.SemaphoreType.DMA((n, 3)), pltpu.SemaphoreType.DMA((n, 3))],
        compiler_params=pltpu.CompilerParams(has_side_effects=True),
    )(*parts)


def kernel(x, meta_tokens, norm_gains, conv_in_proj, conv_w, conv_out_proj, kv_norm, w_k, w_v, w_q, w_o, mlp_w1, mlp_w2, loss_target, m_meta_tokens, m_norm_gains, m_conv_in_proj, m_conv_w, m_conv_out_proj, m_kv_norm, m_w_k, m_w_v, m_w_q, m_w_o, m_mlp_w1, m_mlp_w2, v_meta_tokens, v_norm_gains, v_conv_in_proj, v_conv_w, v_conv_out_proj, v_kv_norm, v_w_k, v_w_v, v_w_q, v_w_o, v_mlp_w1, v_mlp_w2):
    xi, target = x[0], loss_target[0]
    S, D = xi.shape
    L = PAD + N_META + S
    dsh = D // N_DEV
    px, py, pc = _position()
    dev = 4 * px + 2 * py + pc
    tm_big = _row_tile(L, 1664)
    tm_mid = _row_tile(L, 640)

    def pack_small(meta, gains, taps):
        return jnp.concatenate([meta, gains.reshape(DEPTH * 4, dsh), taps.reshape(N_A * 3, dsh),
                                jnp.zeros((2, dsh), F32)], axis=0)

    big_w = [conv_in_proj, conv_out_proj, w_k[None], w_v[None], w_q, w_o, mlp_w1, mlp_w2]
    big_m = [m_conv_in_proj, m_conv_out_proj, m_w_k[None], m_w_v[None], m_w_q, m_w_o, m_mlp_w1, m_mlp_w2]
    big_v = [v_conv_in_proj, v_conv_out_proj, v_w_k[None], v_w_v[None], v_w_q, v_w_o, v_mlp_w1, v_mlp_w2]
    small_w = pack_small(meta_tokens, norm_gains, conv_w)
    w_names = ["win", "wout", "wk", "wv", "wq", "wo", "w1", "w2"]
    wb = {nm: w.astype(BF16) for nm, w in zip(w_names, big_w)}
    gw = {nm: None for nm in w_names}

    def gather_of(parts):
        return _Gather([(wb[nm], ly, gw[nm]) for nm, ly in parts])

    def mixer_weights(ly):
        if ly < N_A:
            return [("win", ly), ("wout", ly)]
        return [("wq", ly - N_A), ("wo", ly - N_A)] + ([("wk", 0), ("wv", 0)] if ly == N_A else [])

    parts0 = mixer_weights(0) + [("w1", 0), ("w2", 0)]
    got = _all_gather("gather_layer0", _Gather([(small_w[None], 0, None)] + gather_of(parts0).items))
    small_full = got[0][:, 0].transpose(1, 0, 2).reshape(40, D)
    for (nm, _), arr in zip(parts0, got[1:]):
        gw[nm] = arr
    meta_full = small_full[0:N_META]
    gain = lambda layer, n: small_full[N_META + 4 * layer + n][None]
    taps = [jnp.concatenate([small_full[32 + 3 * l:35 + 3 * l], jnp.zeros((5, D), F32)], axis=0) for l in range(N_A)]
    kvn = kv_norm[None]

    h = jnp.concatenate([jnp.zeros((PAD, D), F32), meta_full, xi], axis=0)
    n1 = _norm_fwd("norm_in", h, gain(0, 0))
    saved = []
    hk = k = v = None
    for layer in range(DEPTH):
        s = {"h0": h, "n1": n1}
        if layer < N_A:
            s["p"], gw["w2"] = _mm_nn(f"conv_in{layer}", n1, gw["win"], layer, "col", [F32], tm_big,
                                      gather=gather_of([("w2", layer + 1)]))
            s["y"] = _conv_fwd(f"conv{layer}", s["p"], taps[layer])
            mix = _mm_nn(f"conv_out{layer}", s["y"], gw["wout"], layer, "row", [F32], tm_big, tn=D)[0]
        else:
            j = layer - N_A
            if j == 0:
                k = _mm_nn("k_proj", hk, gw["wk"], 0, "row", [BF16], tm_big, tn=D)[0]
                v = _mm_nn("v_proj", hk, gw["wv"], 0, "row", [BF16], tm_big, tn=D)[0]
            s["q"] = _mm_nn(f"q_proj{j}", n1, gw["wq"], j, "row", [BF16], tm_big, tn=D,
                            epi=lambda acc: (acc * ATTN_SCALE,))[0]
            if j == 0:
                parts = mixer_weights(layer + 1) + [("w1", layer + 1), ("w2", layer + 1)]
                got = _attn_fwd(f"attn{j}", s["q"], k, v, gather=gather_of(parts))
                for (nm, _), arr in zip(parts, got[5:]):
                    gw[nm] = arr
            else:
                got = _attn_fwd(f"attn{j}", s["q"], k, v)
            s["o"], s["ob"], s["aw"], s["bw"], s["fl"] = got[:5]
            mix = _mm_nn(f"o_proj{j}", s["ob"], gw["wo"], j, "row", [F32], tm_big, tn=D)[0]
        s["mix"] = mix
        s["h1"], s["n3"] = _res_norm(f"mix_norm{layer}", h, mix, gain(layer, 1), [gain(layer, 2)])
        relu2 = lambda acc: (jnp.square(jnp.maximum(acc, 0.0)),)
        if layer < N_A:
            parts = mixer_weights(layer + 1)
            got = _mm_nn(f"mlp_up{layer}", s["n3"], gw["w1"], layer, "col", [BF16], tm_big, epi=relu2,
                         gather=gather_of(parts))
            for (nm, _), arr in zip(parts, got[1:]):
                gw[nm] = arr
            s["act"] = got[0]
            s["ff"], gw["w1"] = _mm_nn(f"mlp_down{layer}", s["act"], gw["w2"], layer, "row", [F32], tm_mid, tn=D,
                                       gather=gather_of([("w1", layer + 1)]))
        else:
            s["act"] = _mm_nn(f"mlp_up{layer}", s["n3"], gw["w1"], layer, "col", [BF16], tm_big, epi=relu2)[0]
            s["ff"] = _mm_nn(f"mlp_down{layer}", s["act"], gw["w2"], layer, "row", [F32], tm_mid, tn=D)[0]
        pre = [] if layer == DEPTH - 1 else [gain(layer + 1, 0)] + ([kvn] if layer == N_A - 1 else [])
        outs = _res_norm(f"mlp_norm{layer}", s["h1"], s["ff"], gain(layer, 3), pre)
        h = outs[0]
        if pre:
            n1 = outs[1]
        if layer == N_A - 1:
            hk = outs[2]
        saved.append(s)

    sq, dh = _loss_and_grad("loss", h, target)
    loss = lax.psum(0.5 * sq[0, 0] / D, ("x", "y", "c"))

    g_gain = [[None] * 4 for _ in range(DEPTH)]
    g_taps = [None] * N_A
    g_w = {n: None for n in ("win", "wout", "wq", "wo", "w1", "w2")}
    dk_parts, dv_parts = [], []
    g_kvn = None
    s = saved[DEPTH - 1]
    dh, dff, g_gain[DEPTH - 1][3] = _norm_bwd("bwd_top", dh, post=(s["ff"], gain(DEPTH - 1, 3)))
    for layer in reversed(range(DEPTH)):
        s = saved[layer]
        def relu_grad(acc, act):
            act = act.astype(F32)
            return (acc * (2.0 * act * lax.rsqrt(jnp.maximum(act, 1e-30))),)

        da1 = _mm_nt(f"mlp_down_bwd{layer}", dff, gw["w2"], layer, "row", [BF16], tm_big, epi=relu_grad,
                     extras=(s["act"],))[0]
        g_w["w2"] = _mm_tn(f"mlp_w2_grad{layer}", s["act"], dff, "row", tm_big, group=2, layer=layer,
                           layers=DEPTH, into=g_w["w2"])
        g_w["w1"] = _mm_tn(f"mlp_w1_grad{layer}", s["n3"], da1, "col", tm_big, group=4, layer=layer,
                           layers=DEPTH, into=g_w["w1"])
        dn3 = _mm_nt(f"mlp_up_bwd{layer}", da1, gw["w1"], layer, "col", [F32], tm_mid, tn=D)[0]
        dh, dmix, g_gain[layer][2], g_gain[layer][1] = _norm_bwd(
            f"bwd_mid{layer}", dh, stream=s["h1"], pre=[(gain(layer, 2), dn3)], post=(s["mix"], gain(layer, 1)))
        pre = []
        if layer < N_A:
            dy = _mm_nt(f"conv_out_bwd{layer}", dmix, gw["wout"], layer, "row", [F32], tm_big, group=4)[0]
            g_w["wout"] = _mm_tn(f"conv_out_grad{layer}", s["y"], dmix, "row", tm_big, group=4, layer=layer,
                                 layers=N_A, into=g_w["wout"])
            dp, g_taps[layer] = _conv_bwd(f"conv_bwd{layer}", s["p"], dy, taps[layer])
            g_w["win"] = _mm_tn(f"conv_in_grad{layer}", s["n1"], dp, "col", tm_big, group=4, layer=layer, layers=N_A,
                                into=g_w["win"])
            dn1 = _mm_nt(f"conv_in_bwd{layer}", dp, gw["win"], layer, "col", [F32], tm_mid, tn=D)[0]
        else:
            j = layer - N_A
            do = _mm_nt(f"o_proj_bwd{j}", dmix, gw["wo"], j, "row", [F32], tm_big, group=4)[0]
            g_w["wo"] = _mm_tn(f"o_proj_grad{j}", s["ob"], dmix, "row", tm_big, group=4, layer=j,
                               layers=DEPTH - N_A, into=g_w["wo"])
            dq, dk_j, dv_j = _attn_bwd(f"attn_bwd{j}", s["q"], k, v, s["o"], do, s["aw"], s["bw"], s["fl"])
            dk_parts.append(dk_j)
            dv_parts.append(dv_j)
            g_w["wq"] = _mm_tn(f"q_proj_grad{j}", s["n1"], dq, "row", tm_big, group=4, layer=j,
                               layers=DEPTH - N_A, into=g_w["wq"])
            dn1 = _mm_nt(f"q_proj_bwd{j}", dq, gw["wq"], j, "row", [F32], tm_big, group=4)[0]
            if j == 0:
                dkb = _add_cast("dk_sum", dk_parts[0], dk_parts[1])
                dvb = _add_cast("dv_sum", dv_parts[0], dv_parts[1])
                g_wk = _mm_tn("k_proj_grad", hk, dkb, "row", tm_big, group=4)
                g_wv = _mm_tn("v_proj_grad", hk, dvb, "row", tm_big, group=4)
                dhk_k = _mm_nt("k_proj_bwd", dkb, gw["wk"], 0, "row", [F32], tm_big, group=4)[0]
                dhk = _mm_nt("v_proj_bwd", dvb, gw["wv"], 0, "row", [F32], tm_big, group=4,
                             epi=lambda acc, other: (acc + other,), extras=(dhk_k,))[0]
                pre = [(kvn, dhk)]
        pre = [(gain(layer, 0), dn1)] + pre
        if layer > 0:
            sp = saved[layer - 1]
            outs = _norm_bwd(f"bwd_in{layer}", dh, stream=s["h0"], pre=pre, post=(sp["ff"], gain(layer - 1, 3)))
            dh, dff = outs[0], outs[1]
            g_gain[layer][0] = outs[2]
            if len(pre) == 2:
                g_kvn = outs[3]
            g_gain[layer - 1][3] = outs[-1]
        else:
            dh, g_gain[0][0] = _norm_bwd("bwd_in0", dh, stream=s["h0"], pre=pre)

    grad_x = dh[PAD + N_META:][None]
    g_meta = dh[PAD:PAD + N_META]

    small_g = jnp.concatenate(
        [g_meta] + [g_gain[l][n][0:1] for l in range(DEPTH) for n in range(4)]
        + [g_taps[l][0:3] for l in range(N_A)] + [g_kvn[0:1], jnp.zeros((1, D), F32)], axis=0)
    small_sum = _sum_slots("small_grad_sum", _all_gather("gather_small_grads", _Gather([(small_g[None], 0, None)]))[0][:, 0])
    small_mine = lax.dynamic_slice_in_dim(small_sum, dev * dsh, dsh, axis=1)
    small_m = pack_small(m_meta_tokens, m_norm_gains, m_conv_w)
    small_v = pack_small(v_meta_tokens, v_norm_gains, v_conv_w)
    small_d, small_mn, small_vn = _adam_small("adam_small", small_w, small_mine, small_m, small_v)
    pad8 = lambda a: jnp.concatenate([a[None], jnp.zeros((7, D), F32)], axis=0)
    g_kv = small_sum[38]
    kv_d, kv_mn, kv_vn = _adam_small("adam_kv_norm", pad8(kv_norm), pad8(g_kv), pad8(m_kv_norm), pad8(v_kv_norm))

    def unpack_small(a):
        return (a[0:N_META], a[N_META:N_META + 16].reshape(DEPTH, 4, dsh), a[32:38].reshape(N_A, 3, dsh))

    big_g = [g_w["win"], g_w["wout"], g_wk, g_wv, g_w["wq"], g_w["wo"], g_w["w1"], g_w["w2"]]
    flat = lambda a, lead: a.reshape(lead + (-1, a.shape[-1]))
    big_g = [flat(g, (N_DEV,)) for g in big_g]
    from_sibling = _exchange_sibling("grads_to_sibling", big_g)
    c_idx = jnp.reshape(pc, (1,)).astype(jnp.int32)
    chip_idx = jnp.reshape(2 * px + py, (1,)).astype(jnp.int32)
    names = ["conv_in", "conv_out", "w_k", "w_v", "w_q", "w_o", "mlp_w1", "mlp_w2"]
    parts = [_pair_sum(f"pair_sum_{nm}", g, r, c_idx) for nm, g, r in zip(names, big_g, from_sibling)]
    from_chips = _exchange_chips("grads_to_chips", parts)
    big_out = []
    for nm, part, recv, w, m, v2 in zip(names, parts, from_chips, big_w, big_m, big_v):
        res = _final_sum_adam(f"adam_{nm}", part, recv, chip_idx, flat(w, ()), flat(m, ()), flat(v2, ()))
        shape = w.shape[1:] if nm in ("w_k", "w_v") else w.shape
        big_out.append([r.reshape(shape) for r in res])

    def assemble(kind, small_parts, kv_part):
        meta_p, gains_p, taps_p = small_parts
        b = [o[kind] for o in big_out]
        return [meta_p, gains_p, b[0], taps_p, b[1], kv_part, b[2], b[3], b[4], b[5], b[6], b[7]]

    grads = assemble(0, unpack_small(small_mine), g_kv)
    deltas = assemble(1, unpack_small(small_d), kv_d[0])
    new_m = assemble(2, unpack_small(small_mn), kv_mn[0])
    new_v = assemble(3, unpack_small(small_vn), kv_vn[0])
    return (loss, grad_x, *grads, *deltas, *new_m, *new_v)
```

```python
import functools

import jax
import jax.numpy as jnp
from jax import lax
from jax.experimental import pallas as pl
from jax.experimental.pallas import tpu as pltpu

F32 = jnp.float32
BF16 = jnp.bfloat16
MESH = pl.DeviceIdType.MESH

N_DEV = 8
N_META = 16
BLK = 128
PAD = (-N_META) % BLK
HEAD_DIM = 64
DEPTH = 4
N_A = 2
RMS_EPS = 1e-6
ATTN_SCALE = HEAD_DIM ** -0.5
LOG_ZERO = -105.0
ADAM_LR, ADAM_B1, ADAM_B2, ADAM_EPS, ADAM_WD, ADAM_STEP = 0.001, 0.9, 0.999, 1e-08, 0.01, 10
VMEM_LIMIT = 56 * 2 ** 20

NT_DIMS = (((1,), (1,)), ((), ()))
TN_DIMS = (((0,), (0,)), ((), ()))
ANY = pl.BlockSpec(memory_space=pl.ANY)


def _params(*sem):
    return pltpu.CompilerParams(dimension_semantics=sem, vmem_limit_bytes=VMEM_LIMIT)


def _row_tile(rows, pref):
    best = None
    for t in range(16, min(rows, pref) + 1, 16):
        if rows % t == 0:
            best = t
    assert best is not None, (rows, pref)
    return best


def _row_chunks(rows, parts=4):
    if rows % (16 * parts):
        return [slice(0, rows)]
    step = rows // parts
    return [slice(r, r + step) for r in range(0, rows, step)]


def _mm_nn(name, a, w, layer, layout, out_dtypes, tm, tn=None, epi=None, extras=(), gather=None):
    M, K = a.shape
    if layout == "col":
        nb = w.shape[3]
        N, tn = N_DEV * nb, nb
        w_spec = pl.BlockSpec((None, None, K, nb), lambda i, j: (j, layer, 0, 0))
    else:
        kb, N = w.shape[2], w.shape[3]
        assert N_DEV * kb == K
        w_spec = pl.BlockSpec((N_DEV, None, kb, tn), lambda i, j: (0, layer, 0, j))
    ne, no = len(extras), len(out_dtypes)
    g_in = gather.operands() if gather else []
    ngi, ngo = len(g_in), (gather.n if gather else 0)
    nj = N // tn
    steps = (M // tm) * nj

    def body(a_ref, w_ref, *rest):
        outs = rest[ne + ngi:ne + ngi + no]
        if gather:
            start, forward, finish = gather.bind(rest[ne:ne + ngi], rest[ne + ngi + no:ne + ngi + no + ngo],
                                                 rest[ne + ngi + no + ngo:])
            step = pl.program_id(0) * nj + pl.program_id(1)
            pl.when(step == 0)(start)
        wv = w_ref[...]
        if layout == "row":
            wv = wv.reshape(K, tn)
        for rows in _row_chunks(tm):
            acc = jnp.dot(a_ref[rows, :], wv, preferred_element_type=F32)
            vals = epi(acc, *[r[rows, :] for r in rest[:ne]]) if epi else (acc,)
            for o_ref, val in zip(outs, vals):
                o_ref[rows, :] = val.astype(o_ref.dtype)
        if gather:
            pl.when(step == steps // 2)(forward)
            pl.when(step == steps - 1)(finish)

    tile = pl.BlockSpec((tm, tn), lambda i, j: (i, j))
    return pl.pallas_call(
        body, name=name, grid=(M // tm, nj),
        in_specs=[pl.BlockSpec((tm, K), lambda i, j: (i, 0)), w_spec] + [tile] * ne + [ANY] * ngi,
        out_specs=[tile] * no + [ANY] * ngo,
        out_shape=[jax.ShapeDtypeStruct((M, N), d) for d in out_dtypes] + (gather.out_shapes() if gather else []),
        input_output_aliases=gather.aliases(2 + ne, no) if gather else {},
        scratch_shapes=gather.scratch() if gather else [],
        compiler_params=_params("arbitrary", "arbitrary") if gather else _params("parallel", "parallel"),
    )(a, w, *extras, *g_in)


def _mm_nt(name, a, w, layer, layout, out_dtypes, tm, tn=None, group=1, epi=None, extras=()):
    M, Nc = a.shape
    ne = len(extras)
    if layout == "row":
        kb = w.shape[2]
        No, tno = N_DEV * kb, group * kb
        w_spec = pl.BlockSpec((group, None, kb, Nc), lambda i, j: (j, layer, 0, 0))
        a_spec = pl.BlockSpec((tm, Nc), lambda i, j: (i, 0))
        grid = (M // tm, N_DEV // group)
        scratch = []

        def body(a_ref, w_ref, *rest):
            wv = w_ref[...].reshape(tno, Nc)
            for rows in _row_chunks(tm):
                acc = lax.dot_general(a_ref[rows, :], wv, NT_DIMS, preferred_element_type=F32)
                vals = epi(acc, *[r[rows, :] for r in rest[:ne]]) if epi else (acc,)
                for o_ref, val in zip(rest[ne:], vals):
                    o_ref[rows, :] = val.astype(o_ref.dtype)
    else:
        No, nb = w.shape[2], w.shape[3]
        assert N_DEV * nb == Nc
        tno = tn
        w_spec = pl.BlockSpec((N_DEV, None, tn, nb), lambda i, j: (0, layer, j, 0))
        a_spec = pl.BlockSpec((tm, Nc), lambda i, j: (i, 0))
        grid = (M // tm, No // tn)
        scratch = [pltpu.VMEM((tm, tn), F32)]

        def body(a_ref, w_ref, *rest):
            acc_ref = rest[-1]
            for d in range(N_DEV):
                part = lax.dot_general(a_ref[:, d * nb:(d + 1) * nb], w_ref[d], NT_DIMS,
                                       preferred_element_type=F32)
                if d == 0:
                    acc_ref[...] = part
                else:
                    acc_ref[...] += part
            acc = acc_ref[...]
            vals = epi(acc, *[r[...] for r in rest[:ne]]) if epi else (acc,)
            for o_ref, val in zip(rest[ne:-1], vals):
                o_ref[...] = val.astype(o_ref.dtype)

    tile = pl.BlockSpec((tm, tno), lambda i, j: (i, j))
    return pl.pallas_call(
        body, name=name, grid=grid,
        in_specs=[a_spec, w_spec] + [tile] * ne,
        out_specs=[tile] * len(out_dtypes),
        out_shape=[jax.ShapeDtypeStruct((M, No), d) for d in out_dtypes],
        scratch_shapes=scratch,
        compiler_params=_params("parallel", "parallel"),
    )(a, w, *extras)


def _mm_tn(name, a, g, layout, tl, group=1, layer=0, layers=1, into=None, wire=False):
    L, Ka = a.shape
    N = g.shape[1]
    nl = L // tl
    out_dtype = BF16 if wire else F32

    def refs(rest):
        return (rest[-2], rest[-1]) if wire else (rest[-1], rest[-1])

    def store(o_ref, acc_ref):
        if wire:
            @pl.when(pl.program_id(1) == nl - 1)
            def _():
                o_ref[...] = acc_ref[...].astype(BF16)
    if layout == "col":
        nb = N // N_DEV
        grid = (N_DEV // group, nl)
        in_specs = [pl.BlockSpec((tl, Ka), lambda j, k: (k, 0)), pl.BlockSpec((tl, group * nb), lambda j, k: (k, j))]
        out_spec = pl.BlockSpec((group, None, Ka, nb), lambda j, k: (j, layer, 0, 0))
        out_shape = jax.ShapeDtypeStruct((N_DEV, layers, Ka, nb), out_dtype)
        block = (group, Ka, nb)

        def body(a_ref, g_ref, *rest):
            o_ref, acc_ref = refs(rest)

            @pl.when(pl.program_id(1) == 0)
            def _():
                acc_ref[...] = jnp.zeros(block, F32)

            av = a_ref[...]
            for t in range(group):
                acc_ref[t] += lax.dot_general(av, g_ref[:, t * nb:(t + 1) * nb], TN_DIMS, preferred_element_type=F32)
            store(o_ref, acc_ref)
    else:
        kb = Ka // N_DEV
        grid = (N_DEV // group, nl)
        in_specs = [pl.BlockSpec((tl, group * kb), lambda j, k: (k, j)), pl.BlockSpec((tl, N), lambda j, k: (k, 0))]
        out_spec = pl.BlockSpec((group, None, kb, N), lambda j, k: (j, layer, 0, 0))
        out_shape = jax.ShapeDtypeStruct((N_DEV, layers, kb, N), out_dtype)
        block = (group, kb, N)

        def body(a_ref, g_ref, *rest):
            o_ref, acc_ref = refs(rest)

            @pl.when(pl.program_id(1) == 0)
            def _():
                acc_ref[...] = jnp.zeros(block, F32)

            gv = g_ref[...]
            if kb >= 2 * BLK:
                for t in range(group):
                    acc_ref[t] += lax.dot_general(a_ref[:, t * kb:(t + 1) * kb], gv, TN_DIMS,
                                                  preferred_element_type=F32)
            else:
                whole = lax.dot_general(a_ref[...], gv, TN_DIMS, preferred_element_type=F32)
                for t in range(group):
                    acc_ref[t] += whole[t * kb:(t + 1) * kb]
            store(o_ref, acc_ref)

    scratch = [pltpu.VMEM(block, F32)] if wire else []
    if into is None:
        return pl.pallas_call(
            body, name=name, grid=grid, in_specs=in_specs, out_specs=out_spec, out_shape=out_shape,
            scratch_shapes=scratch, compiler_params=_params("parallel", "arbitrary"),
        )(a, g)
    return pl.pallas_call(
        body, name=name, grid=grid, in_specs=in_specs + [ANY], out_specs=out_spec, out_shape=out_shape,
        scratch_shapes=scratch, input_output_aliases={2: 0}, compiler_params=_params("parallel", "arbitrary"),
    )(a, g, into)


def _rstd(x):
    return lax.rsqrt(jnp.mean(x * x, axis=-1, keepdims=True) + RMS_EPS)


def _norm_fwd(name, x, gain):
    L, D = x.shape
    tr = _row_tile(L, 640)

    def body(x_ref, g_ref, o_ref):
        xv = x_ref[...]
        o_ref[...] = (xv * _rstd(xv) * g_ref[...]).astype(BF16)

    return pl.pallas_call(
        body, name=name, grid=(L // tr,),
        in_specs=[pl.BlockSpec((tr, D), lambda i: (i, 0)), pl.BlockSpec((1, D), lambda i: (0, 0))],
        out_specs=pl.BlockSpec((tr, D), lambda i: (i, 0)),
        out_shape=jax.ShapeDtypeStruct((L, D), BF16),
        compiler_params=_params("parallel"),
    )(x, gain)


def _res_norm(name, h, branch, g_post, g_pre):
    L, D = h.shape
    tr = _row_tile(L, 640)
    npre = len(g_pre)

    def body(h_ref, b_ref, gp_ref, *rest):
        bv = b_ref[...]
        hn = h_ref[...] + bv * _rstd(bv) * gp_ref[...]
        rest[npre][...] = hn
        if npre:
            xh = hn * _rstd(hn)
            for t in range(npre):
                rest[npre + 1 + t][...] = (xh * rest[t][...]).astype(BF16)

    row = pl.BlockSpec((tr, D), lambda i: (i, 0))
    gain = pl.BlockSpec((1, D), lambda i: (0, 0))
    return pl.pallas_call(
        body, name=name, grid=(L // tr,),
        in_specs=[row, row, gain] + [gain] * npre,
        out_specs=[row] * (1 + npre),
        out_shape=[jax.ShapeDtypeStruct((L, D), F32)] + [jax.ShapeDtypeStruct((L, D), BF16)] * npre,
        compiler_params=_params("parallel"),
    )(h, branch, g_post, *g_pre)


def _norm_bwd(name, dres, stream=None, pre=(), post=None):
    L, D = dres.shape
    tr = _row_tile(L, 320)
    npre = len(pre)
    has_post = post is not None
    n_in = 1 + (1 + 2 * npre if npre else 0) + (2 if has_post else 0)
    n_dg = npre + (1 if has_post else 0)

    def body(*refs):
        ins, outs = refs[:n_in], refs[n_in:]
        i = pl.program_id(0)
        dsum = ins[0][...]
        dgs = []
        pos = 1
        if npre:
            xv = ins[pos][...]
            pos += 1
            rs = _rstd(xv)
            xh = xv * rs
            dxh = None
            for t in range(npre):
                gv, dy = ins[pos][...], ins[pos + 1][...]
                pos += 2
                dgs.append(jnp.sum(dy * xh, axis=0, keepdims=True))
                term = dy * gv
                dxh = term if dxh is None else dxh + term
            dsum = dsum + rs * (dxh - xh * jnp.mean(dxh * xh, axis=-1, keepdims=True))
        outs[0][...] = dsum
        o = 1
        if has_post:
            bv, gp = ins[pos][...], ins[pos + 1][...]
            rs = _rstd(bv)
            bh = bv * rs
            dgs.append(jnp.sum(dsum * bh, axis=0, keepdims=True))
            dbh = dsum * gp
            outs[1][...] = (rs * (dbh - bh * jnp.mean(dbh * bh, axis=-1, keepdims=True))).astype(BF16)
            o = 2

        @pl.when(i == 0)
        def _():
            for t in range(n_dg):
                outs[o + t][...] = jnp.zeros((8, D), F32)

        for t in range(n_dg):
            outs[o + t][0:1, :] += dgs[t]

    row = pl.BlockSpec((tr, D), lambda i: (i, 0))
    gain = pl.BlockSpec((1, D), lambda i: (0, 0))
    acc = pl.BlockSpec((8, D), lambda i: (0, 0))
    args, in_specs = [dres], [row]
    if npre:
        args.append(stream)
        in_specs.append(row)
        for gv, dy in pre:
            args += [gv, dy]
            in_specs += [gain, row]
    if has_post:
        args += [post[0], post[1]]
        in_specs += [row, gain]
    out_specs = [row] + ([row] if has_post else []) + [acc] * n_dg
    out_shape = ([jax.ShapeDtypeStruct((L, D), F32)] + ([jax.ShapeDtypeStruct((L, D), BF16)] if has_post else [])
                 + [jax.ShapeDtypeStruct((8, D), F32)] * n_dg)
    return pl.pallas_call(
        body, name=name, grid=(L // tr,), in_specs=in_specs, out_specs=out_specs, out_shape=out_shape,
        compiler_params=_params("arbitrary"),
    )(*args)


def _conv_rows(L):
    return max(t for t in range(BLK, 5 * BLK + 1, BLK) if L % t == 0)


def _conv_fwd(name, p, taps):
    L, D3 = p.shape
    D = D3 // 3
    tr = _conv_rows(L)

    def body(p_ref, prev_ref, w_ref, y_ref, u_scr):
        i = pl.program_id(0)
        for lo in range(0, D, BLK):
            cb, cc, cx = slice(lo, lo + BLK), slice(D + lo, D + lo + BLK), slice(2 * D + lo, 2 * D + lo + BLK)
            u_scr[0:8, :] = jnp.where(i > 0, prev_ref[:, cc] * prev_ref[:, cx], 0.0)
            for r0 in range(0, tr, BLK):
                u_scr[8 + r0:8 + r0 + BLK, :] = p_ref[r0:r0 + BLK, cc] * p_ref[r0:r0 + BLK, cx]
            for r0 in range(0, tr, BLK):
                conv = (w_ref[0:1, cb] * u_scr[6 + r0:6 + r0 + BLK, :] + w_ref[1:2, cb] * u_scr[7 + r0:7 + r0 + BLK, :]
                        + w_ref[2:3, cb] * u_scr[8 + r0:8 + r0 + BLK, :])
                y_ref[r0:r0 + BLK, cb] = (p_ref[r0:r0 + BLK, cb] * conv).astype(BF16)

    return pl.pallas_call(
        body, name=name, grid=(L // tr,),
        in_specs=[pl.BlockSpec((tr, D3), lambda i: (i, 0)),
                  pl.BlockSpec((8, D3), lambda i: (jnp.maximum(i * (tr // 8) - 1, 0), 0)),
                  pl.BlockSpec((8, D), lambda i: (0, 0))],
        out_specs=pl.BlockSpec((tr, D), lambda i: (i, 0)),
        out_shape=jax.ShapeDtypeStruct((L, D), BF16),
        scratch_shapes=[pltpu.VMEM((tr + 8, BLK), F32)],
        compiler_params=_params("parallel"),
    )(p, p, taps)


def _conv_bwd(name, p, dy, taps):
    L, D3 = p.shape
    D = D3 // 3
    tr = _conv_rows(L)
    nblk = L // tr
    last8 = L // 8 - 1

    def body(p_ref, prev_ref, next_ref, dy_ref, dyn_ref, w_ref, dp_ref, dw_ref, u_scr, d_scr):
        i = pl.program_id(0)

        @pl.when(i == 0)
        def _():
            dw_ref[...] = jnp.zeros((8, D), F32)

        for lo in range(0, D, BLK):
            cb, cc, cx = slice(lo, lo + BLK), slice(D + lo, D + lo + BLK), slice(2 * D + lo, 2 * D + lo + BLK)
            w0, w1, w2 = w_ref[0:1, cb], w_ref[1:2, cb], w_ref[2:3, cb]
            u_scr[0:8, :] = jnp.where(i > 0, prev_ref[:, cc] * prev_ref[:, cx], 0.0)
            d_scr[tr:tr + 8, :] = jnp.where(i < nblk - 1, dyn_ref[:, cb] * next_ref[:, cb], 0.0)
            for r0 in range(0, tr, BLK):
                rows = slice(r0, r0 + BLK)
                u_scr[8 + r0:8 + r0 + BLK, :] = p_ref[rows, cc] * p_ref[rows, cx]
                d_scr[rows, :] = dy_ref[rows, cb] * p_ref[rows, cb]
            dws = [None, None, None]
            for r0 in range(0, tr, BLK):
                rows = slice(r0, r0 + BLK)
                us = [u_scr[6 + r0:6 + r0 + BLK, :], u_scr[7 + r0:7 + r0 + BLK, :], u_scr[8 + r0:8 + r0 + BLK, :]]
                conv = w0 * us[0] + w1 * us[1] + w2 * us[2]
                dconv = d_scr[rows, :]
                du = w2 * dconv + w1 * d_scr[1 + r0:1 + r0 + BLK, :] + w0 * d_scr[2 + r0:2 + r0 + BLK, :]
                dp_ref[rows, cb] = (dy_ref[rows, cb] * conv).astype(BF16)
                dp_ref[rows, cc] = (du * p_ref[rows, cx]).astype(BF16)
                dp_ref[rows, cx] = (du * p_ref[rows, cc]).astype(BF16)
                for t in range(3):
                    part = jnp.sum(dconv * us[t], axis=0, keepdims=True)
                    dws[t] = part if dws[t] is None else dws[t] + part
            for t in range(3):
                dw_ref[t:t + 1, cb] += dws[t]

    halo_prev = lambda i: (jnp.maximum(i * (tr // 8) - 1, 0), 0)
    halo_next = lambda i: (jnp.minimum((i + 1) * (tr // 8), last8), 0)
    return pl.pallas_call(
        body, name=name, grid=(nblk,),
        in_specs=[pl.BlockSpec((tr, D3), lambda i: (i, 0)), pl.BlockSpec((8, D3), halo_prev),
                  pl.BlockSpec((8, D3), halo_next), pl.BlockSpec((tr, D), lambda i: (i, 0)),
                  pl.BlockSpec((8, D), halo_next), pl.BlockSpec((8, D), lambda i: (0, 0))],
        out_specs=[pl.BlockSpec((tr, D3), lambda i: (i, 0)), pl.BlockSpec((8, D), lambda i: (0, 0))],
        out_shape=[jax.ShapeDtypeStruct((L, D3), BF16), jax.ShapeDtypeStruct((8, D), F32)],
        scratch_shapes=[pltpu.VMEM((tr + 8, BLK), F32), pltpu.VMEM((tr + 8, BLK), F32)],
        compiler_params=_params("arbitrary"),
    )(p, p, p, dy, dy, taps)


KB_STEP = 3
FAR = 1 << 30


def _stack_heads(x):
    lane = lax.broadcasted_iota(jnp.int32, x.shape, 1)
    zero = jnp.zeros_like(x)
    return jnp.concatenate([jnp.where(lane < HEAD_DIM, x, zero), jnp.where(lane >= HEAD_DIM, x, zero)], axis=0)


def _unstack_heads(x2):
    lane = lax.broadcasted_iota(jnp.int32, (BLK, BLK), 1)
    return jnp.where(lane < HEAD_DIM, x2[0:BLK], x2[BLK:2 * BLK])


def _key_block(kb, u):
    kbu = kb - u
    off = pl.multiple_of(jnp.maximum(kbu, 0) * BLK, BLK)
    return off, jnp.where(kbu >= 0, kbu * BLK, FAR)


def _attn_iotas(i):
    lane = lax.broadcasted_iota(jnp.int32, (2 * BLK, BLK), 1)
    row = lax.broadcasted_iota(jnp.int32, (2 * BLK, BLK), 0)
    krow = lax.broadcasted_iota(jnp.int32, (BLK, BLK), 0)
    klane = lax.broadcasted_iota(jnp.int32, (BLK, BLK), 1)
    return lane, i * BLK + (row & (BLK - 1)), krow, klane


def _score_block(q2, kblk, kbase, qpos, lane, rr, tri):
    z = lax.dot_general(q2, kblk, NT_DIMS, preferred_element_type=F32)
    kpos = kbase + lane
    vis = (kpos < qpos) & (kpos >= PAD)
    sp = jnp.log(1.0 + jnp.exp(-jnp.abs(z)))
    lb = jnp.minimum(z, 0.0) - sp
    lom = jnp.where(vis, lb - z, 0.0)
    hi = lom.astype(BF16)
    lo = (lom - hi.astype(F32)).astype(BF16)
    cs = jnp.dot(hi, tri, preferred_element_type=F32) + jnp.dot(lo, tri, preferred_element_type=F32)
    a = jnp.where(vis, jnp.exp(lb + rr + cs), 0.0)
    return vis, lb, lom, a


def _attn_fwd(name, q, k, v, gather=None):
    L, D = q.shape
    nh, nq = D // BLK, L // BLK
    g_in = gather.operands() if gather else []
    ngi, ngo = len(g_in), (gather.n if gather else 0)

    def body(q_ref, k_ref, v_ref, *rest):
        of_ref, ob_ref, aw_ref, bw_ref, fl_ref = rest[ngi:ngi + 5]
        acc_ref, r_ref = rest[ngi + 5 + ngo:ngi + 7 + ngo]
        i = pl.program_id(1)
        if gather:
            start, forward, finish = gather.bind(rest[:ngi], rest[ngi + 5:ngi + 5 + ngo], rest[ngi + 7 + ngo:])
            grid_step = pl.program_id(0) * nq + i
            pl.when(grid_step == 0)(start)
            pl.when(grid_step == nh * nq // 2)(forward)
        lane, qpos, krow, klane = _attn_iotas(i)
        tri = (krow > klane).astype(BF16)
        q2 = _stack_heads(q_ref[...])
        acc_ref[...] = jnp.zeros((2 * BLK, BLK), F32)
        r_ref[...] = jnp.zeros((2 * BLK, 1), F32)

        def blocks(kb, save):
            rr = r_ref[...]
            upd = None
            for u in range(KB_STEP):
                off, kbase = _key_block(kb, u)
                vis, lb, lom, a = _score_block(q2, k_ref[pl.ds(off, BLK), :], kbase, qpos, lane, rr, tri)
                ab = a.astype(BF16)
                if save:
                    aw_ref[:, u * BLK:(u + 1) * BLK] = ab
                    bw_ref[:, u * BLK:(u + 1) * BLK] = jnp.where(vis, jnp.exp(lb), 0.0).astype(BF16)
                part = jnp.dot(ab, v_ref[pl.ds(off, BLK), :], preferred_element_type=F32)
                upd = part if upd is None else upd + part
                rr = rr + jnp.sum(lom, axis=1, keepdims=True)
            acc_ref[...] += upd
            r_ref[...] = rr
            return (jnp.max(rr) < LOG_ZERO).astype(jnp.int32)

        done = blocks(i, True)
        kb_end, _ = lax.while_loop(lambda c: (c[0] >= 0) & (c[1] == 0),
                                   lambda c: (c[0] - KB_STEP, blocks(c[0], False)), (i - KB_STEP, done))
        fl_ref[...] = jnp.zeros((8, BLK), F32) + (kb_end == i - KB_STEP).astype(F32)
        o = _unstack_heads(acc_ref[...])
        of_ref[...] = o
        ob_ref[...] = o.astype(BF16)
        if gather:
            pl.when(grid_step == nh * nq - 1)(finish)

    qspec = pl.BlockSpec((BLK, BLK), lambda h, i: (i, h))
    kvspec = pl.BlockSpec((L, BLK), lambda h, i: (0, h))
    wspec = pl.BlockSpec((None, None, 2 * BLK, KB_STEP * BLK), lambda h, i: (h, i, 0, 0))
    fspec = pl.BlockSpec((None, None, 8, BLK), lambda h, i: (h, i, 0, 0))
    return pl.pallas_call(
        body, name=name, grid=(nh, nq),
        in_specs=[qspec, kvspec, kvspec] + [ANY] * ngi, out_specs=[qspec, qspec, wspec, wspec, fspec] + [ANY] * ngo,
        out_shape=[jax.ShapeDtypeStruct((L, D), F32), jax.ShapeDtypeStruct((L, D), BF16),
                   jax.ShapeDtypeStruct((nh, nq, 2 * BLK, KB_STEP * BLK), BF16),
                   jax.ShapeDtypeStruct((nh, nq, 2 * BLK, KB_STEP * BLK), BF16),
                   jax.ShapeDtypeStruct((nh, nq, 8, BLK), F32)] + (gather.out_shapes() if gather else []),
        input_output_aliases=gather.aliases(3, 5) if gather else {},
        scratch_shapes=[pltpu.VMEM((2 * BLK, BLK), F32), pltpu.VMEM((2 * BLK, 1), F32)]
        + (gather.scratch() if gather else []),
        compiler_params=_params("arbitrary", "arbitrary") if gather else _params("parallel", "arbitrary"),
    )(q, k, v, *g_in)


def _attn_bwd(name, q, k, v, o, do, a_win, b_win, flag, side=None):
    L, D = q.shape
    nh, nq = D // BLK, L // BLK
    ns = side.n if side else 0

    def body(q_ref, k_ref, v_ref, o_ref, do_ref, aw_ref, bw_ref, fl_ref, *rest):
        dq_ref, dk_ref, dv_ref = rest[ns:ns + 3]
        acc_ref, r_ref, c_ref = rest[2 * ns + 3:2 * ns + 6]
        i = pl.program_id(1)
        if side:
            start, finish = side.bind(rest[:ns], rest[ns + 3:2 * ns + 3], rest[2 * ns + 6:])
            grid_step = pl.program_id(0) * nq + i
            pl.when(grid_step == 0)(start)

        @pl.when(i == 0)
        def _():
            dk_ref[...] = jnp.zeros((L, BLK), F32)
            dv_ref[...] = jnp.zeros((L, BLK), F32)

        krow = lax.broadcasted_iota(jnp.int32, (BLK, BLK), 0)
        klane = lax.broadcasted_iota(jnp.int32, (BLK, BLK), 1)
        tri_incl = (krow >= klane).astype(BF16)
        q2 = _stack_heads(q_ref[...])
        do2 = _stack_heads(do_ref[...].astype(BF16))
        total = jnp.sum(do2.astype(F32) * jnp.concatenate([o_ref[...]] * 2, axis=0), axis=1, keepdims=True)

        def grads(off, kblk, ab, beta, cc):
            da = lax.dot_general(do2, v_ref[pl.ds(off, BLK), :], NT_DIMS, preferred_element_type=F32)
            g = da * ab.astype(F32)
            ghi = g.astype(BF16)
            glo = (g - ghi.astype(F32)).astype(BF16)
            sfx = (jnp.dot(ghi, tri_incl, preferred_element_type=F32)
                   + jnp.dot(glo, tri_incl, preferred_element_type=F32))
            before = total - cc - sfx
            dz = (g * (1.0 - beta) - beta * before).astype(BF16)
            dk_ref[pl.ds(off, BLK), :] += lax.dot_general(dz, q2, TN_DIMS, preferred_element_type=F32)
            dv_ref[pl.ds(off, BLK), :] += lax.dot_general(ab, do2, TN_DIMS, preferred_element_type=F32)
            return jnp.dot(dz, kblk, preferred_element_type=F32), cc + jnp.sum(g, axis=1, keepdims=True)

        fast = jnp.max(fl_ref[...]) > 0.5

        @pl.when(fast)
        def _():
            cc = jnp.zeros((2 * BLK, 1), F32)
            upd = None
            for u in range(KB_STEP):
                off, _ = _key_block(i, u)
                cols = slice(u * BLK, (u + 1) * BLK)
                part, cc = grads(off, k_ref[pl.ds(off, BLK), :], aw_ref[:, cols], bw_ref[:, cols].astype(F32), cc)
                upd = part if upd is None else upd + part
            dq_ref[...] = (_unstack_heads(upd) * ATTN_SCALE).astype(BF16)

        @pl.when(jnp.logical_not(fast))
        def _():
            lane, qpos, _, _ = _attn_iotas(i)
            tri = (krow > klane).astype(BF16)
            acc_ref[...] = jnp.zeros((2 * BLK, BLK), F32)
            r_ref[...] = jnp.zeros((2 * BLK, 1), F32)
            c_ref[...] = jnp.zeros((2 * BLK, 1), F32)

            def step(carry):
                kb, _ = carry
                rr = r_ref[...]
                cc = c_ref[...]
                upd = None
                for u in range(KB_STEP):
                    off, kbase = _key_block(kb, u)
                    kblk = k_ref[pl.ds(off, BLK), :]
                    vis, lb, lom, a = _score_block(q2, kblk, kbase, qpos, lane, rr, tri)
                    part, cc = grads(off, kblk, a.astype(BF16), jnp.where(vis, jnp.exp(lb), 0.0), cc)
                    upd = part if upd is None else upd + part
                    rr = rr + jnp.sum(lom, axis=1, keepdims=True)
                acc_ref[...] += upd
                c_ref[...] = cc
                r_ref[...] = rr
                return kb - KB_STEP, (jnp.max(rr) < LOG_ZERO).astype(jnp.int32)

            lax.while_loop(lambda c: (c[0] >= 0) & (c[1] == 0), step, (i, jnp.int32(0)))
            dq_ref[...] = (_unstack_heads(acc_ref[...]) * ATTN_SCALE).astype(BF16)

        if side:
            pl.when(grid_step == nh * nq - 1)(finish)

    qspec = pl.BlockSpec((BLK, BLK), lambda h, i: (i, h))
    kvspec = pl.BlockSpec((L, BLK), lambda h, i: (0, h))
    wspec = pl.BlockSpec((None, None, 2 * BLK, KB_STEP * BLK), lambda h, i: (h, i, 0, 0))
    fspec = pl.BlockSpec((None, None, 8, BLK), lambda h, i: (h, i, 0, 0))
    return pl.pallas_call(
        body, name=name, grid=(nh, nq),
        in_specs=[qspec, kvspec, kvspec, qspec, qspec, wspec, wspec, fspec] + [ANY] * ns,
        out_specs=[qspec, kvspec, kvspec] + [ANY] * ns,
        out_shape=[jax.ShapeDtypeStruct((L, D), BF16), jax.ShapeDtypeStruct((L, D), F32),
                   jax.ShapeDtypeStruct((L, D), F32)] + (side.out_shapes() if side else []),
        scratch_shapes=[pltpu.VMEM((2 * BLK, BLK), F32), pltpu.VMEM((2 * BLK, 1), F32),
                        pltpu.VMEM((2 * BLK, 1), F32)] + (side.scratch() if side else []),
        compiler_params=_params("arbitrary", "arbitrary") if side else _params("parallel", "arbitrary"),
    )(q, k, v, o, do, a_win, b_win, flag, *(side.operands() if side else []))


def _loss_and_grad(name, h, target):
    L, D = h.shape
    first = (PAD + N_META) // BLK

    def body(h_ref, t_ref, sq_ref, dh_ref):
        i = pl.program_id(0)

        @pl.when(i == 0)
        def _():
            sq_ref[...] = jnp.zeros((8, BLK), F32)

        @pl.when(i < first)
        def _():
            dh_ref[...] = jnp.zeros((BLK, D), F32)

        @pl.when(i >= first)
        def _():
            err = h_ref[...] - t_ref[...]
            sq_ref[...] += jnp.sum(err * err)
            dh_ref[...] = err * (1.0 / D)

    return pl.pallas_call(
        body, name=name, grid=(L // BLK,),
        in_specs=[pl.BlockSpec((BLK, D), lambda i: (i, 0)),
                  pl.BlockSpec((BLK, D), lambda i: (jnp.maximum(i - first, 0), 0))],
        out_specs=[pl.BlockSpec((8, BLK), lambda i: (0, 0)), pl.BlockSpec((BLK, D), lambda i: (i, 0))],
        out_shape=[jax.ShapeDtypeStruct((8, BLK), F32), jax.ShapeDtypeStruct((L, D), F32)],
        compiler_params=_params("arbitrary"),
    )(h, target)


def _add_cast(name, a, b):
    L, D = a.shape
    tr = _row_tile(L, 640)

    def body(a_ref, b_ref, o_ref):
        o_ref[...] = (a_ref[...] + b_ref[...]).astype(BF16)

    row = pl.BlockSpec((tr, D), lambda i: (i, 0))
    return pl.pallas_call(body, name=name, grid=(L // tr,), in_specs=[row, row], out_specs=row,
                          out_shape=jax.ShapeDtypeStruct((L, D), BF16), compiler_params=_params("parallel"))(a, b)


def _adamw(w, g, m, v):
    m = ADAM_B1 * m + (1.0 - ADAM_B1) * g
    v = ADAM_B2 * v + (1.0 - ADAM_B2) * (g * g)
    m_hat = m / (1.0 - ADAM_B1 ** ADAM_STEP)
    v_hat = v / (1.0 - ADAM_B2 ** ADAM_STEP)
    delta = -ADAM_LR * (m_hat / (jnp.sqrt(v_hat) + ADAM_EPS) + ADAM_WD * w)
    return delta, m, v


def _adam_small(name, w, g, m, v):
    def body(w_ref, g_ref, m_ref, v_ref, d_ref, mo_ref, vo_ref):
        d, mn, vn = _adamw(w_ref[...], g_ref[...], m_ref[...], v_ref[...])
        d_ref[...] = d
        mo_ref[...] = mn
        vo_ref[...] = vn

    return pl.pallas_call(body, name=name, out_shape=[jax.ShapeDtypeStruct(w.shape, F32)] * 3)(w, g, m, v)


def _sum_slots(name, slots):
    def body(s_ref, o_ref):
        acc = s_ref[0]
        for d in range(1, N_DEV):
            acc = acc + s_ref[d]
        o_ref[...] = acc

    return pl.pallas_call(body, name=name, out_shape=jax.ShapeDtypeStruct(slots.shape[1:], F32))(slots)


def _sum8_adam(name, g, recv, order, w, m, v):
    R, C = w.shape
    tr = _row_tile(R, 512)

    def body(order_ref, g_ref, *rest):
        w_ref, m_ref, v_ref, go_ref, d_ref, mo_ref, vo_ref = rest[N_DEV - 1:]
        gsum = g_ref[...].astype(F32)
        for r_ref in rest[:N_DEV - 1]:
            gsum = gsum + r_ref[...].astype(F32)
        d, mn, vn = _adamw(w_ref[...], gsum, m_ref[...], v_ref[...])
        go_ref[...] = gsum
        d_ref[...] = d
        mo_ref[...] = mn
        vo_ref[...] = vn

    row = pl.BlockSpec((tr, C), lambda i, order: (i, 0))
    slot = lambda r: pl.BlockSpec((None, tr, C), lambda i, order: (order[r], i, 0))
    return pl.pallas_call(
        body, name=name,
        grid_spec=pltpu.PrefetchScalarGridSpec(
            num_scalar_prefetch=1, grid=(R // tr,),
            in_specs=[slot(r) for r in range(N_DEV)] + [row, row, row], out_specs=[row] * 4),
        out_shape=[jax.ShapeDtypeStruct((R, C), F32)] * 4,
        compiler_params=_params("parallel"),
    )(order, g, *([recv] * (N_DEV - 1)), w, m, v)


def _pair_sum(name, g, recv, c_idx):
    _, R, C = g.shape
    tr = _row_tile(R, 512)

    def body(sel_ref, g_ref, r_ref, o_ref):
        o_ref[...] = (g_ref[...] + r_ref[...]).astype(BF16)

    return pl.pallas_call(
        body, name=name,
        grid_spec=pltpu.PrefetchScalarGridSpec(
            num_scalar_prefetch=1, grid=(4, R // tr),
            in_specs=[pl.BlockSpec((None, tr, C), lambda k, i, sel: (2 * k + sel[0], i, 0)),
                      pl.BlockSpec((None, tr, C), lambda k, i, sel: (k, i, 0))],
            out_specs=pl.BlockSpec((None, tr, C), lambda k, i, sel: (k, i, 0))),
        out_shape=jax.ShapeDtypeStruct((4, R, C), BF16),
        compiler_params=_params("parallel", "parallel"),
    )(c_idx, g, recv)


def _final_sum_adam(name, part, recv, chip_idx, w, m, v):
    R, C = w.shape
    tr = _row_tile(R, 512)

    def body(sel_ref, p_ref, r_ref, w_ref, m_ref, v_ref, g_ref, d_ref, mo_ref, vo_ref):
        g = ((p_ref[...].astype(F32) + r_ref[0].astype(F32)) + r_ref[1].astype(F32)) + r_ref[2].astype(F32)
        d, mn, vn = _adamw(w_ref[...], g, m_ref[...], v_ref[...])
        g_ref[...] = g
        d_ref[...] = d
        mo_ref[...] = mn
        vo_ref[...] = vn

    row = pl.BlockSpec((tr, C), lambda i, sel: (i, 0))
    return pl.pallas_call(
        body, name=name,
        grid_spec=pltpu.PrefetchScalarGridSpec(
            num_scalar_prefetch=1, grid=(R // tr,),
            in_specs=[pl.BlockSpec((None, tr, C), lambda i, sel: (sel[0], i, 0)),
                      pl.BlockSpec((3, tr, C), lambda i, sel: (0, i, 0)), row, row, row],
            out_specs=[row] * 4),
        out_shape=[jax.ShapeDtypeStruct((R, C), F32)] * 4,
        compiler_params=_params("parallel"),
    )(chip_idx, part, recv, w, m, v)


def _position():
    return lax.axis_index("x"), lax.axis_index("y"), lax.axis_index("c")


class _Gather:
    def __init__(self, items):
        self.items = items
        self.n = len(items)
        self.filled = [a for a, it in enumerate(items) if it[2] is not None]

    def operands(self):
        return [it[0] for it in self.items] + [self.items[a][2] for a in self.filled]

    def out_shapes(self):
        return [jax.ShapeDtypeStruct((N_DEV,) + it[0].shape, it[0].dtype) for it in self.items]

    def aliases(self, first_in, first_out):
        return {first_in + self.n + k: first_out + a for k, a in enumerate(self.filled)}

    def scratch(self):
        return [pltpu.SemaphoreType.DMA((self.n, 7)), pltpu.SemaphoreType.DMA((self.n, 7)),
                pltpu.SemaphoreType.DMA((self.n,))]

    def bind(self, in_refs, out_refs, sems):
        send_sems, recv_sems, local_sems = sems
        n, items = self.n, self.items
        x, y, c = _position()
        me, sibling = (x, y, c), (x, y, 1 - c)
        chips = [(1 - x, y), (x, 1 - y), (1 - x, 1 - y)]
        src = [in_refs[a].at[items[a][1]] for a in range(n)]

        def copy(a, sem, block, to, own=False):
            dst = out_refs[a].at[4 * block[0] + 2 * block[1] + block[2], items[a][1]]
            return pltpu.make_async_remote_copy(
                src_ref=src[a] if own else dst, dst_ref=dst,
                send_sem=send_sems.at[a, sem], recv_sem=recv_sems.at[a, sem], device_id=to, device_id_type=MESH)

        def local(a):
            return pltpu.make_async_copy(src[a], out_refs[a].at[4 * x + 2 * y + c, items[a][1]], local_sems.at[a])

        def first():
            return [cp for a in range(n) for cp in
                    [copy(a, 0, me, sibling, own=True)]
                    + [copy(a, 1 + j, me, (*chip, c), own=True) for j, chip in enumerate(chips)]]

        def start():
            for a in range(n):
                local(a).start()
            for cp in first():
                cp.start()

        def forward():
            for j, chip in enumerate(chips):
                for a in range(n):
                    copy(a, 1 + j, (*chip, c), me).wait_recv()
                    copy(a, 4 + j, (*chip, c), sibling).start()

        def finish():
            for a in range(n):
                copy(a, 0, sibling, me).wait_recv()
            for j, chip in enumerate(chips):
                for a in range(n):
                    copy(a, 4 + j, (*chip, 1 - c), me).wait_recv()
            for cp in first():
                cp.wait_send()
            for j, chip in enumerate(chips):
                for a in range(n):
                    copy(a, 4 + j, (*chip, c), sibling).wait_send()
            for a in range(n):
                local(a).wait()

        return start, forward, finish


def _all_gather(name, gather):
    n_in = len(gather.operands())

    def body(*refs):
        start, forward, finish = gather.bind(refs[:n_in], refs[n_in:n_in + gather.n], refs[n_in + gather.n:])
        start()
        forward()
        finish()

    return pl.pallas_call(
        body, name=name, in_specs=[ANY] * n_in, out_specs=[ANY] * gather.n, out_shape=gather.out_shapes(),
        input_output_aliases=gather.aliases(0, 0), scratch_shapes=gather.scratch(),
        compiler_params=pltpu.CompilerParams(has_side_effects=True),
    )(*gather.operands())


class _Direct:
    def __init__(self, items):
        self.items = items
        self.n = len(items)

    def operands(self):
        return list(self.items)

    def out_shapes(self):
        return [jax.ShapeDtypeStruct(it.shape, it.dtype) for it in self.items]

    def scratch(self):
        return [pltpu.SemaphoreType.DMA((self.n, 7)), pltpu.SemaphoreType.DMA((self.n, 7))]

    def bind(self, in_refs, out_refs, sems):
        send_sems, recv_sems = sems
        x, y, c = _position()

        def copies():
            out = []
            for a in range(self.n):
                for r in range(1, N_DEV):
                    peer = (1 - x if r & 4 else x, 1 - y if r & 2 else y, 1 - c if r & 1 else c)
                    out.append(pltpu.make_async_remote_copy(
                        src_ref=in_refs[a].at[4 * peer[0] + 2 * peer[1] + peer[2]],
                        dst_ref=out_refs[a].at[4 * x + 2 * y + c],
                        send_sem=send_sems.at[a, r - 1], recv_sem=recv_sems.at[a, r - 1],
                        device_id=peer, device_id_type=MESH))
            return out

        def start():
            for cp in copies():
                cp.start()

        def finish():
            for cp in copies():
                cp.wait()

        return start, finish


def _exchange_sibling(name, grads):
    n = len(grads)

    def body(*refs):
        g_refs, r_refs = refs[:n], refs[n:2 * n]
        send_sems, recv_sems = refs[2 * n:]
        x, y, c = _position()
        copies = []
        for a in range(n):
            for k in range(4):
                copies.append(pltpu.make_async_remote_copy(
                    src_ref=g_refs[a].at[2 * k + (1 - c)], dst_ref=r_refs[a].at[k],
                    send_sem=send_sems.at[a, k], recv_sem=recv_sems.at[a, k],
                    device_id=(x, y, 1 - c), device_id_type=MESH))
        for cp in copies:
            cp.start()
        for cp in copies:
            cp.wait()

    return pl.pallas_call(
        body, name=name, in_specs=[ANY] * n, out_specs=[ANY] * n,
        out_shape=[jax.ShapeDtypeStruct((4,) + g.shape[1:], g.dtype) for g in grads],
        scratch_shapes=[pltpu.SemaphoreType.DMA((n, 4)), pltpu.SemaphoreType.DMA((n, 4))],
        compiler_params=pltpu.CompilerParams(has_side_effects=True),
    )(*grads)


def _exchange_chips(name, parts):
    n = len(parts)

    def body(*refs):
        p_refs, r_refs = refs[:n], refs[n:2 * n]
        send_sems, recv_sems = refs[2 * n:]
        x, y, c = _position()
        chips = [(1 - x, y), (x, 1 - y), (1 - x, 1 - y)]
        copies = []
        for a in range(n):
            for j, chip in enumerate(chips):
                copies.append(pltpu.make_async_remote_copy(
                    src_ref=p_refs[a].at[2 * chip[0] + chip[1]], dst_ref=r_refs[a].at[j],
                    send_sem=send_sems.at[a, j], recv_sem=recv_sems.at[a, j],
                    device_id=(*chip, c), device_id_type=MESH))
        for cp in copies:
            cp.start()
        for cp in copies:
            cp.wait()

    return pl.pallas_call(
        body, name=name, in_specs=[ANY] * n, out_specs=[ANY] * n,
        out_shape=[jax.ShapeDtypeStruct((3,) + p.shape[1:], p.dtype) for p in parts],
        scratch_shapes=[pltpu.SemaphoreType.DMA((n, 3)), pltpu.SemaphoreType.DMA((n, 3))],
        compiler_params=pltpu.CompilerParams(has_side_effects=True),
    )(*parts)


def kernel(x, meta_tokens, norm_gains, conv_in_proj, conv_w, conv_out_proj, kv_norm, w_k, w_v, w_q, w_o, mlp_w1, mlp_w2, loss_target, m_meta_tokens, m_norm_gains, m_conv_in_proj, m_conv_w, m_conv_out_proj, m_kv_norm, m_w_k, m_w_v, m_w_q, m_w_o, m_mlp_w1, m_mlp_w2, v_meta_tokens, v_norm_gains, v_conv_in_proj, v_conv_w, v_conv_out_proj, v_kv_norm, v_w_k, v_w_v, v_w_q, v_w_o, v_mlp_w1, v_mlp_w2):
    xi, target = x[0], loss_target[0]
    S, D = xi.shape
    L = PAD + N_META + S
    dsh = D // N_DEV
    px, py, pc = _position()
    dev = 4 * px + 2 * py + pc
    tm_big = _row_tile(L, 1664)
    tm_mid = _row_tile(L, 640)

    def pack_small(meta, gains, taps):
        return jnp.concatenate([meta, gains.reshape(DEPTH * 4, dsh), taps.reshape(N_A * 3, dsh),
                                jnp.zeros((2, dsh), F32)], axis=0)

    big_w = [conv_in_proj, conv_out_proj, w_k[None], w_v[None], w_q, w_o, mlp_w1, mlp_w2]
    small_w = pack_small(meta_tokens, norm_gains, conv_w)
    w_names = ["win", "wout", "wk", "wv", "wq", "wo", "w1", "w2"]
    wb = {nm: w.astype(BF16) for nm, w in zip(w_names, big_w)}
    gw = {nm: None for nm in w_names}

    def gather_of(parts):
        return _Gather([(wb[nm], ly, gw[nm]) for nm, ly in parts])

    def mixer_weights(ly):
        if ly < N_A:
            return [("win", ly), ("wout", ly)]
        return [("wq", ly - N_A), ("wo", ly - N_A)] + ([("wk", 0), ("wv", 0)] if ly == N_A else [])

    parts0 = mixer_weights(0) + [("w1", 0), ("w2", 0)]
    got = _all_gather("gather_layer0", _Gather([(small_w[None], 0, None)] + gather_of(parts0).items))
    small_full = got[0][:, 0].transpose(1, 0, 2).reshape(40, D)
    for (nm, _), arr in zip(parts0, got[1:]):
        gw[nm] = arr
    meta_full = small_full[0:N_META]
    gain = lambda layer, n: small_full[N_META + 4 * layer + n][None]
    taps = [jnp.concatenate([small_full[32 + 3 * l:35 + 3 * l], jnp.zeros((5, D), F32)], axis=0) for l in range(N_A)]
    kvn = kv_norm[None]

    h = jnp.concatenate([jnp.zeros((PAD, D), F32), meta_full, xi], axis=0)
    n1 = _norm_fwd("norm_in", h, gain(0, 0))
    saved = []
    hk = k = v = None
    for layer in range(DEPTH):
        s = {"h0": h, "n1": n1}
        if layer < N_A:
            s["p"], gw["w2"] = _mm_nn(f"conv_in{layer}", n1, gw["win"], layer, "col", [F32], tm_big,
                                      gather=gather_of([("w2", layer + 1)]))
            s["y"] = _conv_fwd(f"conv{layer}", s["p"], taps[layer])
            mix = _mm_nn(f"conv_out{layer}", s["y"], gw["wout"], layer, "row", [F32], tm_big, tn=D)[0]
        else:
            j = layer - N_A
            if j == 0:
                k = _mm_nn("k_proj", hk, gw["wk"], 0, "row", [BF16], tm_big, tn=D)[0]
                v = _mm_nn("v_proj", hk, gw["wv"], 0, "row", [BF16], tm_big, tn=D)[0]
            s["q"] = _mm_nn(f"q_proj{j}", n1, gw["wq"], j, "row", [BF16], tm_big, tn=D,
                            epi=lambda acc: (acc * ATTN_SCALE,))[0]
            if j == 0:
                parts = mixer_weights(layer + 1) + [("w1", layer + 1), ("w2", layer + 1)]
                got = _attn_fwd(f"attn{j}", s["q"], k, v, gather=gather_of(parts))
                for (nm, _), arr in zip(parts, got[5:]):
                    gw[nm] = arr
            else:
                got = _attn_fwd(f"attn{j}", s["q"], k, v)
            s["o"], s["ob"], s["aw"], s["bw"], s["fl"] = got[:5]
            mix = _mm_nn(f"o_proj{j}", s["ob"], gw["wo"], j, "row", [F32], tm_big, tn=D)[0]
        s["mix"] = mix
        s["h1"], s["n3"] = _res_norm(f"mix_norm{layer}", h, mix, gain(layer, 1), [gain(layer, 2)])
        relu2 = lambda acc: (jnp.square(jnp.maximum(acc, 0.0)),)
        if layer < N_A:
            parts = mixer_weights(layer + 1)
            got = _mm_nn(f"mlp_up{layer}", s["n3"], gw["w1"], layer, "col", [BF16], tm_big, epi=relu2,
                         gather=gather_of(parts))
            for (nm, _), arr in zip(parts, got[1:]):
                gw[nm] = arr
            s["act"] = got[0]
            s["ff"], gw["w1"] = _mm_nn(f"mlp_down{layer}", s["act"], gw["w2"], layer, "row", [F32], tm_big, tn=D // 2,
                                       gather=gather_of([("w1", layer + 1)]))
        else:
            s["act"] = _mm_nn(f"mlp_up{layer}", s["n3"], gw["w1"], layer, "col", [BF16], tm_big, epi=relu2)[0]
            s["ff"] = _mm_nn(f"mlp_down{layer}", s["act"], gw["w2"], layer, "row", [F32], tm_big, tn=D // 2)[0]
        pre = [] if layer == DEPTH - 1 else [gain(layer + 1, 0)] + ([kvn] if layer == N_A - 1 else [])
        outs = _res_norm(f"mlp_norm{layer}", s["h1"], s["ff"], gain(layer, 3), pre)
        h = outs[0]
        if pre:
            n1 = outs[1]
        if layer == N_A - 1:
            hk = outs[2]
        saved.append(s)

    sq, dh = _loss_and_grad("loss", h, target)
    loss = lax.psum(0.5 * sq[0, 0] / D, ("x", "y", "c"))

    g_gain = [[None] * 4 for _ in range(DEPTH)]
    g_taps = [None] * N_A
    g_w = {n: None for n in ("win", "wout", "wq", "wo", "w1a", "w1b", "w2a", "w2b")}
    flat = lambda a, lead: a.reshape(lead + (-1, a.shape[-1]))
    early_names = ["w_o", "mlp_w1b", "mlp_w2b"]
    early_g = early_recv = None
    dk_parts, dv_parts = [], []
    g_kvn = None
    s = saved[DEPTH - 1]
    dh, dff, g_gain[DEPTH - 1][3] = _norm_bwd("bwd_top", dh, post=(s["ff"], gain(DEPTH - 1, 3)))
    for layer in reversed(range(DEPTH)):
        s = saved[layer]
        def relu_grad(acc, act):
            act = act.astype(F32)
            return (acc * (2.0 * act * lax.rsqrt(jnp.maximum(act, 1e-30))),)

        da1 = _mm_nt(f"mlp_down_bwd{layer}", dff, gw["w2"], layer, "row", [BF16], tm_big, epi=relu_grad,
                     extras=(s["act"],))[0]
        half = "b" if layer >= DEPTH // 2 else "a"
        g_w["w2" + half] = _mm_tn(f"mlp_w2_grad{layer}", s["act"], dff, "row", tm_big, group=2,
                                  layer=layer % 2, layers=2, into=g_w["w2" + half], wire=half == "b")
        g_w["w1" + half] = _mm_tn(f"mlp_w1_grad{layer}", s["n3"], da1, "col", tm_big, group=4,
                                  layer=layer % 2, layers=2, into=g_w["w1" + half], wire=half == "b")
        dn3 = _mm_nt(f"mlp_up_bwd{layer}", da1, gw["w1"], layer, "col", [F32], tm_mid, tn=D)[0]
        dh, dmix, g_gain[layer][2], g_gain[layer][1] = _norm_bwd(
            f"bwd_mid{layer}", dh, stream=s["h1"], pre=[(gain(layer, 2), dn3)], post=(s["mix"], gain(layer, 1)))
        pre = []
        if layer < N_A:
            dy = _mm_nt(f"conv_out_bwd{layer}", dmix, gw["wout"], layer, "row", [F32], tm_big, group=4)[0]
            g_w["wout"] = _mm_tn(f"conv_out_grad{layer}", s["y"], dmix, "row", tm_big, group=4, layer=layer,
                                 layers=N_A, into=g_w["wout"])
            dp, g_taps[layer] = _conv_bwd(f"conv_bwd{layer}", s["p"], dy, taps[layer])
            g_w["win"] = _mm_tn(f"conv_in_grad{layer}", s["n1"], dp, "col", tm_big, group=4, layer=layer, layers=N_A,
                                into=g_w["win"])
            dn1 = _mm_nt(f"conv_in_bwd{layer}", dp, gw["win"], layer, "col", [F32], tm_mid, tn=D)[0]
        else:
            j = layer - N_A
            do = _mm_nt(f"o_proj_bwd{j}", dmix, gw["wo"], j, "row", [F32], tm_big, group=4)[0]
            g_w["wo"] = _mm_tn(f"o_proj_grad{j}", s["ob"], dmix, "row", tm_big, group=4, layer=j,
                               layers=DEPTH - N_A, into=g_w["wo"], wire=True)
            if j == 0:
                early_g = [flat(g_w[nm], (N_DEV,)) for nm in ("wo", "w1b", "w2b")]
                got = _attn_bwd(f"attn_bwd{j}", s["q"], k, v, s["o"], do, s["aw"], s["bw"], s["fl"],
                                side=_Direct(early_g))
                early_recv = got[3:]
            else:
                got = _attn_bwd(f"attn_bwd{j}", s["q"], k, v, s["o"], do, s["aw"], s["bw"], s["fl"])
            dq, dk_j, dv_j = got[:3]
            dk_parts.append(dk_j)
            dv_parts.append(dv_j)
            g_w["wq"] = _mm_tn(f"q_proj_grad{j}", s["n1"], dq, "row", tm_big, group=4, layer=j,
                               layers=DEPTH - N_A, into=g_w["wq"])
            dn1 = _mm_nt(f"q_proj_bwd{j}", dq, gw["wq"], j, "row", [F32], tm_big, group=4)[0]
            if j == 0:
                dkb = _add_cast("dk_sum", dk_parts[0], dk_parts[1])
                dvb = _add_cast("dv_sum", dv_parts[0], dv_parts[1])
                g_wk = _mm_tn("k_proj_grad", hk, dkb, "row", tm_big, group=4)
                g_wv = _mm_tn("v_proj_grad", hk, dvb, "row", tm_big, group=4)
                dhk_k = _mm_nt("k_proj_bwd", dkb, gw["wk"], 0, "row", [F32], tm_big, group=4)[0]
                dhk = _mm_nt("v_proj_bwd", dvb, gw["wv"], 0, "row", [F32], tm_big, group=4,
                             epi=lambda acc, other: (acc + other,), extras=(dhk_k,))[0]
                pre = [(kvn, dhk)]
        pre = [(gain(layer, 0), dn1)] + pre
        if layer > 0:
            sp = saved[layer - 1]
            outs = _norm_bwd(f"bwd_in{layer}", dh, stream=s["h0"], pre=pre, post=(sp["ff"], gain(layer - 1, 3)))
            dh, dff = outs[0], outs[1]
            g_gain[layer][0] = outs[2]
            if len(pre) == 2:
                g_kvn = outs[3]
            g_gain[layer - 1][3] = outs[-1]
        else:
            dh, g_gain[0][0] = _norm_bwd("bwd_in0", dh, stream=s["h0"], pre=pre)

    grad_x = dh[PAD + N_META:][None]
    g_meta = dh[PAD:PAD + N_META]

    small_g = jnp.concatenate(
        [g_meta] + [g_gain[l][n][0:1] for l in range(DEPTH) for n in range(4)]
        + [g_taps[l][0:3] for l in range(N_A)] + [g_kvn[0:1], jnp.zeros((1, D), F32)], axis=0)
    small_sum = _sum_slots("small_grad_sum", _all_gather("gather_small_grads", _Gather([(small_g[None], 0, None)]))[0][:, 0])
    small_mine = lax.dynamic_slice_in_dim(small_sum, dev * dsh, dsh, axis=1)
    small_m = pack_small(m_meta_tokens, m_norm_gains, m_conv_w)
    small_v = pack_small(v_meta_tokens, v_norm_gains, v_conv_w)
    small_d, small_mn, small_vn = _adam_small("adam_small", small_w, small_mine, small_m, small_v)
    pad8 = lambda a: jnp.concatenate([a[None], jnp.zeros((7, D), F32)], axis=0)
    g_kv = small_sum[38]
    kv_d, kv_mn, kv_vn = _adam_small("adam_kv_norm", pad8(kv_norm), pad8(g_kv), pad8(m_kv_norm), pad8(v_kv_norm))

    def unpack_small(a):
        return (a[0:N_META], a[N_META:N_META + 16].reshape(DEPTH, 4, dsh), a[32:38].reshape(N_A, 3, dsh))

    hi, lo = slice(DEPTH // 2, DEPTH), slice(0, DEPTH // 2)
    state = lambda ws, ms, vs, part: tuple(flat(t[part], ()) for t in (ws, ms, vs))
    shards = {"conv_in": (conv_in_proj, m_conv_in_proj, v_conv_in_proj), "conv_out": (conv_out_proj, m_conv_out_proj, v_conv_out_proj),
              "w_k": (w_k, m_w_k, v_w_k), "w_v": (w_v, m_w_v, v_w_v), "w_q": (w_q, m_w_q, v_w_q), "w_o": (w_o, m_w_o, v_w_o)}
    shards = {nm: state(*t, slice(None)) for nm, t in shards.items()}
    for nm, t in (("mlp_w1", (mlp_w1, m_mlp_w1, v_mlp_w1)), ("mlp_w2", (mlp_w2, m_mlp_w2, v_mlp_w2))):
        shards[nm + "a"], shards[nm + "b"] = state(*t, lo), state(*t, hi)
    done = {}
    order = jnp.stack([dev ^ r for r in range(N_DEV)]).astype(jnp.int32)
    for nm, g, recv in zip(early_names, early_g, early_recv):
        done[nm] = _sum8_adam(f"adam_{nm}", g, recv, order, *shards[nm])
    late_names = ["conv_in", "conv_out", "w_k", "w_v", "w_q", "mlp_w1a", "mlp_w2a"]
    late_g = [flat(g, (N_DEV,)) for g in (g_w["win"], g_w["wout"], g_wk, g_wv, g_w["wq"], g_w["w1a"], g_w["w2a"])]
    from_sibling = _exchange_sibling("grads_to_sibling", late_g)
    c_idx = jnp.reshape(pc, (1,)).astype(jnp.int32)
    chip_idx = jnp.reshape(2 * px + py, (1,)).astype(jnp.int32)
    parts = [_pair_sum(f"pair_sum_{nm}", g, r, c_idx) for nm, g, r in zip(late_names, late_g, from_sibling)]
    from_chips = _exchange_chips("grads_to_chips", parts)
    for nm, part, recv in zip(late_names, parts, from_chips):
        done[nm] = _final_sum_adam(f"adam_{nm}", part, recv, chip_idx, *shards[nm])

    def assemble(kind, small_parts, kv_part):
        meta_p, gains_p, taps_p = small_parts
        whole = lambda nm, like: done[nm][kind].reshape(like.shape)
        halves = lambda nm, like: jnp.concatenate([done[nm + "a"][kind], done[nm + "b"][kind]], axis=0).reshape(like.shape)
        return [meta_p, gains_p, whole("conv_in", conv_in_proj), taps_p, whole("conv_out", conv_out_proj), kv_part,
                whole("w_k", w_k), whole("w_v", w_v), whole("w_q", w_q), whole("w_o", w_o),
                halves("mlp_w1", mlp_w1), halves("mlp_w2", mlp_w2)]

    grads = assemble(0, unpack_small(small_mine), g_kv)
    deltas = assemble(1, unpack_small(small_d), kv_d[0])
    new_m = assemble(2, unpack_small(small_mn), kv_mn[0])
    new_v = assemble(3, unpack_small(small_vn), kv_vn[0])
    return (loss, grad_x, *grads, *deltas, *new_m, *new_v)
```

```python
import functools

import jax
import jax.numpy as jnp
from jax import lax
from jax.experimental import pallas as pl
from jax.experimental.pallas import tpu as pltpu

F32 = jnp.float32
BF16 = jnp.bfloat16
MESH = pl.DeviceIdType.MESH

N_DEV = 8
N_META = 16
BLK = 128
PAD = (-N_META) % BLK
HEAD_DIM = 64
DEPTH = 4
N_A = 2
RMS_EPS = 1e-6
ATTN_SCALE = HEAD_DIM ** -0.5
LOG_ZERO = -105.0
ADAM_LR, ADAM_B1, ADAM_B2, ADAM_EPS, ADAM_WD, ADAM_STEP = 0.001, 0.9, 0.999, 1e-08, 0.01, 10
VMEM_LIMIT = 56 * 2 ** 20

NT_DIMS = (((1,), (1,)), ((), ()))
TN_DIMS = (((0,), (0,)), ((), ()))
ANY = pl.BlockSpec(memory_space=pl.ANY)


def _params(*sem):
    return pltpu.CompilerParams(dimension_semantics=sem, vmem_limit_bytes=VMEM_LIMIT)


def _row_tile(rows, pref):
    best = None
    for t in range(16, min(rows, pref) + 1, 16):
        if rows % t == 0:
            best = t
    assert best is not None, (rows, pref)
    return best


def _row_chunks(rows, parts=4):
    if rows % (16 * parts):
        return [slice(0, rows)]
    step = rows // parts
    return [slice(r, r + step) for r in range(0, rows, step)]


def _mm_nn(name, a, w, layer, layout, out_dtypes, tm, tn=None, epi=None, extras=(), gather=None):
    M, K = a.shape
    if layout == "col":
        nb = w.shape[3]
        N, tn = N_DEV * nb, nb
        w_spec = pl.BlockSpec((None, None, K, nb), lambda i, j: (j, layer, 0, 0))
    else:
        kb, N = w.shape[2], w.shape[3]
        assert N_DEV * kb == K
        w_spec = pl.BlockSpec((N_DEV, None, kb, tn), lambda i, j: (0, layer, 0, j))
    ne, no = len(extras), len(out_dtypes)
    g_in = gather.operands() if gather else []
    ngi, ngo = len(g_in), (gather.n if gather else 0)
    nj = N // tn
    steps = (M // tm) * nj

    def body(a_ref, w_ref, *rest):
        outs = rest[ne + ngi:ne + ngi + no]
        if gather:
            start, forward, finish = gather.bind(rest[ne:ne + ngi], rest[ne + ngi + no:ne + ngi + no + ngo],
                                                 rest[ne + ngi + no + ngo:])
            step = pl.program_id(0) * nj + pl.program_id(1)
            pl.when(step == 0)(start)
        wv = w_ref[...]
        if layout == "row":
            wv = wv.reshape(K, tn)
        for rows in _row_chunks(tm):
            acc = jnp.dot(a_ref[rows, :], wv, preferred_element_type=F32)
            vals = epi(acc, *[r[rows, :] for r in rest[:ne]]) if epi else (acc,)
            for o_ref, val in zip(outs, vals):
                o_ref[rows, :] = val.astype(o_ref.dtype)
        if gather:
            @pl.when(step == steps - 1)
            def _():
                forward()
                finish()

    tile = pl.BlockSpec((tm, tn), lambda i, j: (i, j))
    return pl.pallas_call(
        body, name=name, grid=(M // tm, nj),
        in_specs=[pl.BlockSpec((tm, K), lambda i, j: (i, 0)), w_spec] + [tile] * ne + [ANY] * ngi,
        out_specs=[tile] * no + [ANY] * ngo,
        out_shape=[jax.ShapeDtypeStruct((M, N), d) for d in out_dtypes] + (gather.out_shapes() if gather else []),
        input_output_aliases=gather.aliases(2 + ne, no) if gather else {},
        scratch_shapes=gather.scratch() if gather else [],
        compiler_params=_params("arbitrary", "arbitrary") if gather else _params("parallel", "parallel"),
    )(a, w, *extras, *g_in)


def _mm_nt(name, a, w, layer, layout, out_dtypes, tm, tn=None, group=1, epi=None, extras=()):
    M, Nc = a.shape
    ne = len(extras)
    if layout == "row":
        kb = w.shape[2]
        No, tno = N_DEV * kb, group * kb
        w_spec = pl.BlockSpec((group, None, kb, Nc), lambda i, j: (j, layer, 0, 0))
        a_spec = pl.BlockSpec((tm, Nc), lambda i, j: (i, 0))
        grid = (M // tm, N_DEV // group)
        scratch = []

        def body(a_ref, w_ref, *rest):
            wv = w_ref[...].reshape(tno, Nc)
            for rows in _row_chunks(tm):
                acc = lax.dot_general(a_ref[rows, :], wv, NT_DIMS, preferred_element_type=F32)
                vals = epi(acc, *[r[rows, :] for r in rest[:ne]]) if epi else (acc,)
                for o_ref, val in zip(rest[ne:], vals):
                    o_ref[rows, :] = val.astype(o_ref.dtype)
    else:
        No, nb = w.shape[2], w.shape[3]
        assert N_DEV * nb == Nc
        tno = tn
        w_spec = pl.BlockSpec((N_DEV, None, tn, nb), lambda i, j: (0, layer, j, 0))
        a_spec = pl.BlockSpec((tm, Nc), lambda i, j: (i, 0))
        grid = (M // tm, No // tn)
        scratch = [pltpu.VMEM((tm, tn), F32)]

        def body(a_ref, w_ref, *rest):
            acc_ref = rest[-1]
            for d in range(N_DEV):
                part = lax.dot_general(a_ref[:, d * nb:(d + 1) * nb], w_ref[d], NT_DIMS,
                                       preferred_element_type=F32)
                if d == 0:
                    acc_ref[...] = part
                else:
                    acc_ref[...] += part
            acc = acc_ref[...]
            vals = epi(acc, *[r[...] for r in rest[:ne]]) if epi else (acc,)
            for o_ref, val in zip(rest[ne:-1], vals):
                o_ref[...] = val.astype(o_ref.dtype)

    tile = pl.BlockSpec((tm, tno), lambda i, j: (i, j))
    return pl.pallas_call(
        body, name=name, grid=grid,
        in_specs=[a_spec, w_spec] + [tile] * ne,
        out_specs=[tile] * len(out_dtypes),
        out_shape=[jax.ShapeDtypeStruct((M, No), d) for d in out_dtypes],
        scratch_shapes=scratch,
        compiler_params=_params("parallel", "parallel"),
    )(a, w, *extras)


def _mm_tn(name, a, g, layout, tl, group=1, layer=0, layers=1, into=None, wire=False):
    L, Ka = a.shape
    N = g.shape[1]
    nl = L // tl
    out_dtype = BF16 if wire else F32

    def refs(rest):
        return (rest[-2], rest[-1]) if wire else (rest[-1], rest[-1])

    def store(o_ref, acc_ref):
        if wire:
            @pl.when(pl.program_id(1) == nl - 1)
            def _():
                o_ref[...] = acc_ref[...].astype(BF16)
    if layout == "col":
        nb = N // N_DEV
        grid = (N_DEV // group, nl)
        in_specs = [pl.BlockSpec((tl, Ka), lambda j, k: (k, 0)), pl.BlockSpec((tl, group * nb), lambda j, k: (k, j))]
        out_spec = pl.BlockSpec((group, None, Ka, nb), lambda j, k: (j, layer, 0, 0))
        out_shape = jax.ShapeDtypeStruct((N_DEV, layers, Ka, nb), out_dtype)
        block = (group, Ka, nb)

        def body(a_ref, g_ref, *rest):
            o_ref, acc_ref = refs(rest)

            @pl.when(pl.program_id(1) == 0)
            def _():
                acc_ref[...] = jnp.zeros(block, F32)

            av = a_ref[...]
            for t in range(group):
                acc_ref[t] += lax.dot_general(av, g_ref[:, t * nb:(t + 1) * nb], TN_DIMS, preferred_element_type=F32)
            store(o_ref, acc_ref)
    else:
        kb = Ka // N_DEV
        grid = (N_DEV // group, nl)
        in_specs = [pl.BlockSpec((tl, group * kb), lambda j, k: (k, j)), pl.BlockSpec((tl, N), lambda j, k: (k, 0))]
        out_spec = pl.BlockSpec((group, None, kb, N), lambda j, k: (j, layer, 0, 0))
        out_shape = jax.ShapeDtypeStruct((N_DEV, layers, kb, N), out_dtype)
        block = (group, kb, N)

        def body(a_ref, g_ref, *rest):
            o_ref, acc_ref = refs(rest)

            @pl.when(pl.program_id(1) == 0)
            def _():
                acc_ref[...] = jnp.zeros(block, F32)

            gv = g_ref[...]
            if kb >= 2 * BLK:
                for t in range(group):
                    acc_ref[t] += lax.dot_general(a_ref[:, t * kb:(t + 1) * kb], gv, TN_DIMS,
                                                  preferred_element_type=F32)
            else:
                whole = lax.dot_general(a_ref[...], gv, TN_DIMS, preferred_element_type=F32)
                for t in range(group):
                    acc_ref[t] += whole[t * kb:(t + 1) * kb]
            store(o_ref, acc_ref)

    scratch = [pltpu.VMEM(block, F32)] if wire else []
    if into is None:
        return pl.pallas_call(
            body, name=name, grid=grid, in_specs=in_specs, out_specs=out_spec, out_shape=out_shape,
            scratch_shapes=scratch, compiler_params=_params("parallel", "arbitrary"),
        )(a, g)
    return pl.pallas_call(
        body, name=name, grid=grid, in_specs=in_specs + [ANY], out_specs=out_spec, out_shape=out_shape,
        scratch_shapes=scratch, input_output_aliases={2: 0}, compiler_params=_params("parallel", "arbitrary"),
    )(a, g, into)


def _rstd(x):
    return lax.rsqrt(jnp.mean(x * x, axis=-1, keepdims=True) + RMS_EPS)


def _norm_fwd(name, x, gain):
    L, D = x.shape
    tr = _row_tile(L, 640)

    def body(x_ref, g_ref, o_ref):
        xv = x_ref[...]
        o_ref[...] = (xv * _rstd(xv) * g_ref[...]).astype(BF16)

    return pl.pallas_call(
        body, name=name, grid=(L // tr,),
        in_specs=[pl.BlockSpec((tr, D), lambda i: (i, 0)), pl.BlockSpec((1, D), lambda i: (0, 0))],
        out_specs=pl.BlockSpec((tr, D), lambda i: (i, 0)),
        out_shape=jax.ShapeDtypeStruct((L, D), BF16),
        compiler_params=_params("parallel"),
    )(x, gain)


def _res_norm(name, h, branch, g_post, g_pre):
    L, D = h.shape
    tr = _row_tile(L, 640)
    npre = len(g_pre)

    def body(h_ref, b_ref, gp_ref, *rest):
        bv = b_ref[...]
        hn = h_ref[...] + bv * _rstd(bv) * gp_ref[...]
        rest[npre][...] = hn
        if npre:
            xh = hn * _rstd(hn)
            for t in range(npre):
                rest[npre + 1 + t][...] = (xh * rest[t][...]).astype(BF16)

    row = pl.BlockSpec((tr, D), lambda i: (i, 0))
    gain = pl.BlockSpec((1, D), lambda i: (0, 0))
    return pl.pallas_call(
        body, name=name, grid=(L // tr,),
        in_specs=[row, row, gain] + [gain] * npre,
        out_specs=[row] * (1 + npre),
        out_shape=[jax.ShapeDtypeStruct((L, D), F32)] + [jax.ShapeDtypeStruct((L, D), BF16)] * npre,
        compiler_params=_params("parallel"),
    )(h, branch, g_post, *g_pre)


def _norm_bwd(name, dres, stream=None, pre=(), post=None):
    L, D = dres.shape
    tr = _row_tile(L, 640)
    npre = len(pre)
    has_post = post is not None
    n_in = 1 + (1 + 2 * npre if npre else 0) + (2 if has_post else 0)
    n_dg = npre + (1 if has_post else 0)

    def body(*refs):
        ins, outs = refs[:n_in], refs[n_in:]
        i = pl.program_id(0)
        dsum = ins[0][...]
        dgs = []
        pos = 1
        if npre:
            xv = ins[pos][...]
            pos += 1
            rs = _rstd(xv)
            xh = xv * rs
            dxh = None
            for t in range(npre):
                gv, dy = ins[pos][...], ins[pos + 1][...]
                pos += 2
                dgs.append(jnp.sum(dy * xh, axis=0, keepdims=True))
                term = dy * gv
                dxh = term if dxh is None else dxh + term
            dsum = dsum + rs * (dxh - xh * jnp.mean(dxh * xh, axis=-1, keepdims=True))
        outs[0][...] = dsum
        o = 1
        if has_post:
            bv, gp = ins[pos][...], ins[pos + 1][...]
            rs = _rstd(bv)
            bh = bv * rs
            dgs.append(jnp.sum(dsum * bh, axis=0, keepdims=True))
            dbh = dsum * gp
            outs[1][...] = (rs * (dbh - bh * jnp.mean(dbh * bh, axis=-1, keepdims=True))).astype(BF16)
            o = 2

        @pl.when(i == 0)
        def _():
            for t in range(n_dg):
                outs[o + t][...] = jnp.zeros((8, D), F32)

        for t in range(n_dg):
            outs[o + t][0:1, :] += dgs[t]

    row = pl.BlockSpec((tr, D), lambda i: (i, 0))
    gain = pl.BlockSpec((1, D), lambda i: (0, 0))
    acc = pl.BlockSpec((8, D), lambda i: (0, 0))
    args, in_specs = [dres], [row]
    if npre:
        args.append(stream)
        in_specs.append(row)
        for gv, dy in pre:
            args += [gv, dy]
            in_specs += [gain, row]
    if has_post:
        args += [post[0], post[1]]
        in_specs += [row, gain]
    out_specs = [row] + ([row] if has_post else []) + [acc] * n_dg
    out_shape = ([jax.ShapeDtypeStruct((L, D), F32)] + ([jax.ShapeDtypeStruct((L, D), BF16)] if has_post else [])
                 + [jax.ShapeDtypeStruct((8, D), F32)] * n_dg)
    return pl.pallas_call(
        body, name=name, grid=(L // tr,), in_specs=in_specs, out_specs=out_specs, out_shape=out_shape,
        compiler_params=_params("arbitrary"),
    )(*args)


def _conv_rows(L):
    return max(t for t in range(BLK, 5 * BLK + 1, BLK) if L % t == 0)


def _conv_fwd(name, p, taps):
    L, D3 = p.shape
    D = D3 // 3
    tr = _conv_rows(L)

    def body(p_ref, prev_ref, w_ref, y_ref, u_scr):
        i = pl.program_id(0)
        for lo in range(0, D, BLK):
            cb, cc, cx = slice(lo, lo + BLK), slice(D + lo, D + lo + BLK), slice(2 * D + lo, 2 * D + lo + BLK)
            u_scr[0:8, :] = jnp.where(i > 0, prev_ref[:, cc] * prev_ref[:, cx], 0.0)
            for r0 in range(0, tr, BLK):
                u_scr[8 + r0:8 + r0 + BLK, :] = p_ref[r0:r0 + BLK, cc] * p_ref[r0:r0 + BLK, cx]
            for r0 in range(0, tr, BLK):
                conv = (w_ref[0:1, cb] * u_scr[6 + r0:6 + r0 + BLK, :] + w_ref[1:2, cb] * u_scr[7 + r0:7 + r0 + BLK, :]
                        + w_ref[2:3, cb] * u_scr[8 + r0:8 + r0 + BLK, :])
                y_ref[r0:r0 + BLK, cb] = (p_ref[r0:r0 + BLK, cb] * conv).astype(BF16)

    return pl.pallas_call(
        body, name=name, grid=(L // tr,),
        in_specs=[pl.BlockSpec((tr, D3), lambda i: (i, 0)),
                  pl.BlockSpec((8, D3), lambda i: (jnp.maximum(i * (tr // 8) - 1, 0), 0)),
                  pl.BlockSpec((8, D), lambda i: (0, 0))],
        out_specs=pl.BlockSpec((tr, D), lambda i: (i, 0)),
        out_shape=jax.ShapeDtypeStruct((L, D), BF16),
        scratch_shapes=[pltpu.VMEM((tr + 8, BLK), F32)],
        compiler_params=_params("parallel"),
    )(p, p, taps)


def _conv_bwd(name, p, dy, taps):
    L, D3 = p.shape
    D = D3 // 3
    tr = _conv_rows(L)
    nblk = L // tr
    last8 = L // 8 - 1

    def body(p_ref, prev_ref, next_ref, dy_ref, dyn_ref, w_ref, dp_ref, dw_ref, u_scr, d_scr):
        i = pl.program_id(0)

        @pl.when(i == 0)
        def _():
            dw_ref[...] = jnp.zeros((8, D), F32)

        for lo in range(0, D, BLK):
            cb, cc, cx = slice(lo, lo + BLK), slice(D + lo, D + lo + BLK), slice(2 * D + lo, 2 * D + lo + BLK)
            w0, w1, w2 = w_ref[0:1, cb], w_ref[1:2, cb], w_ref[2:3, cb]
            u_scr[0:8, :] = jnp.where(i > 0, prev_ref[:, cc] * prev_ref[:, cx], 0.0)
            d_scr[tr:tr + 8, :] = jnp.where(i < nblk - 1, dyn_ref[:, cb] * next_ref[:, cb], 0.0)
            for r0 in range(0, tr, BLK):
                rows = slice(r0, r0 + BLK)
                u_scr[8 + r0:8 + r0 + BLK, :] = p_ref[rows, cc] * p_ref[rows, cx]
                d_scr[rows, :] = dy_ref[rows, cb] * p_ref[rows, cb]
            dws = [None, None, None]
            for r0 in range(0, tr, BLK):
                rows = slice(r0, r0 + BLK)
                us = [u_scr[6 + r0:6 + r0 + BLK, :], u_scr[7 + r0:7 + r0 + BLK, :], u_scr[8 + r0:8 + r0 + BLK, :]]
                conv = w0 * us[0] + w1 * us[1] + w2 * us[2]
                dconv = d_scr[rows, :]
                du = w2 * dconv + w1 * d_scr[1 + r0:1 + r0 + BLK, :] + w0 * d_scr[2 + r0:2 + r0 + BLK, :]
                dp_ref[rows, cb] = (dy_ref[rows, cb] * conv).astype(BF16)
                dp_ref[rows, cc] = (du * p_ref[rows, cx]).astype(BF16)
                dp_ref[rows, cx] = (du * p_ref[rows, cc]).astype(BF16)
                for t in range(3):
                    part = jnp.sum(dconv * us[t], axis=0, keepdims=True)
                    dws[t] = part if dws[t] is None else dws[t] + part
            for t in range(3):
                dw_ref[t:t + 1, cb] += dws[t]

    halo_prev = lambda i: (jnp.maximum(i * (tr // 8) - 1, 0), 0)
    halo_next = lambda i: (jnp.minimum((i + 1) * (tr // 8), last8), 0)
    return pl.pallas_call(
        body, name=name, grid=(nblk,),
        in_specs=[pl.BlockSpec((tr, D3), lambda i: (i, 0)), pl.BlockSpec((8, D3), halo_prev),
                  pl.BlockSpec((8, D3), halo_next), pl.BlockSpec((tr, D), lambda i: (i, 0)),
                  pl.BlockSpec((8, D), halo_next), pl.BlockSpec((8, D), lambda i: (0, 0))],
        out_specs=[pl.BlockSpec((tr, D3), lambda i: (i, 0)), pl.BlockSpec((8, D), lambda i: (0, 0))],
        out_shape=[jax.ShapeDtypeStruct((L, D3), BF16), jax.ShapeDtypeStruct((8, D), F32)],
        scratch_shapes=[pltpu.VMEM((tr + 8, BLK), F32), pltpu.VMEM((tr + 8, BLK), F32)],
        compiler_params=_params("arbitrary"),
    )(p, p, p, dy, dy, taps)


KB_STEP = 3
FAR = 1 << 30


def _stack_heads(x):
    lane = lax.broadcasted_iota(jnp.int32, x.shape, 1)
    zero = jnp.zeros_like(x)
    return jnp.concatenate([jnp.where(lane < HEAD_DIM, x, zero), jnp.where(lane >= HEAD_DIM, x, zero)], axis=0)


def _unstack_heads(x2):
    lane = lax.broadcasted_iota(jnp.int32, (BLK, BLK), 1)
    return jnp.where(lane < HEAD_DIM, x2[0:BLK], x2[BLK:2 * BLK])


def _key_block(kb, u):
    kbu = kb - u
    off = pl.multiple_of(jnp.maximum(kbu, 0) * BLK, BLK)
    return off, jnp.where(kbu >= 0, kbu * BLK, FAR)


def _attn_iotas(i):
    lane = lax.broadcasted_iota(jnp.int32, (2 * BLK, BLK), 1)
    row = lax.broadcasted_iota(jnp.int32, (2 * BLK, BLK), 0)
    krow = lax.broadcasted_iota(jnp.int32, (BLK, BLK), 0)
    klane = lax.broadcasted_iota(jnp.int32, (BLK, BLK), 1)
    return lane, i * BLK + (row & (BLK - 1)), krow, klane


def _score_block(q2, kblk, kbase, qpos, lane, rr, tri):
    z = lax.dot_general(q2, kblk, NT_DIMS, preferred_element_type=F32)
    kpos = kbase + lane
    vis = (kpos < qpos) & (kpos >= PAD)
    sp = jnp.log(1.0 + jnp.exp(-jnp.abs(z)))
    lb = jnp.minimum(z, 0.0) - sp
    lom = jnp.where(vis, lb - z, 0.0)
    hi = lom.astype(BF16)
    lo = (lom - hi.astype(F32)).astype(BF16)
    cs = jnp.dot(hi, tri, preferred_element_type=F32) + jnp.dot(lo, tri, preferred_element_type=F32)
    a = jnp.where(vis, jnp.exp(lb + rr + cs), 0.0)
    return vis, lb, lom, a


def _attn_fwd(name, q, k, v, gather=None):
    L, D = q.shape
    nh, nq = D // BLK, L // BLK
    g_in = gather.operands() if gather else []
    ngi, ngo = len(g_in), (gather.n if gather else 0)

    def body(q_ref, k_ref, v_ref, *rest):
        of_ref, ob_ref, aw_ref, bw_ref, fl_ref = rest[ngi:ngi + 5]
        acc_ref, r_ref = rest[ngi + 5 + ngo:ngi + 7 + ngo]
        i = pl.program_id(1)
        if gather:
            start, forward, finish = gather.bind(rest[:ngi], rest[ngi + 5:ngi + 5 + ngo], rest[ngi + 7 + ngo:])
            grid_step = pl.program_id(0) * nq + i
            pl.when(grid_step == 0)(start)
            pl.when(grid_step == nh * nq // 2)(forward)
        lane, qpos, krow, klane = _attn_iotas(i)
        tri = (krow > klane).astype(BF16)
        q2 = _stack_heads(q_ref[...])
        acc_ref[...] = jnp.zeros((2 * BLK, BLK), F32)
        r_ref[...] = jnp.zeros((2 * BLK, 1), F32)

        def blocks(kb, save):
            rr = r_ref[...]
            upd = None
            for u in range(KB_STEP):
                off, kbase = _key_block(kb, u)
                vis, lb, lom, a = _score_block(q2, k_ref[pl.ds(off, BLK), :], kbase, qpos, lane, rr, tri)
                ab = a.astype(BF16)
                if save:
                    aw_ref[:, u * BLK:(u + 1) * BLK] = ab
                    bw_ref[:, u * BLK:(u + 1) * BLK] = jnp.where(vis, jnp.exp(lb), 0.0).astype(BF16)
                part = jnp.dot(ab, v_ref[pl.ds(off, BLK), :], preferred_element_type=F32)
                upd = part if upd is None else upd + part
                rr = rr + jnp.sum(lom, axis=1, keepdims=True)
            acc_ref[...] += upd
            r_ref[...] = rr
            return (jnp.max(rr) < LOG_ZERO).astype(jnp.int32)

        done = blocks(i, True)
        kb_end, _ = lax.while_loop(lambda c: (c[0] >= 0) & (c[1] == 0),
                                   lambda c: (c[0] - KB_STEP, blocks(c[0], False)), (i - KB_STEP, done))
        fl_ref[...] = jnp.zeros((8, BLK), F32) + (kb_end == i - KB_STEP).astype(F32)
        o = _unstack_heads(acc_ref[...])
        of_ref[...] = o
        ob_ref[...] = o.astype(BF16)
        if gather:
            pl.when(grid_step == nh * nq - 1)(finish)

    qspec = pl.BlockSpec((BLK, BLK), lambda h, i: (i, h))
    kvspec = pl.BlockSpec((L, BLK), lambda h, i: (0, h))
    wspec = pl.BlockSpec((None, None, 2 * BLK, KB_STEP * BLK), lambda h, i: (h, i, 0, 0))
    fspec = pl.BlockSpec((None, None, 8, BLK), lambda h, i: (h, i, 0, 0))
    return pl.pallas_call(
        body, name=name, grid=(nh, nq),
        in_specs=[qspec, kvspec, kvspec] + [ANY] * ngi, out_specs=[qspec, qspec, wspec, wspec, fspec] + [ANY] * ngo,
        out_shape=[jax.ShapeDtypeStruct((L, D), F32), jax.ShapeDtypeStruct((L, D), BF16),
                   jax.ShapeDtypeStruct((nh, nq, 2 * BLK, KB_STEP * BLK), BF16),
                   jax.ShapeDtypeStruct((nh, nq, 2 * BLK, KB_STEP * BLK), BF16),
                   jax.ShapeDtypeStruct((nh, nq, 8, BLK), F32)] + (gather.out_shapes() if gather else []),
        input_output_aliases=gather.aliases(3, 5) if gather else {},
        scratch_shapes=[pltpu.VMEM((2 * BLK, BLK), F32), pltpu.VMEM((2 * BLK, 1), F32)]
        + (gather.scratch() if gather else []),
        compiler_params=_params("arbitrary", "arbitrary") if gather else _params("parallel", "arbitrary"),
    )(q, k, v, *g_in)


def _attn_bwd(name, q, k, v, o, do, a_win, b_win, flag, side=None):
    L, D = q.shape
    nh, nq = D // BLK, L // BLK
    ns = side.n if side else 0

    def body(q_ref, k_ref, v_ref, o_ref, do_ref, aw_ref, bw_ref, fl_ref, *rest):
        dq_ref, dk_ref, dv_ref = rest[ns:ns + 3]
        acc_ref, r_ref, c_ref = rest[2 * ns + 3:2 * ns + 6]
        i = pl.program_id(1)
        if side:
            start, finish = side.bind(rest[:ns], rest[ns + 3:2 * ns + 3], rest[2 * ns + 6:])
            grid_step = pl.program_id(0) * nq + i
            pl.when(grid_step == 0)(start)

        @pl.when(i == 0)
        def _():
            dk_ref[...] = jnp.zeros((L, BLK), F32)
            dv_ref[...] = jnp.zeros((L, BLK), F32)

        krow = lax.broadcasted_iota(jnp.int32, (BLK, BLK), 0)
        klane = lax.broadcasted_iota(jnp.int32, (BLK, BLK), 1)
        tri_incl = (krow >= klane).astype(BF16)
        q2 = _stack_heads(q_ref[...])
        do2 = _stack_heads(do_ref[...].astype(BF16))
        total = jnp.sum(do2.astype(F32) * jnp.concatenate([o_ref[...]] * 2, axis=0), axis=1, keepdims=True)

        def grads(off, kblk, ab, beta, cc):
            da = lax.dot_general(do2, v_ref[pl.ds(off, BLK), :], NT_DIMS, preferred_element_type=F32)
            g = da * ab.astype(F32)
            ghi = g.astype(BF16)
            glo = (g - ghi.astype(F32)).astype(BF16)
            sfx = (jnp.dot(ghi, tri_incl, preferred_element_type=F32)
                   + jnp.dot(glo, tri_incl, preferred_element_type=F32))
            before = total - cc - sfx
            dz = (g * (1.0 - beta) - beta * before).astype(BF16)
            dk_ref[pl.ds(off, BLK), :] += lax.dot_general(dz, q2, TN_DIMS, preferred_element_type=F32)
            dv_ref[pl.ds(off, BLK), :] += lax.dot_general(ab, do2, TN_DIMS, preferred_element_type=F32)
            return jnp.dot(dz, kblk, preferred_element_type=F32), cc + jnp.sum(g, axis=1, keepdims=True)

        fast = jnp.max(fl_ref[...]) > 0.5

        @pl.when(fast)
        def _():
            cc = jnp.zeros((2 * BLK, 1), F32)
            upd = None
            for u in range(KB_STEP):
                off, _ = _key_block(i, u)
                cols = slice(u * BLK, (u + 1) * BLK)
                part, cc = grads(off, k_ref[pl.ds(off, BLK), :], aw_ref[:, cols], bw_ref[:, cols].astype(F32), cc)
                upd = part if upd is None else upd + part
            dq_ref[...] = (_unstack_heads(upd) * ATTN_SCALE).astype(BF16)

        @pl.when(jnp.logical_not(fast))
        def _():
            lane, qpos, _, _ = _attn_iotas(i)
            tri = (krow > klane).astype(BF16)
            acc_ref[...] = jnp.zeros((2 * BLK, BLK), F32)
            r_ref[...] = jnp.zeros((2 * BLK, 1), F32)
            c_ref[...] = jnp.zeros((2 * BLK, 1), F32)

            def step(carry):
                kb, _ = carry
                rr = r_ref[...]
                cc = c_ref[...]
                upd = None
                for u in range(KB_STEP):
                    off, kbase = _key_block(kb, u)
                    kblk = k_ref[pl.ds(off, BLK), :]
                    vis, lb, lom, a = _score_block(q2, kblk, kbase, qpos, lane, rr, tri)
                    part, cc = grads(off, kblk, a.astype(BF16), jnp.where(vis, jnp.exp(lb), 0.0), cc)
                    upd = part if upd is None else upd + part
                    rr = rr + jnp.sum(lom, axis=1, keepdims=True)
                acc_ref[...] += upd
                c_ref[...] = cc
                r_ref[...] = rr
                return kb - KB_STEP, (jnp.max(rr) < LOG_ZERO).astype(jnp.int32)

            lax.while_loop(lambda c: (c[0] >= 0) & (c[1] == 0), step, (i, jnp.int32(0)))
            dq_ref[...] = (_unstack_heads(acc_ref[...]) * ATTN_SCALE).astype(BF16)

        if side:
            pl.when(grid_step == nh * nq - 1)(finish)

    qspec = pl.BlockSpec((BLK, BLK), lambda h, i: (i, h))
    kvspec = pl.BlockSpec((L, BLK), lambda h, i: (0, h))
    wspec = pl.BlockSpec((None, None, 2 * BLK, KB_STEP * BLK), lambda h, i: (h, i, 0, 0))
    fspec = pl.BlockSpec((None, None, 8, BLK), lambda h, i: (h, i, 0, 0))
    return pl.pallas_call(
        body, name=name, grid=(nh, nq),
        in_specs=[qspec, kvspec, kvspec, qspec, qspec, wspec, wspec, fspec] + [ANY] * ns,
        out_specs=[qspec, kvspec, kvspec] + [ANY] * ns,
        out_shape=[jax.ShapeDtypeStruct((L, D), BF16), jax.ShapeDtypeStruct((L, D), F32),
                   jax.ShapeDtypeStruct((L, D), F32)] + (side.out_shapes() if side else []),
        scratch_shapes=[pltpu.VMEM((2 * BLK, BLK), F32), pltpu.VMEM((2 * BLK, 1), F32),
                        pltpu.VMEM((2 * BLK, 1), F32)] + (side.scratch() if side else []),
        compiler_params=_params("arbitrary", "arbitrary") if side else _params("parallel", "arbitrary"),
    )(q, k, v, o, do, a_win, b_win, flag, *(side.operands() if side else []))


def _loss_and_grad(name, h, target):
    L, D = h.shape
    first = (PAD + N_META) // BLK

    def body(h_ref, t_ref, sq_ref, dh_ref):
        i = pl.program_id(0)

        @pl.when(i == 0)
        def _():
            sq_ref[...] = jnp.zeros((8, BLK), F32)

        @pl.when(i < first)
        def _():
            dh_ref[...] = jnp.zeros((BLK, D), F32)

        @pl.when(i >= first)
        def _():
            err = h_ref[...] - t_ref[...]
            sq_ref[...] += jnp.sum(err * err)
            dh_ref[...] = err * (1.0 / D)

    return pl.pallas_call(
        body, name=name, grid=(L // BLK,),
        in_specs=[pl.BlockSpec((BLK, D), lambda i: (i, 0)),
                  pl.BlockSpec((BLK, D), lambda i: (jnp.maximum(i - first, 0), 0))],
        out_specs=[pl.BlockSpec((8, BLK), lambda i: (0, 0)), pl.BlockSpec((BLK, D), lambda i: (i, 0))],
        out_shape=[jax.ShapeDtypeStruct((8, BLK), F32), jax.ShapeDtypeStruct((L, D), F32)],
        compiler_params=_params("arbitrary"),
    )(h, target)


def _add_cast(name, a, b):
    L, D = a.shape
    tr = _row_tile(L, 640)

    def body(a_ref, b_ref, o_ref):
        o_ref[...] = (a_ref[...] + b_ref[...]).astype(BF16)

    row = pl.BlockSpec((tr, D), lambda i: (i, 0))
    return pl.pallas_call(body, name=name, grid=(L // tr,), in_specs=[row, row], out_specs=row,
                          out_shape=jax.ShapeDtypeStruct((L, D), BF16), compiler_params=_params("parallel"))(a, b)


def _adamw(w, g, m, v):
    m = ADAM_B1 * m + (1.0 - ADAM_B1) * g
    v = ADAM_B2 * v + (1.0 - ADAM_B2) * (g * g)
    m_hat = m / (1.0 - ADAM_B1 ** ADAM_STEP)
    v_hat = v / (1.0 - ADAM_B2 ** ADAM_STEP)
    delta = -ADAM_LR * (m_hat / (jnp.sqrt(v_hat) + ADAM_EPS) + ADAM_WD * w)
    return delta, m, v


def _adam_small(name, w, g, m, v):
    def body(w_ref, g_ref, m_ref, v_ref, d_ref, mo_ref, vo_ref):
        d, mn, vn = _adamw(w_ref[...], g_ref[...], m_ref[...], v_ref[...])
        d_ref[...] = d
        mo_ref[...] = mn
        vo_ref[...] = vn

    return pl.pallas_call(body, name=name, out_shape=[jax.ShapeDtypeStruct(w.shape, F32)] * 3)(w, g, m, v)


def _sum_slots(name, slots):
    def body(s_ref, o_ref):
        acc = s_ref[0]
        for d in range(1, N_DEV):
            acc = acc + s_ref[d]
        o_ref[...] = acc

    return pl.pallas_call(body, name=name, out_shape=jax.ShapeDtypeStruct(slots.shape[1:], F32))(slots)


def _sum8_adam(name, g, recv, order, w, m, v):
    R, C = w.shape
    tr = _row_tile(R, 512)

    def body(order_ref, g_ref, *rest):
        w_ref, m_ref, v_ref, go_ref, d_ref, mo_ref, vo_ref = rest[N_DEV - 1:]
        gsum = g_ref[...].astype(F32)
        for r_ref in rest[:N_DEV - 1]:
            gsum = gsum + r_ref[...].astype(F32)
        d, mn, vn = _adamw(w_ref[...], gsum, m_ref[...], v_ref[...])
        go_ref[...] = gsum
        d_ref[...] = d
        mo_ref[...] = mn
        vo_ref[...] = vn

    row = pl.BlockSpec((tr, C), lambda i, order: (i, 0))
    slot = lambda r: pl.BlockSpec((None, tr, C), lambda i, order: (order[r], i, 0))
    return pl.pallas_call(
        body, name=name,
        grid_spec=pltpu.PrefetchScalarGridSpec(
            num_scalar_prefetch=1, grid=(R // tr,),
            in_specs=[slot(r) for r in range(N_DEV)] + [row, row, row], out_specs=[row] * 4),
        out_shape=[jax.ShapeDtypeStruct((R, C), F32)] * 4,
        compiler_params=_params("parallel"),
    )(order, g, *([recv] * (N_DEV - 1)), w, m, v)


def _pair_sum(name, g, recv, c_idx):
    _, R, C = g.shape
    tr = _row_tile(R, 512)

    def body(sel_ref, g_ref, r_ref, o_ref):
        o_ref[...] = (g_ref[...] + r_ref[...]).astype(BF16)

    return pl.pallas_call(
        body, name=name,
        grid_spec=pltpu.PrefetchScalarGridSpec(
            num_scalar_prefetch=1, grid=(4, R // tr),
            in_specs=[pl.BlockSpec((None, tr, C), lambda k, i, sel: (2 * k + sel[0], i, 0)),
                      pl.BlockSpec((None, tr, C), lambda k, i, sel: (k, i, 0))],
            out_specs=pl.BlockSpec((None, tr, C), lambda k, i, sel: (k, i, 0))),
        out_shape=jax.ShapeDtypeStruct((4, R, C), BF16),
        compiler_params=_params("parallel", "parallel"),
    )(c_idx, g, recv)


def _final_sum_adam(name, part, recv, chip_idx, w, m, v):
    R, C = w.shape
    tr = _row_tile(R, 512)

    def body(sel_ref, p_ref, r_ref, w_ref, m_ref, v_ref, g_ref, d_ref, mo_ref, vo_ref):
        g = ((p_ref[...].astype(F32) + r_ref[0].astype(F32)) + r_ref[1].astype(F32)) + r_ref[2].astype(F32)
        d, mn, vn = _adamw(w_ref[...], g, m_ref[...], v_ref[...])
        g_ref[...] = g
        d_ref[...] = d
        mo_ref[...] = mn
        vo_ref[...] = vn

    row = pl.BlockSpec((tr, C), lambda i, sel: (i, 0))
    return pl.pallas_call(
        body, name=name,
        grid_spec=pltpu.PrefetchScalarGridSpec(
            num_scalar_prefetch=1, grid=(R // tr,),
            in_specs=[pl.BlockSpec((None, tr, C), lambda i, sel: (sel[0], i, 0)),
                      pl.BlockSpec((3, tr, C), lambda i, sel: (0, i, 0)), row, row, row],
            out_specs=[row] * 4),
        out_shape=[jax.ShapeDtypeStruct((R, C), F32)] * 4,
        compiler_params=_params("parallel"),
    )(chip_idx, part, recv, w, m, v)


def _position():
    return lax.axis_index("x"), lax.axis_index("y"), lax.axis_index("c")


class _Gather:
    def __init__(self, items):
        self.items = items
        self.n = len(items)
        self.filled = [a for a, it in enumerate(items) if it[2] is not None]

    def operands(self):
        return [it[0] for it in self.items] + [self.items[a][2] for a in self.filled]

    def out_shapes(self):
        return [jax.ShapeDtypeStruct((N_DEV,) + it[0].shape, it[0].dtype) for it in self.items]

    def aliases(self, first_in, first_out):
        return {first_in + self.n + k: first_out + a for k, a in enumerate(self.filled)}

    def scratch(self):
        return [pltpu.SemaphoreType.DMA((self.n, 7)), pltpu.SemaphoreType.DMA((self.n, 7)),
                pltpu.SemaphoreType.DMA((self.n,))]

    def bind(self, in_refs, out_refs, sems):
        send_sems, recv_sems, local_sems = sems
        n, items = self.n, self.items
        x, y, c = _position()
        me, sibling = (x, y, c), (x, y, 1 - c)
        chips = [(1 - x, y), (x, 1 - y), (1 - x, 1 - y)]
        src = [in_refs[a].at[items[a][1]] for a in range(n)]

        def copy(a, sem, block, to, own=False):
            dst = out_refs[a].at[4 * block[0] + 2 * block[1] + block[2], items[a][1]]
            return pltpu.make_async_remote_copy(
                src_ref=src[a] if own else dst, dst_ref=dst,
                send_sem=send_sems.at[a, sem], recv_sem=recv_sems.at[a, sem], device_id=to, device_id_type=MESH)

        def local(a):
            return pltpu.make_async_copy(src[a], out_refs[a].at[4 * x + 2 * y + c, items[a][1]], local_sems.at[a])

        def first():
            return [cp for a in range(n) for cp in
                    [copy(a, 0, me, sibling, own=True)]
                    + [copy(a, 1 + j, me, (*chip, c), own=True) for j, chip in enumerate(chips)]]

        def start():
            for a in range(n):
                local(a).start()
            for cp in first():
                cp.start()

        def forward():
            for j, chip in enumerate(chips):
                for a in range(n):
                    copy(a, 1 + j, (*chip, c), me).wait_recv()
                    copy(a, 4 + j, (*chip, c), sibling).start()

        def finish():
            for a in range(n):
                copy(a, 0, sibling, me).wait_recv()
            for j, chip in enumerate(chips):
                for a in range(n):
                    copy(a, 4 + j, (*chip, 1 - c), me).wait_recv()
            for cp in first():
                cp.wait_send()
            for j, chip in enumerate(chips):
                for a in range(n):
                    copy(a, 4 + j, (*chip, c), sibling).wait_send()
            for a in range(n):
                local(a).wait()

        return start, forward, finish


def _all_gather(name, gather):
    n_in = len(gather.operands())

    def body(*refs):
        start, forward, finish = gather.bind(refs[:n_in], refs[n_in:n_in + gather.n], refs[n_in + gather.n:])
        start()
        forward()
        finish()

    return pl.pallas_call(
        body, name=name, in_specs=[ANY] * n_in, out_specs=[ANY] * gather.n, out_shape=gather.out_shapes(),
        input_output_aliases=gather.aliases(0, 0), scratch_shapes=gather.scratch(),
        compiler_params=pltpu.CompilerParams(has_side_effects=True),
    )(*gather.operands())


class _Direct:
    def __init__(self, items):
        self.items = items
        self.n = len(items)

    def operands(self):
        return list(self.items)

    def out_shapes(self):
        return [jax.ShapeDtypeStruct(it.shape, it.dtype) for it in self.items]

    def scratch(self):
        return [pltpu.SemaphoreType.DMA((self.n, 7)), pltpu.SemaphoreType.DMA((self.n, 7))]

    def bind(self, in_refs, out_refs, sems):
        send_sems, recv_sems = sems
        x, y, c = _position()

        def copies():
            out = []
            for a in range(self.n):
                for r in range(1, N_DEV):
                    peer = (1 - x if r & 4 else x, 1 - y if r & 2 else y, 1 - c if r & 1 else c)
                    out.append(pltpu.make_async_remote_copy(
                        src_ref=in_refs[a].at[4 * peer[0] + 2 * peer[1] + peer[2]],
                        dst_ref=out_refs[a].at[4 * x + 2 * y + c],
                        send_sem=send_sems.at[a, r - 1], recv_sem=recv_sems.at[a, r - 1],
                        device_id=peer, device_id_type=MESH))
            return out

        def start():
            for cp in copies():
                cp.start()

        def finish():
            for cp in copies():
                cp.wait()

        return start, finish


def _exchange_sibling(name, grads):
    n = len(grads)

    def body(*refs):
        g_refs, r_refs = refs[:n], refs[n:2 * n]
        send_sems, recv_sems = refs[2 * n:]
        x, y, c = _position()
        copies = []
        for a in range(n):
            for k in range(4):
                copies.append(pltpu.make_async_remote_copy(
                    src_ref=g_refs[a].at[2 * k + (1 - c)], dst_ref=r_refs[a].at[k],
                    send_sem=send_sems.at[a, k], recv_sem=recv_sems.at[a, k],
                    device_id=(x, y, 1 - c), device_id_type=MESH))
        for cp in copies:
            cp.start()
        for cp in copies:
            cp.wait()

    return pl.pallas_call(
        body, name=name, in_specs=[ANY] * n, out_specs=[ANY] * n,
        out_shape=[jax.ShapeDtypeStruct((4,) + g.shape[1:], g.dtype) for g in grads],
        scratch_shapes=[pltpu.SemaphoreType.DMA((n, 4)), pltpu.SemaphoreType.DMA((n, 4))],
        compiler_params=pltpu.CompilerParams(has_side_effects=True),
    )(*grads)


def _exchange_chips(name, parts):
    n = len(parts)

    def body(*refs):
        p_refs, r_refs = refs[:n], refs[n:2 * n]
        send_sems, recv_sems = refs[2 * n:]
        x, y, c = _position()
        chips = [(1 - x, y), (x, 1 - y), (1 - x, 1 - y)]
        copies = []
        for a in range(n):
            for j, chip in enumerate(chips):
                copies.append(pltpu.make_async_remote_copy(
                    src_ref=p_refs[a].at[2 * chip[0] + chip[1]], dst_ref=r_refs[a].at[j],
                    send_sem=send_sems.at[a, j], recv_sem=recv_sems.at[a, j],
                    device_id=(*chip, c), device_id_type=MESH))
        for cp in copies:
            cp.start()
        for cp in copies:
            cp.wait()

    return pl.pallas_call(
        body, name=name, in_specs=[ANY] * n, out_specs=[ANY] * n,
        out_shape=[jax.ShapeDtypeStruct((3,) + p.shape[1:], p.dtype) for p in parts],
        scratch_shapes=[pltpu.SemaphoreType.DMA((n, 3)), pltpu.SemaphoreType.DMA((n, 3))],
        compiler_params=pltpu.CompilerParams(has_side_effects=True),
    )(*parts)


def kernel(x, meta_tokens, norm_gains, conv_in_proj, conv_w, conv_out_proj, kv_norm, w_k, w_v, w_q, w_o, mlp_w1, mlp_w2, loss_target, m_meta_tokens, m_norm_gains, m_conv_in_proj, m_conv_w, m_conv_out_proj, m_kv_norm, m_w_k, m_w_v, m_w_q, m_w_o, m_mlp_w1, m_mlp_w2, v_meta_tokens, v_norm_gains, v_conv_in_proj, v_conv_w, v_conv_out_proj, v_kv_norm, v_w_k, v_w_v, v_w_q, v_w_o, v_mlp_w1, v_mlp_w2):
    xi, target = x[0], loss_target[0]
    S, D = xi.shape
    L = PAD + N_META + S
    dsh = D // N_DEV
    px, py, pc = _position()
    dev = 4 * px + 2 * py + pc
    tm_big = _row_tile(L, 1664)
    tm_mid = _row_tile(L, 640)

    def pack_small(meta, gains, taps):
        return jnp.concatenate([meta, gains.reshape(DEPTH * 4, dsh), taps.reshape(N_A * 3, dsh),
                                jnp.zeros((2, dsh), F32)], axis=0)

    big_w = [conv_in_proj, conv_out_proj, w_k[None], w_v[None], w_q, w_o, mlp_w1, mlp_w2]
    small_w = pack_small(meta_tokens, norm_gains, conv_w)
    w_names = ["win", "wout", "wk", "wv", "wq", "wo", "w1", "w2"]
    wb = {nm: w.astype(BF16) for nm, w in zip(w_names, big_w)}
    gw = {nm: None for nm in w_names}
    rides = {"conv_in0": [("wout", 0), ("w1", 0)], "mlp_up0": [("w2", 0), ("win", 1)],
             "mlp_down0": [("w1", 1), ("wout", 1)], "conv_in1": [("w2", 1)],
             "mlp_up1": [("wq", 0), ("wk", 0), ("wv", 0), ("w2", 2)], "mlp_down1": [("w1", 2), ("wo", 0)],
             "attn0": [("wq", 1), ("wo", 1), ("w1", 3), ("w2", 3)]}

    def gather_of(parts):
        return _Gather([(wb[nm], ly, gw[nm]) for nm, ly in parts]) if parts else None

    def hosting(call, name, *args, **kw):
        parts = rides.get(name, [])
        got = call(name, *args, gather=gather_of(parts), **kw)
        for (nm, _), arr in zip(parts, got[len(got) - len(parts):]):
            gw[nm] = arr
        return got[:len(got) - len(parts)]

    parts0 = [("win", 0)]
    got = _all_gather("gather_first", _Gather([(small_w[None], 0, None)] + gather_of(parts0).items))
    small_full = got[0][:, 0].transpose(1, 0, 2).reshape(40, D)
    for (nm, _), arr in zip(parts0, got[1:]):
        gw[nm] = arr
    meta_full = small_full[0:N_META]
    gain = lambda layer, n: small_full[N_META + 4 * layer + n][None]
    taps = [jnp.concatenate([small_full[32 + 3 * l:35 + 3 * l], jnp.zeros((5, D), F32)], axis=0) for l in range(N_A)]
    kvn = kv_norm[None]

    h = jnp.concatenate([jnp.zeros((PAD, D), F32), meta_full, xi], axis=0)
    n1 = _norm_fwd("norm_in", h, gain(0, 0))
    saved = []
    hk = k = v = None
    for layer in range(DEPTH):
        s = {"h0": h, "n1": n1}
        if layer < N_A:
            s["p"] = hosting(_mm_nn, f"conv_in{layer}", n1, gw["win"], layer, "col", [F32], tm_big)[0]
            s["y"] = _conv_fwd(f"conv{layer}", s["p"], taps[layer])
            mix = _mm_nn(f"conv_out{layer}", s["y"], gw["wout"], layer, "row", [F32], tm_big, tn=D)[0]
        else:
            j = layer - N_A
            if j == 0:
                k = _mm_nn("k_proj", hk, gw["wk"], 0, "row", [BF16], tm_big, tn=D)[0]
                v = _mm_nn("v_proj", hk, gw["wv"], 0, "row", [BF16], tm_big, tn=D)[0]
            s["q"] = _mm_nn(f"q_proj{j}", n1, gw["wq"], j, "row", [BF16], tm_big, tn=D,
                            epi=lambda acc: (acc * ATTN_SCALE,))[0]
            s["o"], s["ob"], s["aw"], s["bw"], s["fl"] = hosting(_attn_fwd, f"attn{j}", s["q"], k, v)
            mix = _mm_nn(f"o_proj{j}", s["ob"], gw["wo"], j, "row", [F32], tm_big, tn=D)[0]
        s["mix"] = mix
        s["h1"], s["n3"] = _res_norm(f"mix_norm{layer}", h, mix, gain(layer, 1), [gain(layer, 2)])
        relu2 = lambda acc: (jnp.square(jnp.maximum(acc, 0.0)),)
        s["act"] = hosting(_mm_nn, f"mlp_up{layer}", s["n3"], gw["w1"], layer, "col", [BF16], tm_big, epi=relu2)[0]
        s["ff"] = hosting(_mm_nn, f"mlp_down{layer}", s["act"], gw["w2"], layer, "row", [F32], tm_big, tn=D // 2)[0]
        pre = [] if layer == DEPTH - 1 else [gain(layer + 1, 0)] + ([kvn] if layer == N_A - 1 else [])
        outs = _res_norm(f"mlp_norm{layer}", s["h1"], s["ff"], gain(layer, 3), pre)
        h = outs[0]
        if pre:
            n1 = outs[1]
        if layer == N_A - 1:
            hk = outs[2]
        saved.append(s)

    sq, dh = _loss_and_grad("loss", h, target)
    loss = lax.psum(0.5 * sq[0, 0] / D, ("x", "y", "c"))

    g_gain = [[None] * 4 for _ in range(DEPTH)]
    g_taps = [None] * N_A
    g_w = {n: None for n in ("win", "wout", "wq", "wo", "w1a", "w1b", "w2a", "w2b")}
    flat = lambda a, lead: a.reshape(lead + (-1, a.shape[-1]))
    early_names = ["w_o", "mlp_w1b", "mlp_w2b"]
    early_g = early_recv = None
    dk_parts, dv_parts = [], []
    g_kvn = None
    s = saved[DEPTH - 1]
    dh, dff, g_gain[DEPTH - 1][3] = _norm_bwd("bwd_top", dh, post=(s["ff"], gain(DEPTH - 1, 3)))
    for layer in reversed(range(DEPTH)):
        s = saved[layer]
        def relu_grad(acc, act):
            act = act.astype(F32)
            return (acc * (2.0 * act * lax.rsqrt(jnp.maximum(act, 1e-30))),)

        da1 = _mm_nt(f"mlp_down_bwd{layer}", dff, gw["w2"], layer, "row", [BF16], tm_big, epi=relu_grad,
                     extras=(s["act"],))[0]
        half = "b" if layer >= DEPTH // 2 else "a"
        g_w["w2" + half] = _mm_tn(f"mlp_w2_grad{layer}", s["act"], dff, "row", tm_big, group=2,
                                  layer=layer % 2, layers=2, into=g_w["w2" + half], wire=half == "b")
        g_w["w1" + half] = _mm_tn(f"mlp_w1_grad{layer}", s["n3"], da1, "col", tm_big, group=4,
                                  layer=layer % 2, layers=2, into=g_w["w1" + half], wire=half == "b")
        dn3 = _mm_nt(f"mlp_up_bwd{layer}", da1, gw["w1"], layer, "col", [F32], tm_mid, tn=D)[0]
        dh, dmix, g_gain[layer][2], g_gain[layer][1] = _norm_bwd(
            f"bwd_mid{layer}", dh, stream=s["h1"], pre=[(gain(layer, 2), dn3)], post=(s["mix"], gain(layer, 1)))
        pre = []
        if layer < N_A:
            dy = _mm_nt(f"conv_out_bwd{layer}", dmix, gw["wout"], layer, "row", [F32], tm_big, group=4)[0]
            g_w["wout"] = _mm_tn(f"conv_out_grad{layer}", s["y"], dmix, "row", tm_big, group=4, layer=layer,
                                 layers=N_A, into=g_w["wout"])
            dp, g_taps[layer] = _conv_bwd(f"conv_bwd{layer}", s["p"], dy, taps[layer])
            g_w["win"] = _mm_tn(f"conv_in_grad{layer}", s["n1"], dp, "col", tm_big, group=4, layer=layer, layers=N_A,
                                into=g_w["win"])
            dn1 = _mm_nt(f"conv_in_bwd{layer}", dp, gw["win"], layer, "col", [F32], tm_mid, tn=D)[0]
        else:
            j = layer - N_A
            do = _mm_nt(f"o_proj_bwd{j}", dmix, gw["wo"], j, "row", [BF16], tm_big, group=4)[0]
            g_w["wo"] = _mm_tn(f"o_proj_grad{j}", s["ob"], dmix, "row", tm_big, group=4, layer=j,
                               layers=DEPTH - N_A, into=g_w["wo"], wire=True)
            if j == 0:
                early_g = [flat(g_w[nm], (N_DEV,)) for nm in ("wo", "w1b", "w2b")]
                got = _attn_bwd(f"attn_bwd{j}", s["q"], k, v, s["o"], do, s["aw"], s["bw"], s["fl"],
                                side=_Direct(early_g))
                early_recv = got[3:]
            else:
                got = _attn_bwd(f"attn_bwd{j}", s["q"], k, v, s["o"], do, s["aw"], s["bw"], s["fl"])
            dq, dk_j, dv_j = got[:3]
            dk_parts.append(dk_j)
            dv_parts.append(dv_j)
            g_w["wq"] = _mm_tn(f"q_proj_grad{j}", s["n1"], dq, "row", tm_big, group=4, layer=j,
                               layers=DEPTH - N_A, into=g_w["wq"])
            dn1 = _mm_nt(f"q_proj_bwd{j}", dq, gw["wq"], j, "row", [F32], tm_big, group=4)[0]
            if j == 0:
                dkb = _add_cast("dk_sum", dk_parts[0], dk_parts[1])
                dvb = _add_cast("dv_sum", dv_parts[0], dv_parts[1])
                g_wk = _mm_tn("k_proj_grad", hk, dkb, "row", tm_big, group=4)
                g_wv = _mm_tn("v_proj_grad", hk, dvb, "row", tm_big, group=4)
                dhk_k = _mm_nt("k_proj_bwd", dkb, gw["wk"], 0, "row", [F32], tm_big, group=4)[0]
                dhk = _mm_nt("v_proj_bwd", dvb, gw["wv"], 0, "row", [F32], tm_big, group=4,
                             epi=lambda acc, other: (acc + other,), extras=(dhk_k,))[0]
                pre = [(kvn, dhk)]
        pre = [(gain(layer, 0), dn1)] + pre
        if layer > 0:
            sp = saved[layer - 1]
            outs = _norm_bwd(f"bwd_in{layer}", dh, stream=s["h0"], pre=pre, post=(sp["ff"], gain(layer - 1, 3)))
            dh, dff = outs[0], outs[1]
            g_gain[layer][0] = outs[2]
            if len(pre) == 2:
                g_kvn = outs[3]
            g_gain[layer - 1][3] = outs[-1]
        else:
            dh, g_gain[0][0] = _norm_bwd("bwd_in0", dh, stream=s["h0"], pre=pre)

    grad_x = dh[PAD + N_META:][None]
    g_meta = dh[PAD:PAD + N_META]

    small_g = jnp.concatenate(
        [g_meta] + [g_gain[l][n][0:1] for l in range(DEPTH) for n in range(4)]
        + [g_taps[l][0:3] for l in range(N_A)] + [g_kvn[0:1], jnp.zeros((1, D), F32)], axis=0)
    small_sum = _sum_slots("small_grad_sum", _all_gather("gather_small_grads", _Gather([(small_g[None], 0, None)]))[0][:, 0])
    small_mine = lax.dynamic_slice_in_dim(small_sum, dev * dsh, dsh, axis=1)
    small_m = pack_small(m_meta_tokens, m_norm_gains, m_conv_w)
    small_v = pack_small(v_meta_tokens, v_norm_gains, v_conv_w)
    small_d, small_mn, small_vn = _adam_small("adam_small", small_w, small_mine, small_m, small_v)
    pad8 = lambda a: jnp.concatenate([a[None], jnp.zeros((7, D), F32)], axis=0)
    g_kv = small_sum[38]
    kv_d, kv_mn, kv_vn = _adam_small("adam_kv_norm", pad8(kv_norm), pad8(g_kv), pad8(m_kv_norm), pad8(v_kv_norm))

    def unpack_small(a):
        return (a[0:N_META], a[N_META:N_META + 16].reshape(DEPTH, 4, dsh), a[32:38].reshape(N_A, 3, dsh))

    hi, lo = slice(DEPTH // 2, DEPTH), slice(0, DEPTH // 2)
    state = lambda ws, ms, vs, part: tuple(flat(t[part], ()) for t in (ws, ms, vs))
    shards = {"conv_in": (conv_in_proj, m_conv_in_proj, v_conv_in_proj), "conv_out": (conv_out_proj, m_conv_out_proj, v_conv_out_proj),
              "w_k": (w_k, m_w_k, v_w_k), "w_v": (w_v, m_w_v, v_w_v), "w_q": (w_q, m_w_q, v_w_q), "w_o": (w_o, m_w_o, v_w_o)}
    shards = {nm: state(*t, slice(None)) for nm, t in shards.items()}
    for nm, t in (("mlp_w1", (mlp_w1, m_mlp_w1, v_mlp_w1)), ("mlp_w2", (mlp_w2, m_mlp_w2, v_mlp_w2))):
        shards[nm + "a"], shards[nm + "b"] = state(*t, lo), state(*t, hi)
    done = {}
    order = jnp.stack([dev ^ r for r in range(N_DEV)]).astype(jnp.int32)
    for nm, g, recv in zip(early_names, early_g, early_recv):
        done[nm] = _sum8_adam(f"adam_{nm}", g, recv, order, *shards[nm])
    late_names = ["conv_in", "conv_out", "w_k", "w_v", "w_q", "mlp_w1a", "mlp_w2a"]
    late_g = [flat(g, (N_DEV,)) for g in (g_w["win"], g_w["wout"], g_wk, g_wv, g_w["wq"], g_w["w1a"], g_w["w2a"])]
    from_sibling = _exchange_sibling("grads_to_sibling", late_g)
    c_idx = jnp.reshape(pc, (1,)).astype(jnp.int32)
    chip_idx = jnp.reshape(2 * px + py, (1,)).astype(jnp.int32)
    parts = [_pair_sum(f"pair_sum_{nm}", g, r, c_idx) for nm, g, r in zip(late_names, late_g, from_sibling)]
    from_chips = _exchange_chips("grads_to_chips", parts)
    for nm, part, recv in zip(late_names, parts, from_chips):
        done[nm] = _final_sum_adam(f"adam_{nm}", part, recv, chip_idx, *shards[nm])

    def assemble(kind, small_parts, kv_part):
        meta_p, gains_p, taps_p = small_parts
        whole = lambda nm, like: done[nm][kind].reshape(like.shape)
        halves = lambda nm, like: jnp.concatenate([done[nm + "a"][kind], done[nm + "b"][kind]], axis=0).reshape(like.shape)
        return [meta_p, gains_p, whole("conv_in", conv_in_proj), taps_p, whole("conv_out", conv_out_proj), kv_part,
                whole("w_k", w_k), whole("w_v", w_v), whole("w_q", w_q), whole("w_o", w_o),
                halves("mlp_w1", mlp_w1), halves("mlp_w2", mlp_w2)]

    grads = assemble(0, unpack_small(small_mine), g_kv)
    deltas = assemble(1, unpack_small(small_d), kv_d[0])
    new_m = assemble(2, unpack_small(small_mn), kv_mn[0])
    new_v = assemble(3, unpack_small(small_vn), kv_vn[0])
    return (loss, grad_x, *grads, *deltas, *new_m, *new_v)
```

```python
import functools

import jax
import jax.numpy as jnp
from jax import lax
from jax.experimental import pallas as pl
from jax.experimental.pallas import tpu as pltpu

F32 = jnp.float32
BF16 = jnp.bfloat16
MESH = pl.DeviceIdType.MESH

N_DEV = 8
N_META = 16
BLK = 128
PAD = (-N_META) % BLK
HEAD_DIM = 64
DEPTH = 4
N_A = 2
RMS_EPS = 1e-6
ATTN_SCALE = HEAD_DIM ** -0.5
LOG_ZERO = -105.0
ADAM_LR, ADAM_B1, ADAM_B2, ADAM_EPS, ADAM_WD, ADAM_STEP = 0.001, 0.9, 0.999, 1e-08, 0.01, 10
VMEM_LIMIT = 56 * 2 ** 20

NT_DIMS = (((1,), (1,)), ((), ()))
TN_DIMS = (((0,), (0,)), ((), ()))
ANY = pl.BlockSpec(memory_space=pl.ANY)


def _params(*sem):
    return pltpu.CompilerParams(dimension_semantics=sem, vmem_limit_bytes=VMEM_LIMIT)


def _row_tile(rows, pref):
    best = None
    for t in range(16, min(rows, pref) + 1, 16):
        if rows % t == 0:
            best = t
    assert best is not None, (rows, pref)
    return best


def _row_chunks(rows, parts=4):
    if rows % (16 * parts):
        return [slice(0, rows)]
    step = rows // parts
    return [slice(r, r + step) for r in range(0, rows, step)]


def _mm_nn(name, a, w, layer, layout, out_dtypes, tm, tn=None, epi=None, extras=(), gather=None):
    M, K = a.shape
    if layout == "col":
        nb = w.shape[3]
        N, tn = N_DEV * nb, nb
        w_spec = pl.BlockSpec((None, None, K, nb), lambda i, j: (j, layer, 0, 0))
    else:
        kb, N = w.shape[2], w.shape[3]
        assert N_DEV * kb == K
        w_spec = pl.BlockSpec((N_DEV, None, kb, tn), lambda i, j: (0, layer, 0, j))
    ne, no = len(extras), len(out_dtypes)
    g_in = gather.operands() if gather else []
    ngi, ngo = len(g_in), (gather.n if gather else 0)
    nj = N // tn
    steps = (M // tm) * nj

    def body(a_ref, w_ref, *rest):
        outs = rest[ne + ngi:ne + ngi + no]
        if gather:
            start, forward, finish = gather.bind(rest[ne:ne + ngi], rest[ne + ngi + no:ne + ngi + no + ngo],
                                                 rest[ne + ngi + no + ngo:])
            step = pl.program_id(0) * nj + pl.program_id(1)
            pl.when(step == 0)(start)
        wv = w_ref[...]
        if layout == "row":
            wv = wv.reshape(K, tn)
        for rows in _row_chunks(tm):
            acc = jnp.dot(a_ref[rows, :], wv, preferred_element_type=F32)
            vals = epi(acc, *[r[rows, :] for r in rest[:ne]]) if epi else (acc,)
            for o_ref, val in zip(outs, vals):
                o_ref[rows, :] = val.astype(o_ref.dtype)
        if gather:
            @pl.when(step == steps - 1)
            def _():
                forward()
                finish()

    tile = pl.BlockSpec((tm, tn), lambda i, j: (i, j))
    return pl.pallas_call(
        body, name=name, grid=(M // tm, nj),
        in_specs=[pl.BlockSpec((tm, K), lambda i, j: (i, 0)), w_spec] + [tile] * ne + [ANY] * ngi,
        out_specs=[tile] * no + [ANY] * ngo,
        out_shape=[jax.ShapeDtypeStruct((M, N), d) for d in out_dtypes] + (gather.out_shapes() if gather else []),
        input_output_aliases=gather.aliases(2 + ne, no) if gather else {},
        scratch_shapes=gather.scratch() if gather else [],
        compiler_params=_params("arbitrary", "arbitrary") if gather else _params("parallel", "parallel"),
    )(a, w, *extras, *g_in)


def _mm_nt(name, a, w, layer, layout, out_dtypes, tm, tn=None, group=1, epi=None, extras=()):
    M, Nc = a.shape
    ne = len(extras)
    if layout == "row":
        kb = w.shape[2]
        No, tno = N_DEV * kb, group * kb
        w_spec = pl.BlockSpec((group, None, kb, Nc), lambda i, j: (j, layer, 0, 0))
        a_spec = pl.BlockSpec((tm, Nc), lambda i, j: (i, 0))
        grid = (M // tm, N_DEV // group)
        scratch = []

        def body(a_ref, w_ref, *rest):
            wv = w_ref[...].reshape(tno, Nc)
            for rows in _row_chunks(tm):
                acc = lax.dot_general(a_ref[rows, :], wv, NT_DIMS, preferred_element_type=F32)
                vals = epi(acc, *[r[rows, :] for r in rest[:ne]]) if epi else (acc,)
                for o_ref, val in zip(rest[ne:], vals):
                    o_ref[rows, :] = val.astype(o_ref.dtype)
    else:
        No, nb = w.shape[2], w.shape[3]
        assert N_DEV * nb == Nc
        tno = tn
        w_spec = pl.BlockSpec((N_DEV, None, tn, nb), lambda i, j: (0, layer, j, 0))
        a_spec = pl.BlockSpec((tm, Nc), lambda i, j: (i, 0))
        grid = (M // tm, No // tn)
        scratch = [pltpu.VMEM((tm, tn), F32)]

        def body(a_ref, w_ref, *rest):
            acc_ref = rest[-1]
            for d in range(N_DEV):
                part = lax.dot_general(a_ref[:, d * nb:(d + 1) * nb], w_ref[d], NT_DIMS,
                                       preferred_element_type=F32)
                if d == 0:
                    acc_ref[...] = part
                else:
                    acc_ref[...] += part
            acc = acc_ref[...]
            vals = epi(acc, *[r[...] for r in rest[:ne]]) if epi else (acc,)
            for o_ref, val in zip(rest[ne:-1], vals):
                o_ref[...] = val.astype(o_ref.dtype)

    tile = pl.BlockSpec((tm, tno), lambda i, j: (i, j))
    return pl.pallas_call(
        body, name=name, grid=grid,
        in_specs=[a_spec, w_spec] + [tile] * ne,
        out_specs=[tile] * len(out_dtypes),
        out_shape=[jax.ShapeDtypeStruct((M, No), d) for d in out_dtypes],
        scratch_shapes=scratch,
        compiler_params=_params("parallel", "parallel"),
    )(a, w, *extras)


def _mm_tn(name, a, g, layout, tl, group=1, layer=0, layers=1, into=None, wire=False):
    L, Ka = a.shape
    N = g.shape[1]
    nl = L // tl
    out_dtype = BF16 if wire else F32

    def refs(rest):
        return (rest[-2], rest[-1]) if wire else (rest[-1], rest[-1])

    def store(o_ref, acc_ref):
        if wire:
            @pl.when(pl.program_id(1) == nl - 1)
            def _():
                o_ref[...] = acc_ref[...].astype(BF16)
    if layout == "col":
        nb = N // N_DEV
        grid = (N_DEV // group, nl)
        in_specs = [pl.BlockSpec((tl, Ka), lambda j, k: (k, 0)), pl.BlockSpec((tl, group * nb), lambda j, k: (k, j))]
        out_spec = pl.BlockSpec((group, None, Ka, nb), lambda j, k: (j, layer, 0, 0))
        out_shape = jax.ShapeDtypeStruct((N_DEV, layers, Ka, nb), out_dtype)
        block = (group, Ka, nb)

        def body(a_ref, g_ref, *rest):
            o_ref, acc_ref = refs(rest)

            @pl.when(pl.program_id(1) == 0)
            def _():
                acc_ref[...] = jnp.zeros(block, F32)

            av = a_ref[...]
            for t in range(group):
                acc_ref[t] += lax.dot_general(av, g_ref[:, t * nb:(t + 1) * nb], TN_DIMS, preferred_element_type=F32)
            store(o_ref, acc_ref)
    else:
        kb = Ka // N_DEV
        grid = (N_DEV // group, nl)
        in_specs = [pl.BlockSpec((tl, group * kb), lambda j, k: (k, j)), pl.BlockSpec((tl, N), lambda j, k: (k, 0))]
        out_spec = pl.BlockSpec((group, None, kb, N), lambda j, k: (j, layer, 0, 0))
        out_shape = jax.ShapeDtypeStruct((N_DEV, layers, kb, N), out_dtype)
        block = (group, kb, N)

        def body(a_ref, g_ref, *rest):
            o_ref, acc_ref = refs(rest)

            @pl.when(pl.program_id(1) == 0)
            def _():
                acc_ref[...] = jnp.zeros(block, F32)

            gv = g_ref[...]
            if kb >= 2 * BLK:
                for t in range(group):
                    acc_ref[t] += lax.dot_general(a_ref[:, t * kb:(t + 1) * kb], gv, TN_DIMS,
                                                  preferred_element_type=F32)
            else:
                whole = lax.dot_general(a_ref[...], gv, TN_DIMS, preferred_element_type=F32)
                for t in range(group):
                    acc_ref[t] += whole[t * kb:(t + 1) * kb]
            store(o_ref, acc_ref)

    scratch = [pltpu.VMEM(block, F32)] if wire else []
    if into is None:
        return pl.pallas_call(
            body, name=name, grid=grid, in_specs=in_specs, out_specs=out_spec, out_shape=out_shape,
            scratch_shapes=scratch, compiler_params=_params("parallel", "arbitrary"),
        )(a, g)
    return pl.pallas_call(
        body, name=name, grid=grid, in_specs=in_specs + [ANY], out_specs=out_spec, out_shape=out_shape,
        scratch_shapes=scratch, input_output_aliases={2: 0}, compiler_params=_params("parallel", "arbitrary"),
    )(a, g, into)


def _rstd(x):
    return lax.rsqrt(jnp.mean(x * x, axis=-1, keepdims=True) + RMS_EPS)


def _norm_fwd(name, x, gain):
    L, D = x.shape
    tr = _row_tile(L, 640)

    def body(x_ref, g_ref, o_ref):
        xv = x_ref[...]
        o_ref[...] = (xv * _rstd(xv) * g_ref[...]).astype(BF16)

    return pl.pallas_call(
        body, name=name, grid=(L // tr,),
        in_specs=[pl.BlockSpec((tr, D), lambda i: (i, 0)), pl.BlockSpec((1, D), lambda i: (0, 0))],
        out_specs=pl.BlockSpec((tr, D), lambda i: (i, 0)),
        out_shape=jax.ShapeDtypeStruct((L, D), BF16),
        compiler_params=_params("parallel"),
    )(x, gain)


def _res_norm(name, h, branch, g_post, g_pre):
    L, D = h.shape
    tr = _row_tile(L, 640)
    npre = len(g_pre)

    def body(h_ref, b_ref, gp_ref, *rest):
        bv = b_ref[...]
        hn = h_ref[...] + bv * _rstd(bv) * gp_ref[...]
        rest[npre][...] = hn
        if npre:
            xh = hn * _rstd(hn)
            for t in range(npre):
                rest[npre + 1 + t][...] = (xh * rest[t][...]).astype(BF16)

    row = pl.BlockSpec((tr, D), lambda i: (i, 0))
    gain = pl.BlockSpec((1, D), lambda i: (0, 0))
    return pl.pallas_call(
        body, name=name, grid=(L // tr,),
        in_specs=[row, row, gain] + [gain] * npre,
        out_specs=[row] * (1 + npre),
        out_shape=[jax.ShapeDtypeStruct((L, D), F32)] + [jax.ShapeDtypeStruct((L, D), BF16)] * npre,
        compiler_params=_params("parallel"),
    )(h, branch, g_post, *g_pre)


def _norm_bwd(name, dres, stream=None, pre=(), post=None):
    L, D = dres.shape
    tr = _row_tile(L, 640)
    npre = len(pre)
    has_post = post is not None
    n_in = 1 + (1 + 2 * npre if npre else 0) + (2 if has_post else 0)
    n_dg = npre + (1 if has_post else 0)

    def body(*refs):
        ins, outs = refs[:n_in], refs[n_in:]
        i = pl.program_id(0)
        dsum = ins[0][...]
        dgs = []
        pos = 1
        if npre:
            xv = ins[pos][...]
            pos += 1
            rs = _rstd(xv)
            xh = xv * rs
            dxh = None
            for t in range(npre):
                gv, dy = ins[pos][...], ins[pos + 1][...]
                pos += 2
                dgs.append(jnp.sum(dy * xh, axis=0, keepdims=True))
                term = dy * gv
                dxh = term if dxh is None else dxh + term
            dsum = dsum + rs * (dxh - xh * jnp.mean(dxh * xh, axis=-1, keepdims=True))
        outs[0][...] = dsum
        o = 1
        if has_post:
            bv, gp = ins[pos][...], ins[pos + 1][...]
            rs = _rstd(bv)
            bh = bv * rs
            dgs.append(jnp.sum(dsum * bh, axis=0, keepdims=True))
            dbh = dsum * gp
            outs[1][...] = (rs * (dbh - bh * jnp.mean(dbh * bh, axis=-1, keepdims=True))).astype(BF16)
            o = 2

        @pl.when(i == 0)
        def _():
            for t in range(n_dg):
                outs[o + t][...] = jnp.zeros((8, D), F32)

        for t in range(n_dg):
            outs[o + t][0:1, :] += dgs[t]

    row = pl.BlockSpec((tr, D), lambda i: (i, 0))
    gain = pl.BlockSpec((1, D), lambda i: (0, 0))
    acc = pl.BlockSpec((8, D), lambda i: (0, 0))
    args, in_specs = [dres], [row]
    if npre:
        args.append(stream)
        in_specs.append(row)
        for gv, dy in pre:
            args += [gv, dy]
            in_specs += [gain, row]
    if has_post:
        args += [post[0], post[1]]
        in_specs += [row, gain]
    out_specs = [row] + ([row] if has_post else []) + [acc] * n_dg
    out_shape = ([jax.ShapeDtypeStruct((L, D), F32)] + ([jax.ShapeDtypeStruct((L, D), BF16)] if has_post else [])
                 + [jax.ShapeDtypeStruct((8, D), F32)] * n_dg)
    return pl.pallas_call(
        body, name=name, grid=(L // tr,), in_specs=in_specs, out_specs=out_specs, out_shape=out_shape,
        compiler_params=_params("arbitrary"),
    )(*args)


def _conv_rows(L):
    return max(t for t in range(BLK, 5 * BLK + 1, BLK) if L % t == 0)


def _conv_fwd(name, p, taps):
    L, D3 = p.shape
    D = D3 // 3
    tr = _conv_rows(L)

    def body(p_ref, prev_ref, w_ref, y_ref, u_scr):
        i = pl.program_id(0)
        for lo in range(0, D, BLK):
            cb, cc, cx = slice(lo, lo + BLK), slice(D + lo, D + lo + BLK), slice(2 * D + lo, 2 * D + lo + BLK)
            u_scr[0:8, :] = jnp.where(i > 0, prev_ref[:, cc] * prev_ref[:, cx], 0.0)
            for r0 in range(0, tr, BLK):
                u_scr[8 + r0:8 + r0 + BLK, :] = p_ref[r0:r0 + BLK, cc] * p_ref[r0:r0 + BLK, cx]
            for r0 in range(0, tr, BLK):
                conv = (w_ref[0:1, cb] * u_scr[6 + r0:6 + r0 + BLK, :] + w_ref[1:2, cb] * u_scr[7 + r0:7 + r0 + BLK, :]
                        + w_ref[2:3, cb] * u_scr[8 + r0:8 + r0 + BLK, :])
                y_ref[r0:r0 + BLK, cb] = (p_ref[r0:r0 + BLK, cb] * conv).astype(BF16)

    return pl.pallas_call(
        body, name=name, grid=(L // tr,),
        in_specs=[pl.BlockSpec((tr, D3), lambda i: (i, 0)),
                  pl.BlockSpec((8, D3), lambda i: (jnp.maximum(i * (tr // 8) - 1, 0), 0)),
                  pl.BlockSpec((8, D), lambda i: (0, 0))],
        out_specs=pl.BlockSpec((tr, D), lambda i: (i, 0)),
        out_shape=jax.ShapeDtypeStruct((L, D), BF16),
        scratch_shapes=[pltpu.VMEM((tr + 8, BLK), F32)],
        compiler_params=_params("parallel"),
    )(p, p, taps)


def _conv_bwd(name, p, dy, taps):
    L, D3 = p.shape
    D = D3 // 3
    tr = _conv_rows(L)
    nblk = L // tr
    last8 = L // 8 - 1

    def body(p_ref, prev_ref, next_ref, dy_ref, dyn_ref, w_ref, dp_ref, dw_ref, u_scr, d_scr):
        i = pl.program_id(0)

        @pl.when(i == 0)
        def _():
            dw_ref[...] = jnp.zeros((8, D), F32)

        for lo in range(0, D, BLK):
            cb, cc, cx = slice(lo, lo + BLK), slice(D + lo, D + lo + BLK), slice(2 * D + lo, 2 * D + lo + BLK)
            w0, w1, w2 = w_ref[0:1, cb], w_ref[1:2, cb], w_ref[2:3, cb]
            u_scr[0:8, :] = jnp.where(i > 0, prev_ref[:, cc] * prev_ref[:, cx], 0.0)
            d_scr[tr:tr + 8, :] = jnp.where(i < nblk - 1, dyn_ref[:, cb] * next_ref[:, cb], 0.0)
            for r0 in range(0, tr, BLK):
                rows = slice(r0, r0 + BLK)
                u_scr[8 + r0:8 + r0 + BLK, :] = p_ref[rows, cc] * p_ref[rows, cx]
                d_scr[rows, :] = dy_ref[rows, cb] * p_ref[rows, cb]
            dws = [None, None, None]
            for r0 in range(0, tr, BLK):
                rows = slice(r0, r0 + BLK)
                us = [u_scr[6 + r0:6 + r0 + BLK, :], u_scr[7 + r0:7 + r0 + BLK, :], u_scr[8 + r0:8 + r0 + BLK, :]]
                conv = w0 * us[0] + w1 * us[1] + w2 * us[2]
                dconv = d_scr[rows, :]
                du = w2 * dconv + w1 * d_scr[1 + r0:1 + r0 + BLK, :] + w0 * d_scr[2 + r0:2 + r0 + BLK, :]
                dp_ref[rows, cb] = (dy_ref[rows, cb] * conv).astype(BF16)
                dp_ref[rows, cc] = (du * p_ref[rows, cx]).astype(BF16)
                dp_ref[rows, cx] = (du * p_ref[rows, cc]).astype(BF16)
                for t in range(3):
                    part = jnp.sum(dconv * us[t], axis=0, keepdims=True)
                    dws[t] = part if dws[t] is None else dws[t] + part
            for t in range(3):
                dw_ref[t:t + 1, cb] += dws[t]

    halo_prev = lambda i: (jnp.maximum(i * (tr // 8) - 1, 0), 0)
    halo_next = lambda i: (jnp.minimum((i + 1) * (tr // 8), last8), 0)
    return pl.pallas_call(
        body, name=name, grid=(nblk,),
        in_specs=[pl.BlockSpec((tr, D3), lambda i: (i, 0)), pl.BlockSpec((8, D3), halo_prev),
                  pl.BlockSpec((8, D3), halo_next), pl.BlockSpec((tr, D), lambda i: (i, 0)),
                  pl.BlockSpec((8, D), halo_next), pl.BlockSpec((8, D), lambda i: (0, 0))],
        out_specs=[pl.BlockSpec((tr, D3), lambda i: (i, 0)), pl.BlockSpec((8, D), lambda i: (0, 0))],
        out_shape=[jax.ShapeDtypeStruct((L, D3), BF16), jax.ShapeDtypeStruct((8, D), F32)],
        scratch_shapes=[pltpu.VMEM((tr + 8, BLK), F32), pltpu.VMEM((tr + 8, BLK), F32)],
        compiler_params=_params("arbitrary"),
    )(p, p, p, dy, dy, taps)


KB_STEP = 3
FAR = 1 << 30


def _stack_heads(x):
    lane = lax.broadcasted_iota(jnp.int32, x.shape, 1)
    zero = jnp.zeros_like(x)
    return jnp.concatenate([jnp.where(lane < HEAD_DIM, x, zero), jnp.where(lane >= HEAD_DIM, x, zero)], axis=0)


def _unstack_heads(x2):
    lane = lax.broadcasted_iota(jnp.int32, (BLK, BLK), 1)
    return jnp.where(lane < HEAD_DIM, x2[0:BLK], x2[BLK:2 * BLK])


def _key_block(kb, u):
    kbu = kb - u
    off = pl.multiple_of(jnp.maximum(kbu, 0) * BLK, BLK)
    return off, jnp.where(kbu >= 0, kbu * BLK, FAR)


def _attn_iotas(i):
    lane = lax.broadcasted_iota(jnp.int32, (2 * BLK, BLK), 1)
    row = lax.broadcasted_iota(jnp.int32, (2 * BLK, BLK), 0)
    krow = lax.broadcasted_iota(jnp.int32, (BLK, BLK), 0)
    klane = lax.broadcasted_iota(jnp.int32, (BLK, BLK), 1)
    return lane, i * BLK + (row & (BLK - 1)), krow, klane


def _score_block(q2, kblk, kbase, qpos, lane, rr, tri):
    z = lax.dot_general(q2, kblk, NT_DIMS, preferred_element_type=F32)
    kpos = kbase + lane
    vis = (kpos < qpos) & (kpos >= PAD)
    sp = jnp.log(1.0 + jnp.exp(-jnp.abs(z)))
    lb = jnp.minimum(z, 0.0) - sp
    lom = jnp.where(vis, lb - z, 0.0)
    hi = lom.astype(BF16)
    lo = (lom - hi.astype(F32)).astype(BF16)
    cs = jnp.dot(hi, tri, preferred_element_type=F32) + jnp.dot(lo, tri, preferred_element_type=F32)
    a = jnp.where(vis, jnp.exp(lb + rr + cs), 0.0)
    return vis, lb, lom, a


def _attn_fwd(name, q, k, v, gather=None):
    L, D = q.shape
    nh, nq = D // BLK, L // BLK
    g_in = gather.operands() if gather else []
    ngi, ngo = len(g_in), (gather.n if gather else 0)

    def body(q_ref, k_ref, v_ref, *rest):
        of_ref, ob_ref, aw_ref, bw_ref, fl_ref = rest[ngi:ngi + 5]
        acc_ref, r_ref = rest[ngi + 5 + ngo:ngi + 7 + ngo]
        i = pl.program_id(1)
        if gather:
            start, forward, finish = gather.bind(rest[:ngi], rest[ngi + 5:ngi + 5 + ngo], rest[ngi + 7 + ngo:])
            grid_step = pl.program_id(0) * nq + i
            pl.when(grid_step == 0)(start)
            pl.when(grid_step == nh * nq // 2)(forward)
        lane, qpos, krow, klane = _attn_iotas(i)
        tri = (krow > klane).astype(BF16)
        q2 = _stack_heads(q_ref[...])
        acc_ref[...] = jnp.zeros((2 * BLK, BLK), F32)
        r_ref[...] = jnp.zeros((2 * BLK, 1), F32)

        def blocks(kb, save):
            rr = r_ref[...]
            upd = None
            for u in range(KB_STEP):
                off, kbase = _key_block(kb, u)
                vis, lb, lom, a = _score_block(q2, k_ref[pl.ds(off, BLK), :], kbase, qpos, lane, rr, tri)
                ab = a.astype(BF16)
                if save:
                    aw_ref[:, u * BLK:(u + 1) * BLK] = ab
                    bw_ref[:, u * BLK:(u + 1) * BLK] = jnp.where(vis, jnp.exp(lb), 0.0).astype(BF16)
                part = jnp.dot(ab, v_ref[pl.ds(off, BLK), :], preferred_element_type=F32)
                upd = part if upd is None else upd + part
                rr = rr + jnp.sum(lom, axis=1, keepdims=True)
            acc_ref[...] += upd
            r_ref[...] = rr
            return (jnp.max(rr) < LOG_ZERO).astype(jnp.int32)

        done = blocks(i, True)
        kb_end, _ = lax.while_loop(lambda c: (c[0] >= 0) & (c[1] == 0),
                                   lambda c: (c[0] - KB_STEP, blocks(c[0], False)), (i - KB_STEP, done))
        fl_ref[...] = jnp.zeros((8, BLK), F32) + (kb_end == i - KB_STEP).astype(F32)
        o = _unstack_heads(acc_ref[...])
        of_ref[...] = o
        ob_ref[...] = o.astype(BF16)
        if gather:
            pl.when(grid_step == nh * nq - 1)(finish)

    qspec = pl.BlockSpec((BLK, BLK), lambda h, i: (i, h))
    kvspec = pl.BlockSpec((L, BLK), lambda h, i: (0, h))
    wspec = pl.BlockSpec((None, None, 2 * BLK, KB_STEP * BLK), lambda h, i: (h, i, 0, 0))
    fspec = pl.BlockSpec((None, None, 8, BLK), lambda h, i: (h, i, 0, 0))
    return pl.pallas_call(
        body, name=name, grid=(nh, nq),
        in_specs=[qspec, kvspec, kvspec] + [ANY] * ngi, out_specs=[qspec, qspec, wspec, wspec, fspec] + [ANY] * ngo,
        out_shape=[jax.ShapeDtypeStruct((L, D), F32), jax.ShapeDtypeStruct((L, D), BF16),
                   jax.ShapeDtypeStruct((nh, nq, 2 * BLK, KB_STEP * BLK), BF16),
                   jax.ShapeDtypeStruct((nh, nq, 2 * BLK, KB_STEP * BLK), BF16),
                   jax.ShapeDtypeStruct((nh, nq, 8, BLK), F32)] + (gather.out_shapes() if gather else []),
        input_output_aliases=gather.aliases(3, 5) if gather else {},
        scratch_shapes=[pltpu.VMEM((2 * BLK, BLK), F32), pltpu.VMEM((2 * BLK, 1), F32)]
        + (gather.scratch() if gather else []),
        compiler_params=_params("arbitrary", "arbitrary") if gather else _params("parallel", "arbitrary"),
    )(q, k, v, *g_in)


def _attn_bwd(name, q, k, v, o, do, a_win, b_win, flag, side=None):
    L, D = q.shape
    nh, nq = D // BLK, L // BLK
    ns = side.n if side else 0

    def body(q_ref, k_ref, v_ref, o_ref, do_ref, aw_ref, bw_ref, fl_ref, *rest):
        dq_ref, dk_ref, dv_ref = rest[ns:ns + 3]
        acc_ref, r_ref, c_ref = rest[2 * ns + 3:2 * ns + 6]
        i = pl.program_id(1)
        if side:
            start, finish = side.bind(rest[:ns], rest[ns + 3:2 * ns + 3], rest[2 * ns + 6:])
            grid_step = pl.program_id(0) * nq + i
            pl.when(grid_step == 0)(start)

        @pl.when(i == 0)
        def _():
            dk_ref[...] = jnp.zeros((L, BLK), F32)
            dv_ref[...] = jnp.zeros((L, BLK), F32)

        krow = lax.broadcasted_iota(jnp.int32, (BLK, BLK), 0)
        klane = lax.broadcasted_iota(jnp.int32, (BLK, BLK), 1)
        tri_incl = (krow >= klane).astype(BF16)
        q2 = _stack_heads(q_ref[...])
        do2 = _stack_heads(do_ref[...].astype(BF16))
        total = jnp.sum(do2.astype(F32) * jnp.concatenate([o_ref[...]] * 2, axis=0), axis=1, keepdims=True)

        def grads(off, kblk, ab, beta, cc):
            da = lax.dot_general(do2, v_ref[pl.ds(off, BLK), :], NT_DIMS, preferred_element_type=F32)
            g = da * ab.astype(F32)
            ghi = g.astype(BF16)
            glo = (g - ghi.astype(F32)).astype(BF16)
            sfx = (jnp.dot(ghi, tri_incl, preferred_element_type=F32)
                   + jnp.dot(glo, tri_incl, preferred_element_type=F32))
            before = total - cc - sfx
            dz = (g * (1.0 - beta) - beta * before).astype(BF16)
            dk_ref[pl.ds(off, BLK), :] += lax.dot_general(dz, q2, TN_DIMS, preferred_element_type=F32)
            dv_ref[pl.ds(off, BLK), :] += lax.dot_general(ab, do2, TN_DIMS, preferred_element_type=F32)
            return jnp.dot(dz, kblk, preferred_element_type=F32), cc + jnp.sum(g, axis=1, keepdims=True)

        fast = jnp.max(fl_ref[...]) > 0.5

        @pl.when(fast)
        def _():
            cc = jnp.zeros((2 * BLK, 1), F32)
            upd = None
            for u in range(KB_STEP):
                off, _ = _key_block(i, u)
                cols = slice(u * BLK, (u + 1) * BLK)
                part, cc = grads(off, k_ref[pl.ds(off, BLK), :], aw_ref[:, cols], bw_ref[:, cols].astype(F32), cc)
                upd = part if upd is None else upd + part
            dq_ref[...] = (_unstack_heads(upd) * ATTN_SCALE).astype(BF16)

        @pl.when(jnp.logical_not(fast))
        def _():
            lane, qpos, _, _ = _attn_iotas(i)
            tri = (krow > klane).astype(BF16)
            acc_ref[...] = jnp.zeros((2 * BLK, BLK), F32)
            r_ref[...] = jnp.zeros((2 * BLK, 1), F32)
            c_ref[...] = jnp.zeros((2 * BLK, 1), F32)

            def step(carry):
                kb, _ = carry
                rr = r_ref[...]
                cc = c_ref[...]
                upd = None
                for u in range(KB_STEP):
                    off, kbase = _key_block(kb, u)
                    kblk = k_ref[pl.ds(off, BLK), :]
                    vis, lb, lom, a = _score_block(q2, kblk, kbase, qpos, lane, rr, tri)
                    part, cc = grads(off, kblk, a.astype(BF16), jnp.where(vis, jnp.exp(lb), 0.0), cc)
                    upd = part if upd is None else upd + part
                    rr = rr + jnp.sum(lom, axis=1, keepdims=True)
                acc_ref[...] += upd
                c_ref[...] = cc
                r_ref[...] = rr
                return kb - KB_STEP, (jnp.max(rr) < LOG_ZERO).astype(jnp.int32)

            lax.while_loop(lambda c: (c[0] >= 0) & (c[1] == 0), step, (i, jnp.int32(0)))
            dq_ref[...] = (_unstack_heads(acc_ref[...]) * ATTN_SCALE).astype(BF16)

        if side:
            pl.when(grid_step == nh * nq - 1)(finish)

    qspec = pl.BlockSpec((BLK, BLK), lambda h, i: (i, h))
    kvspec = pl.BlockSpec((L, BLK), lambda h, i: (0, h))
    wspec = pl.BlockSpec((None, None, 2 * BLK, KB_STEP * BLK), lambda h, i: (h, i, 0, 0))
    fspec = pl.BlockSpec((None, None, 8, BLK), lambda h, i: (h, i, 0, 0))
    return pl.pallas_call(
        body, name=name, grid=(nh, nq),
        in_specs=[qspec, kvspec, kvspec, qspec, qspec, wspec, wspec, fspec] + [ANY] * ns,
        out_specs=[qspec, kvspec, kvspec] + [ANY] * ns,
        out_shape=[jax.ShapeDtypeStruct((L, D), BF16), jax.ShapeDtypeStruct((L, D), F32),
                   jax.ShapeDtypeStruct((L, D), F32)] + (side.out_shapes() if side else []),
        scratch_shapes=[pltpu.VMEM((2 * BLK, BLK), F32), pltpu.VMEM((2 * BLK, 1), F32),
                        pltpu.VMEM((2 * BLK, 1), F32)] + (side.scratch() if side else []),
        compiler_params=_params("arbitrary", "arbitrary") if side else _params("parallel", "arbitrary"),
    )(q, k, v, o, do, a_win, b_win, flag, *(side.operands() if side else []))


def _loss_and_grad(name, h, target):
    L, D = h.shape
    first = (PAD + N_META) // BLK

    def body(h_ref, t_ref, sq_ref, dh_ref):
        i = pl.program_id(0)

        @pl.when(i == 0)
        def _():
            sq_ref[...] = jnp.zeros((8, BLK), F32)

        @pl.when(i < first)
        def _():
            dh_ref[...] = jnp.zeros((BLK, D), F32)

        @pl.when(i >= first)
        def _():
            err = h_ref[...] - t_ref[...]
            sq_ref[...] += jnp.sum(err * err)
            dh_ref[...] = err * (1.0 / D)

    return pl.pallas_call(
        body, name=name, grid=(L // BLK,),
        in_specs=[pl.BlockSpec((BLK, D), lambda i: (i, 0)),
                  pl.BlockSpec((BLK, D), lambda i: (jnp.maximum(i - first, 0), 0))],
        out_specs=[pl.BlockSpec((8, BLK), lambda i: (0, 0)), pl.BlockSpec((BLK, D), lambda i: (i, 0))],
        out_shape=[jax.ShapeDtypeStruct((8, BLK), F32), jax.ShapeDtypeStruct((L, D), F32)],
        compiler_params=_params("arbitrary"),
    )(h, target)


def _add_cast(name, a, b):
    L, D = a.shape
    tr = _row_tile(L, 640)

    def body(a_ref, b_ref, o_ref):
        o_ref[...] = (a_ref[...] + b_ref[...]).astype(BF16)

    row = pl.BlockSpec((tr, D), lambda i: (i, 0))
    return pl.pallas_call(body, name=name, grid=(L // tr,), in_specs=[row, row], out_specs=row,
                          out_shape=jax.ShapeDtypeStruct((L, D), BF16), compiler_params=_params("parallel"))(a, b)


def _adamw(w, g, m, v):
    m = ADAM_B1 * m + (1.0 - ADAM_B1) * g
    v = ADAM_B2 * v + (1.0 - ADAM_B2) * (g * g)
    m_hat = m / (1.0 - ADAM_B1 ** ADAM_STEP)
    v_hat = v / (1.0 - ADAM_B2 ** ADAM_STEP)
    delta = -ADAM_LR * (m_hat / (jnp.sqrt(v_hat) + ADAM_EPS) + ADAM_WD * w)
    return delta, m, v


def _adam_small(name, w, g, m, v):
    def body(w_ref, g_ref, m_ref, v_ref, d_ref, mo_ref, vo_ref):
        d, mn, vn = _adamw(w_ref[...], g_ref[...], m_ref[...], v_ref[...])
        d_ref[...] = d
        mo_ref[...] = mn
        vo_ref[...] = vn

    return pl.pallas_call(body, name=name, out_shape=[jax.ShapeDtypeStruct(w.shape, F32)] * 3)(w, g, m, v)


def _sum_slots(name, slots):
    def body(s_ref, o_ref):
        acc = s_ref[0]
        for d in range(1, N_DEV):
            acc = acc + s_ref[d]
        o_ref[...] = acc

    return pl.pallas_call(body, name=name, out_shape=jax.ShapeDtypeStruct(slots.shape[1:], F32))(slots)


def _sum8_adam(name, g, recv, order, w, m, v):
    R, C = w.shape
    tr = _row_tile(R, 512)

    def body(order_ref, g_ref, *rest):
        w_ref, m_ref, v_ref, go_ref, d_ref, mo_ref, vo_ref = rest[N_DEV - 1:]
        gsum = g_ref[...].astype(F32)
        for r_ref in rest[:N_DEV - 1]:
            gsum = gsum + r_ref[...].astype(F32)
        d, mn, vn = _adamw(w_ref[...], gsum, m_ref[...], v_ref[...])
        go_ref[...] = gsum
        d_ref[...] = d
        mo_ref[...] = mn
        vo_ref[...] = vn

    row = pl.BlockSpec((tr, C), lambda i, order: (i, 0))
    slot = lambda r: pl.BlockSpec((None, tr, C), lambda i, order: (order[r], i, 0))
    return pl.pallas_call(
        body, name=name,
        grid_spec=pltpu.PrefetchScalarGridSpec(
            num_scalar_prefetch=1, grid=(R // tr,),
            in_specs=[slot(r) for r in range(N_DEV)] + [row, row, row], out_specs=[row] * 4),
        out_shape=[jax.ShapeDtypeStruct((R, C), F32)] * 4,
        compiler_params=_params("parallel"),
    )(order, g, *([recv] * (N_DEV - 1)), w, m, v)


def _pair_sum(name, g, recv, c_idx):
    _, R, C = g.shape
    tr = _row_tile(R, 512)

    def body(sel_ref, g_ref, r_ref, o_ref):
        o_ref[...] = (g_ref[...].astype(F32) + r_ref[...].astype(F32)).astype(BF16)

    return pl.pallas_call(
        body, name=name,
        grid_spec=pltpu.PrefetchScalarGridSpec(
            num_scalar_prefetch=1, grid=(4, R // tr),
            in_specs=[pl.BlockSpec((None, tr, C), lambda k, i, sel: (2 * k + sel[0], i, 0)),
                      pl.BlockSpec((None, tr, C), lambda k, i, sel: (k, i, 0))],
            out_specs=pl.BlockSpec((None, tr, C), lambda k, i, sel: (k, i, 0))),
        out_shape=jax.ShapeDtypeStruct((4, R, C), BF16),
        compiler_params=_params("parallel", "parallel"),
    )(c_idx, g, recv)


def _final_sum_adam(name, part, recv, chip_idx, w, m, v):
    R, C = w.shape
    tr = _row_tile(R, 512)

    def body(sel_ref, p_ref, r_ref, w_ref, m_ref, v_ref, g_ref, d_ref, mo_ref, vo_ref):
        g = ((p_ref[...].astype(F32) + r_ref[0].astype(F32)) + r_ref[1].astype(F32)) + r_ref[2].astype(F32)
        d, mn, vn = _adamw(w_ref[...], g, m_ref[...], v_ref[...])
        g_ref[...] = g
        d_ref[...] = d
        mo_ref[...] = mn
        vo_ref[...] = vn

    row = pl.BlockSpec((tr, C), lambda i, sel: (i, 0))
    return pl.pallas_call(
        body, name=name,
        grid_spec=pltpu.PrefetchScalarGridSpec(
            num_scalar_prefetch=1, grid=(R // tr,),
            in_specs=[pl.BlockSpec((None, tr, C), lambda i, sel: (sel[0], i, 0)),
                      pl.BlockSpec((3, tr, C), lambda i, sel: (0, i, 0)), row, row, row],
            out_specs=[row] * 4),
        out_shape=[jax.ShapeDtypeStruct((R, C), F32)] * 4,
        compiler_params=_params("parallel"),
    )(chip_idx, part, recv, w, m, v)


def _position():
    return lax.axis_index("x"), lax.axis_index("y"), lax.axis_index("c")


class _Gather:
    def __init__(self, items):
        self.items = items
        self.n = len(items)
        self.filled = [a for a, it in enumerate(items) if it[2] is not None]

    def operands(self):
        return [it[0] for it in self.items] + [self.items[a][2] for a in self.filled]

    def out_shapes(self):
        return [jax.ShapeDtypeStruct((N_DEV,) + it[0].shape, it[0].dtype) for it in self.items]

    def aliases(self, first_in, first_out):
        return {first_in + self.n + k: first_out + a for k, a in enumerate(self.filled)}

    def scratch(self):
        return [pltpu.SemaphoreType.DMA((self.n, 7)), pltpu.SemaphoreType.DMA((self.n, 7)),
                pltpu.SemaphoreType.DMA((self.n,))]

    def bind(self, in_refs, out_refs, sems):
        send_sems, recv_sems, local_sems = sems
        n, items = self.n, self.items
        x, y, c = _position()
        me, sibling = (x, y, c), (x, y, 1 - c)
        chips = [(1 - x, y), (x, 1 - y), (1 - x, 1 - y)]
        src = [in_refs[a].at[items[a][1]] for a in range(n)]

        def copy(a, sem, block, to, own=False):
            dst = out_refs[a].at[4 * block[0] + 2 * block[1] + block[2], items[a][1]]
            return pltpu.make_async_remote_copy(
                src_ref=src[a] if own else dst, dst_ref=dst,
                send_sem=send_sems.at[a, sem], recv_sem=recv_sems.at[a, sem], device_id=to, device_id_type=MESH)

        def local(a):
            return pltpu.make_async_copy(src[a], out_refs[a].at[4 * x + 2 * y + c, items[a][1]], local_sems.at[a])

        def first():
            return [cp for a in range(n) for cp in
                    [copy(a, 0, me, sibling, own=True)]
                    + [copy(a, 1 + j, me, (*chip, c), own=True) for j, chip in enumerate(chips)]]

        def start():
            for a in range(n):
                local(a).start()
            for cp in first():
                cp.start()

        def forward():
            for j, chip in enumerate(chips):
                for a in range(n):
                    copy(a, 1 + j, (*chip, c), me).wait_recv()
                    copy(a, 4 + j, (*chip, c), sibling).start()

        def finish():
            for a in range(n):
                copy(a, 0, sibling, me).wait_recv()
            for j, chip in enumerate(chips):
                for a in range(n):
                    copy(a, 4 + j, (*chip, 1 - c), me).wait_recv()
            for cp in first():
                cp.wait_send()
            for j, chip in enumerate(chips):
                for a in range(n):
                    copy(a, 4 + j, (*chip, c), sibling).wait_send()
            for a in range(n):
                local(a).wait()

        return start, forward, finish


def _all_gather(name, gather):
    n_in = len(gather.operands())

    def body(*refs):
        start, forward, finish = gather.bind(refs[:n_in], refs[n_in:n_in + gather.n], refs[n_in + gather.n:])
        start()
        forward()
        finish()

    return pl.pallas_call(
        body, name=name, in_specs=[ANY] * n_in, out_specs=[ANY] * gather.n, out_shape=gather.out_shapes(),
        input_output_aliases=gather.aliases(0, 0), scratch_shapes=gather.scratch(),
        compiler_params=pltpu.CompilerParams(has_side_effects=True),
    )(*gather.operands())


class _Direct:
    def __init__(self, items):
        self.items = items
        self.n = len(items)

    def operands(self):
        return list(self.items)

    def out_shapes(self):
        return [jax.ShapeDtypeStruct(it.shape, it.dtype) for it in self.items]

    def scratch(self):
        return [pltpu.SemaphoreType.DMA((self.n, 7)), pltpu.SemaphoreType.DMA((self.n, 7))]

    def bind(self, in_refs, out_refs, sems):
        send_sems, recv_sems = sems
        x, y, c = _position()

        def copies():
            out = []
            for a in range(self.n):
                for r in range(1, N_DEV):
                    peer = (1 - x if r & 4 else x, 1 - y if r & 2 else y, 1 - c if r & 1 else c)
                    out.append(pltpu.make_async_remote_copy(
                        src_ref=in_refs[a].at[4 * peer[0] + 2 * peer[1] + peer[2]],
                        dst_ref=out_refs[a].at[4 * x + 2 * y + c],
                        send_sem=send_sems.at[a, r - 1], recv_sem=recv_sems.at[a, r - 1],
                        device_id=peer, device_id_type=MESH))
            return out

        def start():
            for cp in copies():
                cp.start()

        def finish():
            for cp in copies():
                cp.wait()

        return start, finish


def _exchange_sibling(name, grads):
    n = len(grads)

    def body(*refs):
        g_refs, r_refs = refs[:n], refs[n:2 * n]
        send_sems, recv_sems = refs[2 * n:]
        x, y, c = _position()
        copies = []
        for a in range(n):
            for k in range(4):
                copies.append(pltpu.make_async_remote_copy(
                    src_ref=g_refs[a].at[2 * k + (1 - c)], dst_ref=r_refs[a].at[k],
                    send_sem=send_sems.at[a, k], recv_sem=recv_sems.at[a, k],
                    device_id=(x, y, 1 - c), device_id_type=MESH))
        for cp in copies:
            cp.start()
        for cp in copies:
            cp.wait()

    return pl.pallas_call(
        body, name=name, in_specs=[ANY] * n, out_specs=[ANY] * n,
        out_shape=[jax.ShapeDtypeStruct((4,) + g.shape[1:], g.dtype) for g in grads],
        scratch_shapes=[pltpu.SemaphoreType.DMA((n, 4)), pltpu.SemaphoreType.DMA((n, 4))],
        compiler_params=pltpu.CompilerParams(has_side_effects=True),
    )(*grads)


def _exchange_chips(name, parts):
    n = len(parts)

    def body(*refs):
        p_refs, r_refs = refs[:n], refs[n:2 * n]
        send_sems, recv_sems = refs[2 * n:]
        x, y, c = _position()
        chips = [(1 - x, y), (x, 1 - y), (1 - x, 1 - y)]
        copies = []
        for a in range(n):
            for j, chip in enumerate(chips):
                copies.append(pltpu.make_async_remote_copy(
                    src_ref=p_refs[a].at[2 * chip[0] + chip[1]], dst_ref=r_refs[a].at[j],
                    send_sem=send_sems.at[a, j], recv_sem=recv_sems.at[a, j],
                    device_id=(*chip, c), device_id_type=MESH))
        for cp in copies:
            cp.start()
        for cp in copies:
            cp.wait()

    return pl.pallas_call(
        body, name=name, in_specs=[ANY] * n, out_specs=[ANY] * n,
        out_shape=[jax.ShapeDtypeStruct((3,) + p.shape[1:], p.dtype) for p in parts],
        scratch_shapes=[pltpu.SemaphoreType.DMA((n, 3)), pltpu.SemaphoreType.DMA((n, 3))],
        compiler_params=pltpu.CompilerParams(has_side_effects=True),
    )(*parts)


def kernel(x, meta_tokens, norm_gains, conv_in_proj, conv_w, conv_out_proj, kv_norm, w_k, w_v, w_q, w_o, mlp_w1, mlp_w2, loss_target, m_meta_tokens, m_norm_gains, m_conv_in_proj, m_conv_w, m_conv_out_proj, m_kv_norm, m_w_k, m_w_v, m_w_q, m_w_o, m_mlp_w1, m_mlp_w2, v_meta_tokens, v_norm_gains, v_conv_in_proj, v_conv_w, v_conv_out_proj, v_kv_norm, v_w_k, v_w_v, v_w_q, v_w_o, v_mlp_w1, v_mlp_w2):
    xi, target = x[0], loss_target[0]
    S, D = xi.shape
    L = PAD + N_META + S
    dsh = D // N_DEV
    px, py, pc = _position()
    dev = 4 * px + 2 * py + pc
    tm_big = _row_tile(L, 1664)
    tm_mid = _row_tile(L, 640)

    def pack_small(meta, gains, taps):
        return jnp.concatenate([meta, gains.reshape(DEPTH * 4, dsh), taps.reshape(N_A * 3, dsh),
                                jnp.zeros((2, dsh), F32)], axis=0)

    big_w = [conv_in_proj, conv_out_proj, w_k[None], w_v[None], w_q, w_o, mlp_w1, mlp_w2]
    small_w = pack_small(meta_tokens, norm_gains, conv_w)
    w_names = ["win", "wout", "wk", "wv", "wq", "wo", "w1", "w2"]
    wb = {nm: w.astype(BF16) for nm, w in zip(w_names, big_w)}
    gw = {nm: None for nm in w_names}
    rides = {"conv_in0": [("wout", 0), ("w1", 0)], "mlp_up0": [("w2", 0), ("win", 1)],
             "mlp_down0": [("w1", 1), ("wout", 1)], "conv_in1": [("w2", 1)],
             "mlp_up1": [("wq", 0), ("wk", 0), ("wv", 0), ("w2", 2)], "mlp_down1": [("w1", 2), ("wo", 0)],
             "attn0": [("wq", 1), ("wo", 1), ("w1", 3), ("w2", 3)]}

    def gather_of(parts):
        return _Gather([(wb[nm], ly, gw[nm]) for nm, ly in parts]) if parts else None

    def hosting(call, name, *args, **kw):
        parts = rides.get(name, [])
        got = call(name, *args, gather=gather_of(parts), **kw)
        for (nm, _), arr in zip(parts, got[len(got) - len(parts):]):
            gw[nm] = arr
        return got[:len(got) - len(parts)]

    parts0 = [("win", 0)]
    got = _all_gather("gather_first", _Gather([(small_w[None], 0, None)] + gather_of(parts0).items))
    small_full = got[0][:, 0].transpose(1, 0, 2).reshape(40, D)
    for (nm, _), arr in zip(parts0, got[1:]):
        gw[nm] = arr
    meta_full = small_full[0:N_META]
    gain = lambda layer, n: small_full[N_META + 4 * layer + n][None]
    taps = [jnp.concatenate([small_full[32 + 3 * l:35 + 3 * l], jnp.zeros((5, D), F32)], axis=0) for l in range(N_A)]
    kvn = kv_norm[None]

    h = jnp.concatenate([jnp.zeros((PAD, D), F32), meta_full, xi], axis=0)
    n1 = _norm_fwd("norm_in", h, gain(0, 0))
    saved = []
    hk = k = v = None
    for layer in range(DEPTH):
        s = {"h0": h, "n1": n1}
        if layer < N_A:
            s["p"] = hosting(_mm_nn, f"conv_in{layer}", n1, gw["win"], layer, "col", [F32], tm_big)[0]
            s["y"] = _conv_fwd(f"conv{layer}", s["p"], taps[layer])
            mix = _mm_nn(f"conv_out{layer}", s["y"], gw["wout"], layer, "row", [F32], tm_big, tn=D)[0]
        else:
            j = layer - N_A
            if j == 0:
                k = _mm_nn("k_proj", hk, gw["wk"], 0, "row", [BF16], tm_big, tn=D)[0]
                v = _mm_nn("v_proj", hk, gw["wv"], 0, "row", [BF16], tm_big, tn=D)[0]
            s["q"] = _mm_nn(f"q_proj{j}", n1, gw["wq"], j, "row", [BF16], tm_big, tn=D,
                            epi=lambda acc: (acc * ATTN_SCALE,))[0]
            s["o"], s["ob"], s["aw"], s["bw"], s["fl"] = hosting(_attn_fwd, f"attn{j}", s["q"], k, v)
            mix = _mm_nn(f"o_proj{j}", s["ob"], gw["wo"], j, "row", [F32], tm_big, tn=D)[0]
        s["mix"] = mix
        s["h1"], s["n3"] = _res_norm(f"mix_norm{layer}", h, mix, gain(layer, 1), [gain(layer, 2)])
        relu2 = lambda acc: (jnp.square(jnp.maximum(acc, 0.0)),)
        s["act"] = hosting(_mm_nn, f"mlp_up{layer}", s["n3"], gw["w1"], layer, "col", [BF16], tm_big, epi=relu2)[0]
        s["ff"] = hosting(_mm_nn, f"mlp_down{layer}", s["act"], gw["w2"], layer, "row", [F32], tm_big, tn=D // 2)[0]
        pre = [] if layer == DEPTH - 1 else [gain(layer + 1, 0)] + ([kvn] if layer == N_A - 1 else [])
        outs = _res_norm(f"mlp_norm{layer}", s["h1"], s["ff"], gain(layer, 3), pre)
        h = outs[0]
        if pre:
            n1 = outs[1]
        if layer == N_A - 1:
            hk = outs[2]
        saved.append(s)

    sq, dh = _loss_and_grad("loss", h, target)
    loss = lax.psum(0.5 * sq[0, 0] / D, ("x", "y", "c"))

    g_gain = [[None] * 4 for _ in range(DEPTH)]
    g_taps = [None] * N_A
    g_w = {n: None for n in ("win", "wout", "wq", "wo", "w1a", "w1b", "w2a", "w2b")}
    flat = lambda a, lead: a.reshape(lead + (-1, a.shape[-1]))
    early_names = ["w_o", "mlp_w1b", "mlp_w2b"]
    early_g = early_recv = None
    dk_parts, dv_parts = [], []
    g_kvn = None
    s = saved[DEPTH - 1]
    dh, dff, g_gain[DEPTH - 1][3] = _norm_bwd("bwd_top", dh, post=(s["ff"], gain(DEPTH - 1, 3)))
    for layer in reversed(range(DEPTH)):
        s = saved[layer]
        def relu_grad(acc, act):
            act = act.astype(F32)
            return (acc * (2.0 * act * lax.rsqrt(jnp.maximum(act, 1e-30))),)

        da1 = _mm_nt(f"mlp_down_bwd{layer}", dff, gw["w2"], layer, "row", [BF16], tm_big, epi=relu_grad,
                     extras=(s["act"],))[0]
        half = "b" if layer >= DEPTH // 2 else "a"
        g_w["w2" + half] = _mm_tn(f"mlp_w2_grad{layer}", s["act"], dff, "row", tm_big, group=2,
                                  layer=layer % 2, layers=2, into=g_w["w2" + half], wire=True)
        g_w["w1" + half] = _mm_tn(f"mlp_w1_grad{layer}", s["n3"], da1, "col", tm_big, group=4,
                                  layer=layer % 2, layers=2, into=g_w["w1" + half], wire=True)
        dn3 = _mm_nt(f"mlp_up_bwd{layer}", da1, gw["w1"], layer, "col", [F32], tm_mid, tn=D)[0]
        dh, dmix, g_gain[layer][2], g_gain[layer][1] = _norm_bwd(
            f"bwd_mid{layer}", dh, stream=s["h1"], pre=[(gain(layer, 2), dn3)], post=(s["mix"], gain(layer, 1)))
        pre = []
        if layer < N_A:
            dy = _mm_nt(f"conv_out_bwd{layer}", dmix, gw["wout"], layer, "row", [F32], tm_big, group=4)[0]
            g_w["wout"] = _mm_tn(f"conv_out_grad{layer}", s["y"], dmix, "row", tm_big, group=4, layer=layer,
                                 layers=N_A, into=g_w["wout"], wire=True)
            dp, g_taps[layer] = _conv_bwd(f"conv_bwd{layer}", s["p"], dy, taps[layer])
            g_w["win"] = _mm_tn(f"conv_in_grad{layer}", s["n1"], dp, "col", tm_big, group=4, layer=layer, layers=N_A,
                                into=g_w["win"], wire=True)
            dn1 = _mm_nt(f"conv_in_bwd{layer}", dp, gw["win"], layer, "col", [F32], tm_mid, tn=D)[0]
        else:
            j = layer - N_A
            do = _mm_nt(f"o_proj_bwd{j}", dmix, gw["wo"], j, "row", [BF16], tm_big, group=4)[0]
            g_w["wo"] = _mm_tn(f"o_proj_grad{j}", s["ob"], dmix, "row", tm_big, group=4, layer=j,
                               layers=DEPTH - N_A, into=g_w["wo"], wire=True)
            if j == 0:
                early_g = [flat(g_w[nm], (N_DEV,)) for nm in ("wo", "w1b", "w2b")]
                got = _attn_bwd(f"attn_bwd{j}", s["q"], k, v, s["o"], do, s["aw"], s["bw"], s["fl"],
                                side=_Direct(early_g))
                early_recv = got[3:]
            else:
                got = _attn_bwd(f"attn_bwd{j}", s["q"], k, v, s["o"], do, s["aw"], s["bw"], s["fl"])
            dq, dk_j, dv_j = got[:3]
            dk_parts.append(dk_j)
            dv_parts.append(dv_j)
            g_w["wq"] = _mm_tn(f"q_proj_grad{j}", s["n1"], dq, "row", tm_big, group=4, layer=j,
                               layers=DEPTH - N_A, into=g_w["wq"], wire=True)
            dn1 = _mm_nt(f"q_proj_bwd{j}", dq, gw["wq"], j, "row", [F32], tm_big, group=4)[0]
            if j == 0:
                dkb = _add_cast("dk_sum", dk_parts[0], dk_parts[1])
                dvb = _add_cast("dv_sum", dv_parts[0], dv_parts[1])
                g_wk = _mm_tn("k_proj_grad", hk, dkb, "row", tm_big, group=4, wire=True)
                g_wv = _mm_tn("v_proj_grad", hk, dvb, "row", tm_big, group=4, wire=True)
                dhk_k = _mm_nt("k_proj_bwd", dkb, gw["wk"], 0, "row", [F32], tm_big, group=4)[0]
                dhk = _mm_nt("v_proj_bwd", dvb, gw["wv"], 0, "row", [F32], tm_big, group=4,
                             epi=lambda acc, other: (acc + other,), extras=(dhk_k,))[0]
                pre = [(kvn, dhk)]
        pre = [(gain(layer, 0), dn1)] + pre
        if layer > 0:
            sp = saved[layer - 1]
            outs = _norm_bwd(f"bwd_in{layer}", dh, stream=s["h0"], pre=pre, post=(sp["ff"], gain(layer - 1, 3)))
            dh, dff = outs[0], outs[1]
            g_gain[layer][0] = outs[2]
            if len(pre) == 2:
                g_kvn = outs[3]
            g_gain[layer - 1][3] = outs[-1]
        else:
            dh, g_gain[0][0] = _norm_bwd("bwd_in0", dh, stream=s["h0"], pre=pre)

    grad_x = dh[PAD + N_META:][None]
    g_meta = dh[PAD:PAD + N_META]

    small_g = jnp.concatenate(
        [g_meta] + [g_gain[l][n][0:1] for l in range(DEPTH) for n in range(4)]
        + [g_taps[l][0:3] for l in range(N_A)] + [g_kvn[0:1], jnp.zeros((1, D), F32)], axis=0)
    small_sum = _sum_slots("small_grad_sum", _all_gather("gather_small_grads", _Gather([(small_g[None], 0, None)]))[0][:, 0])
    small_mine = lax.dynamic_slice_in_dim(small_sum, dev * dsh, dsh, axis=1)
    small_m = pack_small(m_meta_tokens, m_norm_gains, m_conv_w)
    small_v = pack_small(v_meta_tokens, v_norm_gains, v_conv_w)
    small_d, small_mn, small_vn = _adam_small("adam_small", small_w, small_mine, small_m, small_v)
    pad8 = lambda a: jnp.concatenate([a[None], jnp.zeros((7, D), F32)], axis=0)
    g_kv = small_sum[38]
    kv_d, kv_mn, kv_vn = _adam_small("adam_kv_norm", pad8(kv_norm), pad8(g_kv), pad8(m_kv_norm), pad8(v_kv_norm))

    def unpack_small(a):
        return (a[0:N_META], a[N_META:N_META + 16].reshape(DEPTH, 4, dsh), a[32:38].reshape(N_A, 3, dsh))

    hi, lo = slice(DEPTH // 2, DEPTH), slice(0, DEPTH // 2)
    state = lambda ws, ms, vs, part: tuple(flat(t[part], ()) for t in (ws, ms, vs))
    shards = {"conv_in": (conv_in_proj, m_conv_in_proj, v_conv_in_proj), "conv_out": (conv_out_proj, m_conv_out_proj, v_conv_out_proj),
              "w_k": (w_k, m_w_k, v_w_k), "w_v": (w_v, m_w_v, v_w_v), "w_q": (w_q, m_w_q, v_w_q), "w_o": (w_o, m_w_o, v_w_o)}
    shards = {nm: state(*t, slice(None)) for nm, t in shards.items()}
    for nm, t in (("mlp_w1", (mlp_w1, m_mlp_w1, v_mlp_w1)), ("mlp_w2", (mlp_w2, m_mlp_w2, v_mlp_w2))):
        shards[nm + "a"], shards[nm + "b"] = state(*t, lo), state(*t, hi)
    done = {}
    order = jnp.stack([dev ^ r for r in range(N_DEV)]).astype(jnp.int32)
    for nm, g, recv in zip(early_names, early_g, early_recv):
        done[nm] = _sum8_adam(f"adam_{nm}", g, recv, order, *shards[nm])
    late_names = ["conv_in", "conv_out", "w_k", "w_v", "w_q", "mlp_w1a", "mlp_w2a"]
    late_g = [flat(g, (N_DEV,)) for g in (g_w["win"], g_w["wout"], g_wk, g_wv, g_w["wq"], g_w["w1a"], g_w["w2a"])]
    from_sibling = _exchange_sibling("grads_to_sibling", late_g)
    c_idx = jnp.reshape(pc, (1,)).astype(jnp.int32)
    chip_idx = jnp.reshape(2 * px + py, (1,)).astype(jnp.int32)
    parts = [_pair_sum(f"pair_sum_{nm}", g, r, c_idx) for nm, g, r in zip(late_names, late_g, from_sibling)]
    from_chips = _exchange_chips("grads_to_chips", parts)
    for nm, part, recv in zip(late_names, parts, from_chips):
        done[nm] = _final_sum_adam(f"adam_{nm}", part, recv, chip_idx, *shards[nm])

    def assemble(kind, small_parts, kv_part):
        meta_p, gains_p, taps_p = small_parts
        whole = lambda nm, like: done[nm][kind].reshape(like.shape)
        halves = lambda nm, like: jnp.concatenate([done[nm + "a"][kind], done[nm + "b"][kind]], axis=0).reshape(like.shape)
        return [meta_p, gains_p, whole("conv_in", conv_in_proj), taps_p, whole("conv_out", conv_out_proj), kv_part,
                whole("w_k", w_k), whole("w_v", w_v), whole("w_q", w_q), whole("w_o", w_o),
                halves("mlp_w1", mlp_w1), halves("mlp_w2", mlp_w2)]

    grads = assemble(0, unpack_small(small_mine), g_kv)
    deltas = assemble(1, unpack_small(small_d), kv_d[0])
    new_m = assemble(2, unpack_small(small_mn), kv_mn[0])
    new_v = assemble(3, unpack_small(small_vn), kv_vn[0])
    return (loss, grad_x, *grads, *deltas, *new_m, *new_v)
```

```python
import functools

import jax
import jax.numpy as jnp
from jax import lax
from jax.experimental import pallas as pl
from jax.experimental.pallas import tpu as pltpu

F32 = jnp.float32
BF16 = jnp.bfloat16
MESH = pl.DeviceIdType.MESH

N_DEV = 8
N_META = 16
BLK = 128
PAD = (-N_META) % BLK
HEAD_DIM = 64
DEPTH = 4
N_A = 2
RMS_EPS = 1e-6
ATTN_SCALE = HEAD_DIM ** -0.5
LOG_ZERO = -105.0
ADAM_LR, ADAM_B1, ADAM_B2, ADAM_EPS, ADAM_WD, ADAM_STEP = 0.001, 0.9, 0.999, 1e-08, 0.01, 10
VMEM_LIMIT = 56 * 2 ** 20

NT_DIMS = (((1,), (1,)), ((), ()))
TN_DIMS = (((0,), (0,)), ((), ()))
ANY = pl.BlockSpec(memory_space=pl.ANY)


def _params(*sem):
    return pltpu.CompilerParams(dimension_semantics=sem, vmem_limit_bytes=VMEM_LIMIT)


def _row_tile(rows, pref):
    best = None
    for t in range(16, min(rows, pref) + 1, 16):
        if rows % t == 0:
            best = t
    assert best is not None, (rows, pref)
    return best


def _row_chunks(rows, parts=4):
    if rows % (16 * parts):
        return [slice(0, rows)]
    step = rows // parts
    return [slice(r, r + step) for r in range(0, rows, step)]


def _mm_nn(name, a, w, layer, layout, out_dtypes, tm, tn=None, epi=None, extras=(), gather=None):
    M, K = a.shape
    if layout == "col":
        nb = w.shape[3]
        N, tn = N_DEV * nb, nb
        w_spec = pl.BlockSpec((None, None, K, nb), lambda i, j: (j, layer, 0, 0))
    else:
        kb, N = w.shape[2], w.shape[3]
        assert N_DEV * kb == K
        w_spec = pl.BlockSpec((N_DEV, None, kb, tn), lambda i, j: (0, layer, 0, j))
    ne, no = len(extras), len(out_dtypes)
    g_in = gather.operands() if gather else []
    ngi, ngo = len(g_in), (gather.n if gather else 0)
    nj = N // tn
    steps = (M // tm) * nj

    def body(a_ref, w_ref, *rest):
        outs = rest[ne + ngi:ne + ngi + no]
        if gather:
            start, forward, finish = gather.bind(rest[ne:ne + ngi], rest[ne + ngi + no:ne + ngi + no + ngo],
                                                 rest[ne + ngi + no + ngo:])
            step = pl.program_id(0) * nj + pl.program_id(1)
            pl.when(step == 0)(start)
        wv = w_ref[...]
        if layout == "row":
            wv = wv.reshape(K, tn)
        for rows in _row_chunks(tm):
            acc = jnp.dot(a_ref[rows, :], wv, preferred_element_type=F32)
            vals = epi(acc, *[r[rows, :] for r in rest[:ne]]) if epi else (acc,)
            for o_ref, val in zip(outs, vals):
                o_ref[rows, :] = val.astype(o_ref.dtype)
        if gather:
            @pl.when(step == steps - 1)
            def _():
                forward()
                finish()

    tile = pl.BlockSpec((tm, tn), lambda i, j: (i, j))
    return pl.pallas_call(
        body, name=name, grid=(M // tm, nj),
        in_specs=[pl.BlockSpec((tm, K), lambda i, j: (i, 0)), w_spec] + [tile] * ne + [ANY] * ngi,
        out_specs=[tile] * no + [ANY] * ngo,
        out_shape=[jax.ShapeDtypeStruct((M, N), d) for d in out_dtypes] + (gather.out_shapes() if gather else []),
        input_output_aliases=gather.aliases(2 + ne, no) if gather else {},
        scratch_shapes=gather.scratch() if gather else [],
        compiler_params=_params("arbitrary", "arbitrary") if gather else _params("parallel", "parallel"),
    )(a, w, *extras, *g_in)


def _mm_nt(name, a, w, layer, layout, out_dtypes, tm, tn=None, group=1, epi=None, extras=()):
    M, Nc = a.shape
    ne = len(extras)
    if layout == "row":
        kb = w.shape[2]
        No, tno = N_DEV * kb, group * kb
        w_spec = pl.BlockSpec((group, None, kb, Nc), lambda i, j: (j, layer, 0, 0))
        a_spec = pl.BlockSpec((tm, Nc), lambda i, j: (i, 0))
        grid = (M // tm, N_DEV // group)
        scratch = []

        def body(a_ref, w_ref, *rest):
            wv = w_ref[...].reshape(tno, Nc)
            for rows in _row_chunks(tm):
                acc = lax.dot_general(a_ref[rows, :], wv, NT_DIMS, preferred_element_type=F32)
                vals = epi(acc, *[r[rows, :] for r in rest[:ne]]) if epi else (acc,)
                for o_ref, val in zip(rest[ne:], vals):
                    o_ref[rows, :] = val.astype(o_ref.dtype)
    else:
        No, nb = w.shape[2], w.shape[3]
        assert N_DEV * nb == Nc
        tno = tn
        w_spec = pl.BlockSpec((N_DEV, None, tn, nb), lambda i, j: (0, layer, j, 0))
        a_spec = pl.BlockSpec((tm, Nc), lambda i, j: (i, 0))
        grid = (M // tm, No // tn)
        scratch = [pltpu.VMEM((tm, tn), F32)]

        def body(a_ref, w_ref, *rest):
            acc_ref = rest[-1]
            for d in range(N_DEV):
                part = lax.dot_general(a_ref[:, d * nb:(d + 1) * nb], w_ref[d], NT_DIMS,
                                       preferred_element_type=F32)
                if d == 0:
                    acc_ref[...] = part
                else:
                    acc_ref[...] += part
            acc = acc_ref[...]
            vals = epi(acc, *[r[...] for r in rest[:ne]]) if epi else (acc,)
            for o_ref, val in zip(rest[ne:-1], vals):
                o_ref[...] = val.astype(o_ref.dtype)

    tile = pl.BlockSpec((tm, tno), lambda i, j: (i, j))
    return pl.pallas_call(
        body, name=name, grid=grid,
        in_specs=[a_spec, w_spec] + [tile] * ne,
        out_specs=[tile] * len(out_dtypes),
        out_shape=[jax.ShapeDtypeStruct((M, No), d) for d in out_dtypes],
        scratch_shapes=scratch,
        compiler_params=_params("parallel", "parallel"),
    )(a, w, *extras)


def _mm_tn(name, a, g, layout, tl, group=1, layer=0, layers=1, into=None, wire=False):
    L, Ka = a.shape
    N = g.shape[1]
    nl = L // tl
    out_dtype = BF16 if wire else F32

    def refs(rest):
        return (rest[-2], rest[-1]) if wire else (rest[-1], rest[-1])

    def store(o_ref, acc_ref):
        if wire:
            @pl.when(pl.program_id(1) == nl - 1)
            def _():
                o_ref[...] = acc_ref[...].astype(BF16)
    if layout == "col":
        nb = N // N_DEV
        grid = (N_DEV // group, nl)
        in_specs = [pl.BlockSpec((tl, Ka), lambda j, k: (k, 0)), pl.BlockSpec((tl, group * nb), lambda j, k: (k, j))]
        out_spec = pl.BlockSpec((group, None, Ka, nb), lambda j, k: (j, layer, 0, 0))
        out_shape = jax.ShapeDtypeStruct((N_DEV, layers, Ka, nb), out_dtype)
        block = (group, Ka, nb)

        def body(a_ref, g_ref, *rest):
            o_ref, acc_ref = refs(rest)

            @pl.when(pl.program_id(1) == 0)
            def _():
                acc_ref[...] = jnp.zeros(block, F32)

            av = a_ref[...]
            for t in range(group):
                acc_ref[t] += lax.dot_general(av, g_ref[:, t * nb:(t + 1) * nb], TN_DIMS, preferred_element_type=F32)
            store(o_ref, acc_ref)
    else:
        kb = Ka // N_DEV
        grid = (N_DEV // group, nl)
        in_specs = [pl.BlockSpec((tl, group * kb), lambda j, k: (k, j)), pl.BlockSpec((tl, N), lambda j, k: (k, 0))]
        out_spec = pl.BlockSpec((group, None, kb, N), lambda j, k: (j, layer, 0, 0))
        out_shape = jax.ShapeDtypeStruct((N_DEV, layers, kb, N), out_dtype)
        block = (group, kb, N)

        def body(a_ref, g_ref, *rest):
            o_ref, acc_ref = refs(rest)

            @pl.when(pl.program_id(1) == 0)
            def _():
                acc_ref[...] = jnp.zeros(block, F32)

            gv = g_ref[...]
            if kb >= 2 * BLK:
                for t in range(group):
                    acc_ref[t] += lax.dot_general(a_ref[:, t * kb:(t + 1) * kb], gv, TN_DIMS,
                                                  preferred_element_type=F32)
            else:
                whole = lax.dot_general(a_ref[...], gv, TN_DIMS, preferred_element_type=F32)
                for t in range(group):
                    acc_ref[t] += whole[t * kb:(t + 1) * kb]
            store(o_ref, acc_ref)

    scratch = [pltpu.VMEM(block, F32)] if wire else []
    if into is None:
        return pl.pallas_call(
            body, name=name, grid=grid, in_specs=in_specs, out_specs=out_spec, out_shape=out_shape,
            scratch_shapes=scratch, compiler_params=_params("parallel", "arbitrary"),
        )(a, g)
    return pl.pallas_call(
        body, name=name, grid=grid, in_specs=in_specs + [ANY], out_specs=out_spec, out_shape=out_shape,
        scratch_shapes=scratch, input_output_aliases={2: 0}, compiler_params=_params("parallel", "arbitrary"),
    )(a, g, into)


def _rstd(x):
    return lax.rsqrt(jnp.mean(x * x, axis=-1, keepdims=True) + RMS_EPS)


def _norm_fwd(name, x, gain):
    L, D = x.shape
    tr = _row_tile(L, 640)

    def body(x_ref, g_ref, o_ref):
        xv = x_ref[...]
        o_ref[...] = (xv * _rstd(xv) * g_ref[...]).astype(BF16)

    return pl.pallas_call(
        body, name=name, grid=(L // tr,),
        in_specs=[pl.BlockSpec((tr, D), lambda i: (i, 0)), pl.BlockSpec((1, D), lambda i: (0, 0))],
        out_specs=pl.BlockSpec((tr, D), lambda i: (i, 0)),
        out_shape=jax.ShapeDtypeStruct((L, D), BF16),
        compiler_params=_params("parallel"),
    )(x, gain)


def _res_norm(name, h, branch, g_post, g_pre):
    L, D = h.shape
    tr = _row_tile(L, 640)
    npre = len(g_pre)

    def body(h_ref, b_ref, gp_ref, *rest):
        bv = b_ref[...]
        hn = h_ref[...] + bv * _rstd(bv) * gp_ref[...]
        rest[npre][...] = hn
        if npre:
            xh = hn * _rstd(hn)
            for t in range(npre):
                rest[npre + 1 + t][...] = (xh * rest[t][...]).astype(BF16)

    row = pl.BlockSpec((tr, D), lambda i: (i, 0))
    gain = pl.BlockSpec((1, D), lambda i: (0, 0))
    return pl.pallas_call(
        body, name=name, grid=(L // tr,),
        in_specs=[row, row, gain] + [gain] * npre,
        out_specs=[row] * (1 + npre),
        out_shape=[jax.ShapeDtypeStruct((L, D), F32)] + [jax.ShapeDtypeStruct((L, D), BF16)] * npre,
        compiler_params=_params("parallel"),
    )(h, branch, g_post, *g_pre)


def _norm_bwd(name, dres, stream=None, pre=(), post=None):
    L, D = dres.shape
    tr = _row_tile(L, 640)
    npre = len(pre)
    has_post = post is not None
    n_in = 1 + (1 + 2 * npre if npre else 0) + (2 if has_post else 0)
    n_dg = npre + (1 if has_post else 0)

    def body(*refs):
        ins, outs = refs[:n_in], refs[n_in:]
        i = pl.program_id(0)
        dsum = ins[0][...]
        dgs = []
        pos = 1
        if npre:
            xv = ins[pos][...]
            pos += 1
            rs = _rstd(xv)
            xh = xv * rs
            dxh = None
            for t in range(npre):
                gv, dy = ins[pos][...], ins[pos + 1][...]
                pos += 2
                dgs.append(jnp.sum(dy * xh, axis=0, keepdims=True))
                term = dy * gv
                dxh = term if dxh is None else dxh + term
            dsum = dsum + rs * (dxh - xh * jnp.mean(dxh * xh, axis=-1, keepdims=True))
        outs[0][...] = dsum
        o = 1
        if has_post:
            bv, gp = ins[pos][...], ins[pos + 1][...]
            rs = _rstd(bv)
            bh = bv * rs
            dgs.append(jnp.sum(dsum * bh, axis=0, keepdims=True))
            dbh = dsum * gp
            outs[1][...] = (rs * (dbh - bh * jnp.mean(dbh * bh, axis=-1, keepdims=True))).astype(BF16)
            o = 2

        @pl.when(i == 0)
        def _():
            for t in range(n_dg):
                outs[o + t][...] = jnp.zeros((8, D), F32)

        for t in range(n_dg):
            outs[o + t][0:1, :] += dgs[t]

    row = pl.BlockSpec((tr, D), lambda i: (i, 0))
    gain = pl.BlockSpec((1, D), lambda i: (0, 0))
    acc = pl.BlockSpec((8, D), lambda i: (0, 0))
    args, in_specs = [dres], [row]
    if npre:
        args.append(stream)
        in_specs.append(row)
        for gv, dy in pre:
            args += [gv, dy]
            in_specs += [gain, row]
    if has_post:
        args += [post[0], post[1]]
        in_specs += [row, gain]
    out_specs = [row] + ([row] if has_post else []) + [acc] * n_dg
    out_shape = ([jax.ShapeDtypeStruct((L, D), F32)] + ([jax.ShapeDtypeStruct((L, D), BF16)] if has_post else [])
                 + [jax.ShapeDtypeStruct((8, D), F32)] * n_dg)
    return pl.pallas_call(
        body, name=name, grid=(L // tr,), in_specs=in_specs, out_specs=out_specs, out_shape=out_shape,
        compiler_params=_params("arbitrary"),
    )(*args)


def _conv_rows(L):
    return max(t for t in range(BLK, 5 * BLK + 1, BLK) if L % t == 0)


def _conv_fwd(name, p, taps):
    L, D3 = p.shape
    D = D3 // 3
    tr = _conv_rows(L)

    def body(p_ref, prev_ref, w_ref, y_ref, u_scr):
        i = pl.program_id(0)
        for lo in range(0, D, BLK):
            cb, cc, cx = slice(lo, lo + BLK), slice(D + lo, D + lo + BLK), slice(2 * D + lo, 2 * D + lo + BLK)
            u_scr[0:8, :] = jnp.where(i > 0, prev_ref[:, cc] * prev_ref[:, cx], 0.0)
            for r0 in range(0, tr, BLK):
                u_scr[8 + r0:8 + r0 + BLK, :] = p_ref[r0:r0 + BLK, cc] * p_ref[r0:r0 + BLK, cx]
            for r0 in range(0, tr, BLK):
                conv = (w_ref[0:1, cb] * u_scr[6 + r0:6 + r0 + BLK, :] + w_ref[1:2, cb] * u_scr[7 + r0:7 + r0 + BLK, :]
                        + w_ref[2:3, cb] * u_scr[8 + r0:8 + r0 + BLK, :])
                y_ref[r0:r0 + BLK, cb] = (p_ref[r0:r0 + BLK, cb] * conv).astype(BF16)

    return pl.pallas_call(
        body, name=name, grid=(L // tr,),
        in_specs=[pl.BlockSpec((tr, D3), lambda i: (i, 0)),
                  pl.BlockSpec((8, D3), lambda i: (jnp.maximum(i * (tr // 8) - 1, 0), 0)),
                  pl.BlockSpec((8, D), lambda i: (0, 0))],
        out_specs=pl.BlockSpec((tr, D), lambda i: (i, 0)),
        out_shape=jax.ShapeDtypeStruct((L, D), BF16),
        scratch_shapes=[pltpu.VMEM((tr + 8, BLK), F32)],
        compiler_params=_params("parallel"),
    )(p, p, taps)


def _conv_bwd(name, p, dy, taps):
    L, D3 = p.shape
    D = D3 // 3
    tr = _conv_rows(L)
    nblk = L // tr
    last8 = L // 8 - 1

    def body(p_ref, prev_ref, next_ref, dy_ref, dyn_ref, w_ref, dp_ref, dw_ref, u_scr, d_scr):
        i = pl.program_id(0)

        @pl.when(i == 0)
        def _():
            dw_ref[...] = jnp.zeros((8, D), F32)

        for lo in range(0, D, BLK):
            cb, cc, cx = slice(lo, lo + BLK), slice(D + lo, D + lo + BLK), slice(2 * D + lo, 2 * D + lo + BLK)
            w0, w1, w2 = w_ref[0:1, cb], w_ref[1:2, cb], w_ref[2:3, cb]
            u_scr[0:8, :] = jnp.where(i > 0, prev_ref[:, cc] * prev_ref[:, cx], 0.0)
            d_scr[tr:tr + 8, :] = jnp.where(i < nblk - 1, dyn_ref[:, cb] * next_ref[:, cb], 0.0)
            for r0 in range(0, tr, BLK):
                rows = slice(r0, r0 + BLK)
                u_scr[8 + r0:8 + r0 + BLK, :] = p_ref[rows, cc] * p_ref[rows, cx]
                d_scr[rows, :] = dy_ref[rows, cb] * p_ref[rows, cb]
            dws = [None, None, None]
            for r0 in range(0, tr, BLK):
                rows = slice(r0, r0 + BLK)
                us = [u_scr[6 + r0:6 + r0 + BLK, :], u_scr[7 + r0:7 + r0 + BLK, :], u_scr[8 + r0:8 + r0 + BLK, :]]
                conv = w0 * us[0] + w1 * us[1] + w2 * us[2]
                dconv = d_scr[rows, :]
                du = w2 * dconv + w1 * d_scr[1 + r0:1 + r0 + BLK, :] + w0 * d_scr[2 + r0:2 + r0 + BLK, :]
                dp_ref[rows, cb] = (dy_ref[rows, cb] * conv).astype(BF16)
                dp_ref[rows, cc] = (du * p_ref[rows, cx]).astype(BF16)
                dp_ref[rows, cx] = (du * p_ref[rows, cc]).astype(BF16)
                for t in range(3):
                    part = jnp.sum(dconv * us[t], axis=0, keepdims=True)
                    dws[t] = part if dws[t] is None else dws[t] + part
            for t in range(3):
                dw_ref[t:t + 1, cb] += dws[t]

    halo_prev = lambda i: (jnp.maximum(i * (tr // 8) - 1, 0), 0)
    halo_next = lambda i: (jnp.minimum((i + 1) * (tr // 8), last8), 0)
    return pl.pallas_call(
        body, name=name, grid=(nblk,),
        in_specs=[pl.BlockSpec((tr, D3), lambda i: (i, 0)), pl.BlockSpec((8, D3), halo_prev),
                  pl.BlockSpec((8, D3), halo_next), pl.BlockSpec((tr, D), lambda i: (i, 0)),
                  pl.BlockSpec((8, D), halo_next), pl.BlockSpec((8, D), lambda i: (0, 0))],
        out_specs=[pl.BlockSpec((tr, D3), lambda i: (i, 0)), pl.BlockSpec((8, D), lambda i: (0, 0))],
        out_shape=[jax.ShapeDtypeStruct((L, D3), BF16), jax.ShapeDtypeStruct((8, D), F32)],
        scratch_shapes=[pltpu.VMEM((tr + 8, BLK), F32), pltpu.VMEM((tr + 8, BLK), F32)],
        compiler_params=_params("arbitrary"),
    )(p, p, p, dy, dy, taps)


KB_STEP = 3
FAR = 1 << 30


def _stack_heads(x):
    lane = lax.broadcasted_iota(jnp.int32, x.shape, 1)
    zero = jnp.zeros_like(x)
    return jnp.concatenate([jnp.where(lane < HEAD_DIM, x, zero), jnp.where(lane >= HEAD_DIM, x, zero)], axis=0)


def _unstack_heads(x2):
    lane = lax.broadcasted_iota(jnp.int32, (BLK, BLK), 1)
    return jnp.where(lane < HEAD_DIM, x2[0:BLK], x2[BLK:2 * BLK])


def _key_block(kb, u):
    kbu = kb - u
    off = pl.multiple_of(jnp.maximum(kbu, 0) * BLK, BLK)
    return off, jnp.where(kbu >= 0, kbu * BLK, FAR)


def _attn_iotas(i):
    lane = lax.broadcasted_iota(jnp.int32, (2 * BLK, BLK), 1)
    row = lax.broadcasted_iota(jnp.int32, (2 * BLK, BLK), 0)
    krow = lax.broadcasted_iota(jnp.int32, (BLK, BLK), 0)
    klane = lax.broadcasted_iota(jnp.int32, (BLK, BLK), 1)
    return lane, i * BLK + (row & (BLK - 1)), krow, klane


def _score_block(q2, kblk, kbase, qpos, lane, rr, tri):
    z = lax.dot_general(q2, kblk, NT_DIMS, preferred_element_type=F32)
    kpos = kbase + lane
    vis = (kpos < qpos) & (kpos >= PAD)
    sp = jnp.log(1.0 + jnp.exp(-jnp.abs(z)))
    lb = jnp.minimum(z, 0.0) - sp
    lom = jnp.where(vis, lb - z, 0.0)
    hi = lom.astype(BF16)
    lo = (lom - hi.astype(F32)).astype(BF16)
    cs = jnp.dot(hi, tri, preferred_element_type=F32) + jnp.dot(lo, tri, preferred_element_type=F32)
    a = jnp.where(vis, jnp.exp(lb + rr + cs), 0.0)
    return vis, lb, lom, a


def _attn_fwd(name, q, k, v, gather=None):
    L, D = q.shape
    nh, nq = D // BLK, L // BLK
    g_in = gather.operands() if gather else []
    ngi, ngo = len(g_in), (gather.n if gather else 0)

    def body(q_ref, k_ref, v_ref, *rest):
        of_ref, ob_ref, aw_ref, bw_ref, fl_ref = rest[ngi:ngi + 5]
        acc_ref, r_ref = rest[ngi + 5 + ngo:ngi + 7 + ngo]
        i = pl.program_id(1)
        if gather:
            start, forward, finish = gather.bind(rest[:ngi], rest[ngi + 5:ngi + 5 + ngo], rest[ngi + 7 + ngo:])
            grid_step = pl.program_id(0) * nq + i
            pl.when(grid_step == 0)(start)
            pl.when(grid_step == nh * nq // 2)(forward)
        lane, qpos, krow, klane = _attn_iotas(i)
        tri = (krow > klane).astype(BF16)
        q2 = _stack_heads(q_ref[...])
        acc_ref[...] = jnp.zeros((2 * BLK, BLK), F32)
        r_ref[...] = jnp.zeros((2 * BLK, 1), F32)

        def blocks(kb, save):
            rr = r_ref[...]
            upd = None
            for u in range(KB_STEP):
                off, kbase = _key_block(kb, u)
                vis, lb, lom, a = _score_block(q2, k_ref[pl.ds(off, BLK), :], kbase, qpos, lane, rr, tri)
                ab = a.astype(BF16)
                if save:
                    aw_ref[:, u * BLK:(u + 1) * BLK] = ab
                    bw_ref[:, u * BLK:(u + 1) * BLK] = jnp.where(vis, jnp.exp(lb), 0.0).astype(BF16)
                part = jnp.dot(ab, v_ref[pl.ds(off, BLK), :], preferred_element_type=F32)
                upd = part if upd is None else upd + part
                rr = rr + jnp.sum(lom, axis=1, keepdims=True)
            acc_ref[...] += upd
            r_ref[...] = rr
            return (jnp.max(rr) < LOG_ZERO).astype(jnp.int32)

        done = blocks(i, True)
        kb_end, _ = lax.while_loop(lambda c: (c[0] >= 0) & (c[1] == 0),
                                   lambda c: (c[0] - KB_STEP, blocks(c[0], False)), (i - KB_STEP, done))
        fl_ref[...] = jnp.zeros((8, BLK), F32) + (kb_end == i - KB_STEP).astype(F32)
        o = _unstack_heads(acc_ref[...])
        of_ref[...] = o
        ob_ref[...] = o.astype(BF16)
        if gather:
            pl.when(grid_step == nh * nq - 1)(finish)

    qspec = pl.BlockSpec((BLK, BLK), lambda h, i: (i, h))
    kvspec = pl.BlockSpec((L, BLK), lambda h, i: (0, h))
    wspec = pl.BlockSpec((None, None, 2 * BLK, KB_STEP * BLK), lambda h, i: (h, i, 0, 0))
    fspec = pl.BlockSpec((None, None, 8, BLK), lambda h, i: (h, i, 0, 0))
    return pl.pallas_call(
        body, name=name, grid=(nh, nq),
        in_specs=[qspec, kvspec, kvspec] + [ANY] * ngi, out_specs=[qspec, qspec, wspec, wspec, fspec] + [ANY] * ngo,
        out_shape=[jax.ShapeDtypeStruct((L, D), F32), jax.ShapeDtypeStruct((L, D), BF16),
                   jax.ShapeDtypeStruct((nh, nq, 2 * BLK, KB_STEP * BLK), BF16),
                   jax.ShapeDtypeStruct((nh, nq, 2 * BLK, KB_STEP * BLK), BF16),
                   jax.ShapeDtypeStruct((nh, nq, 8, BLK), F32)] + (gather.out_shapes() if gather else []),
        input_output_aliases=gather.aliases(3, 5) if gather else {},
        scratch_shapes=[pltpu.VMEM((2 * BLK, BLK), F32), pltpu.VMEM((2 * BLK, 1), F32)]
        + (gather.scratch() if gather else []),
        compiler_params=_params("arbitrary", "arbitrary") if gather else _params("parallel", "arbitrary"),
    )(q, k, v, *g_in)


def _attn_bwd(name, q, k, v, o, do, a_win, b_win, flag, side=None):
    L, D = q.shape
    nh, nq = D // BLK, L // BLK
    ns = side.n if side else 0

    def body(q_ref, k_ref, v_ref, o_ref, do_ref, aw_ref, bw_ref, fl_ref, *rest):
        dq_ref, dk_ref, dv_ref = rest[ns:ns + 3]
        acc_ref, r_ref, c_ref = rest[2 * ns + 3:2 * ns + 6]
        i = pl.program_id(1)
        if side:
            start, finish = side.bind(rest[:ns], rest[ns + 3:2 * ns + 3], rest[2 * ns + 6:])
            grid_step = pl.program_id(0) * nq + i
            pl.when(grid_step == 0)(start)

        @pl.when(i == 0)
        def _():
            dk_ref[...] = jnp.zeros((L, BLK), F32)
            dv_ref[...] = jnp.zeros((L, BLK), F32)

        krow = lax.broadcasted_iota(jnp.int32, (BLK, BLK), 0)
        klane = lax.broadcasted_iota(jnp.int32, (BLK, BLK), 1)
        tri_incl = (krow >= klane).astype(BF16)
        q2 = q_all = _stack_heads(q_ref[...])
        do_all = _stack_heads(do_ref[...].astype(BF16))
        total_all = jnp.sum(do_all.astype(F32) * jnp.concatenate([o_ref[...]] * 2, axis=0), axis=1, keepdims=True)

        def grads(off, kblk, ab, beta, cc, rows=slice(None)):
            do2, q2, total = do_all[rows], q_all[rows], total_all[rows]
            da = lax.dot_general(do2, v_ref[pl.ds(off, BLK), :], NT_DIMS, preferred_element_type=F32)
            g = da * ab.astype(F32)
            ghi = g.astype(BF16)
            glo = (g - ghi.astype(F32)).astype(BF16)
            sfx = (jnp.dot(ghi, tri_incl, preferred_element_type=F32)
                   + jnp.dot(glo, tri_incl, preferred_element_type=F32))
            before = total - cc - sfx
            dz = (g * (1.0 - beta) - beta * before).astype(BF16)
            dk_ref[pl.ds(off, BLK), :] += lax.dot_general(dz, q2, TN_DIMS, preferred_element_type=F32)
            dv_ref[pl.ds(off, BLK), :] += lax.dot_general(ab, do2, TN_DIMS, preferred_element_type=F32)
            return jnp.dot(dz, kblk, preferred_element_type=F32), cc + jnp.sum(g, axis=1, keepdims=True)

        fast = jnp.max(fl_ref[...]) > 0.5

        @pl.when(fast)
        def _():
            heads = []
            for hh in range(2):
                rows = slice(hh * BLK, (hh + 1) * BLK)
                cc = jnp.zeros((BLK, 1), F32)
                upd = None
                for u in range(KB_STEP):
                    off, _ = _key_block(i, u)
                    cols = slice(u * BLK, (u + 1) * BLK)
                    part, cc = grads(off, k_ref[pl.ds(off, BLK), :], aw_ref[rows, cols],
                                     bw_ref[rows, cols].astype(F32), cc, rows)
                    upd = part if upd is None else upd + part
                heads.append(upd)
            dq_ref[...] = (_unstack_heads(jnp.concatenate(heads, axis=0)) * ATTN_SCALE).astype(BF16)

        @pl.when(jnp.logical_not(fast))
        def _():
            lane, qpos, _, _ = _attn_iotas(i)
            tri = (krow > klane).astype(BF16)
            acc_ref[...] = jnp.zeros((2 * BLK, BLK), F32)
            r_ref[...] = jnp.zeros((2 * BLK, 1), F32)
            c_ref[...] = jnp.zeros((2 * BLK, 1), F32)

            def step(carry):
                kb, _ = carry
                rr = r_ref[...]
                cc = c_ref[...]
                upd = None
                for u in range(KB_STEP):
                    off, kbase = _key_block(kb, u)
                    kblk = k_ref[pl.ds(off, BLK), :]
                    vis, lb, lom, a = _score_block(q2, kblk, kbase, qpos, lane, rr, tri)
                    part, cc = grads(off, kblk, a.astype(BF16), jnp.where(vis, jnp.exp(lb), 0.0), cc)
                    upd = part if upd is None else upd + part
                    rr = rr + jnp.sum(lom, axis=1, keepdims=True)
                acc_ref[...] += upd
                c_ref[...] = cc
                r_ref[...] = rr
                return kb - KB_STEP, (jnp.max(rr) < LOG_ZERO).astype(jnp.int32)

            lax.while_loop(lambda c: (c[0] >= 0) & (c[1] == 0), step, (i, jnp.int32(0)))
            dq_ref[...] = (_unstack_heads(acc_ref[...]) * ATTN_SCALE).astype(BF16)

        if side:
            pl.when(grid_step == nh * nq - 1)(finish)

    qspec = pl.BlockSpec((BLK, BLK), lambda h, i: (i, h))
    kvspec = pl.BlockSpec((L, BLK), lambda h, i: (0, h))
    wspec = pl.BlockSpec((None, None, 2 * BLK, KB_STEP * BLK), lambda h, i: (h, i, 0, 0))
    fspec = pl.BlockSpec((None, None, 8, BLK), lambda h, i: (h, i, 0, 0))
    return pl.pallas_call(
        body, name=name, grid=(nh, nq),
        in_specs=[qspec, kvspec, kvspec, qspec, qspec, wspec, wspec, fspec] + [ANY] * ns,
        out_specs=[qspec, kvspec, kvspec] + [ANY] * ns,
        out_shape=[jax.ShapeDtypeStruct((L, D), BF16), jax.ShapeDtypeStruct((L, D), F32),
                   jax.ShapeDtypeStruct((L, D), F32)] + (side.out_shapes() if side else []),
        scratch_shapes=[pltpu.VMEM((2 * BLK, BLK), F32), pltpu.VMEM((2 * BLK, 1), F32),
                        pltpu.VMEM((2 * BLK, 1), F32)] + (side.scratch() if side else []),
        compiler_params=_params("arbitrary", "arbitrary") if side else _params("parallel", "arbitrary"),
    )(q, k, v, o, do, a_win, b_win, flag, *(side.operands() if side else []))


def _loss_and_grad(name, h, target):
    L, D = h.shape
    first = (PAD + N_META) // BLK

    def body(h_ref, t_ref, sq_ref, dh_ref):
        i = pl.program_id(0)

        @pl.when(i == 0)
        def _():
            sq_ref[...] = jnp.zeros((8, BLK), F32)

        @pl.when(i < first)
        def _():
            dh_ref[...] = jnp.zeros((BLK, D), F32)

        @pl.when(i >= first)
        def _():
            err = h_ref[...] - t_ref[...]
            sq_ref[...] += jnp.sum(err * err)
            dh_ref[...] = err * (1.0 / D)

    return pl.pallas_call(
        body, name=name, grid=(L // BLK,),
        in_specs=[pl.BlockSpec((BLK, D), lambda i: (i, 0)),
                  pl.BlockSpec((BLK, D), lambda i: (jnp.maximum(i - first, 0), 0))],
        out_specs=[pl.BlockSpec((8, BLK), lambda i: (0, 0)), pl.BlockSpec((BLK, D), lambda i: (i, 0))],
        out_shape=[jax.ShapeDtypeStruct((8, BLK), F32), jax.ShapeDtypeStruct((L, D), F32)],
        compiler_params=_params("arbitrary"),
    )(h, target)


def _add_cast(name, a, b):
    L, D = a.shape
    tr = _row_tile(L, 640)

    def body(a_ref, b_ref, o_ref):
        o_ref[...] = (a_ref[...] + b_ref[...]).astype(BF16)

    row = pl.BlockSpec((tr, D), lambda i: (i, 0))
    return pl.pallas_call(body, name=name, grid=(L // tr,), in_specs=[row, row], out_specs=row,
                          out_shape=jax.ShapeDtypeStruct((L, D), BF16), compiler_params=_params("parallel"))(a, b)


def _adamw(w, g, m, v):
    m = ADAM_B1 * m + (1.0 - ADAM_B1) * g
    v = ADAM_B2 * v + (1.0 - ADAM_B2) * (g * g)
    m_hat = m / (1.0 - ADAM_B1 ** ADAM_STEP)
    v_hat = v / (1.0 - ADAM_B2 ** ADAM_STEP)
    delta = -ADAM_LR * (m_hat / (jnp.sqrt(v_hat) + ADAM_EPS) + ADAM_WD * w)
    return delta, m, v


def _adam_small(name, w, g, m, v):
    def body(w_ref, g_ref, m_ref, v_ref, d_ref, mo_ref, vo_ref):
        d, mn, vn = _adamw(w_ref[...], g_ref[...], m_ref[...], v_ref[...])
        d_ref[...] = d
        mo_ref[...] = mn
        vo_ref[...] = vn

    return pl.pallas_call(body, name=name, out_shape=[jax.ShapeDtypeStruct(w.shape, F32)] * 3)(w, g, m, v)


def _sum_slots(name, slots):
    def body(s_ref, o_ref):
        acc = s_ref[0]
        for d in range(1, N_DEV):
            acc = acc + s_ref[d]
        o_ref[...] = acc

    return pl.pallas_call(body, name=name, out_shape=jax.ShapeDtypeStruct(slots.shape[1:], F32))(slots)


def _sum8_adam(name, g, recv, order, w, m, v):
    R, C = w.shape
    tr = _row_tile(R, 512)

    def body(order_ref, g_ref, *rest):
        w_ref, m_ref, v_ref, go_ref, d_ref, mo_ref, vo_ref = rest[N_DEV - 1:]
        gsum = g_ref[...].astype(F32)
        for r_ref in rest[:N_DEV - 1]:
            gsum = gsum + r_ref[...].astype(F32)
        d, mn, vn = _adamw(w_ref[...], gsum, m_ref[...], v_ref[...])
        go_ref[...] = gsum
        d_ref[...] = d
        mo_ref[...] = mn
        vo_ref[...] = vn

    row = pl.BlockSpec((tr, C), lambda i, order: (i, 0))
    slot = lambda r: pl.BlockSpec((None, tr, C), lambda i, order: (order[r], i, 0))
    return pl.pallas_call(
        body, name=name,
        grid_spec=pltpu.PrefetchScalarGridSpec(
            num_scalar_prefetch=1, grid=(R // tr,),
            in_specs=[slot(r) for r in range(N_DEV)] + [row, row, row], out_specs=[row] * 4),
        out_shape=[jax.ShapeDtypeStruct((R, C), F32)] * 4,
        compiler_params=_params("parallel"),
    )(order, g, *([recv] * (N_DEV - 1)), w, m, v)


def _pair_sum(name, g, recv, c_idx):
    _, R, C = g.shape
    tr = _row_tile(R, 512)

    def body(sel_ref, g_ref, r_ref, o_ref):
        o_ref[...] = (g_ref[...].astype(F32) + r_ref[...].astype(F32)).astype(BF16)

    return pl.pallas_call(
        body, name=name,
        grid_spec=pltpu.PrefetchScalarGridSpec(
            num_scalar_prefetch=1, grid=(4, R // tr),
            in_specs=[pl.BlockSpec((None, tr, C), lambda k, i, sel: (2 * k + sel[0], i, 0)),
                      pl.BlockSpec((None, tr, C), lambda k, i, sel: (k, i, 0))],
            out_specs=pl.BlockSpec((None, tr, C), lambda k, i, sel: (k, i, 0))),
        out_shape=jax.ShapeDtypeStruct((4, R, C), BF16),
        compiler_params=_params("parallel", "parallel"),
    )(c_idx, g, recv)


def _final_sum_adam(name, part, recv, chip_idx, w, m, v):
    R, C = w.shape
    tr = _row_tile(R, 512)

    def body(sel_ref, p_ref, r_ref, w_ref, m_ref, v_ref, g_ref, d_ref, mo_ref, vo_ref):
        g = ((p_ref[...].astype(F32) + r_ref[0].astype(F32)) + r_ref[1].astype(F32)) + r_ref[2].astype(F32)
        d, mn, vn = _adamw(w_ref[...], g, m_ref[...], v_ref[...])
        g_ref[...] = g
        d_ref[...] = d
        mo_ref[...] = mn
        vo_ref[...] = vn

    row = pl.BlockSpec((tr, C), lambda i, sel: (i, 0))
    return pl.pallas_call(
        body, name=name,
        grid_spec=pltpu.PrefetchScalarGridSpec(
            num_scalar_prefetch=1, grid=(R // tr,),
            in_specs=[pl.BlockSpec((None, tr, C), lambda i, sel: (sel[0], i, 0)),
                      pl.BlockSpec((3, tr, C), lambda i, sel: (0, i, 0)), row, row, row],
            out_specs=[row] * 4),
        out_shape=[jax.ShapeDtypeStruct((R, C), F32)] * 4,
        compiler_params=_params("parallel"),
    )(chip_idx, part, recv, w, m, v)


def _position():
    return lax.axis_index("x"), lax.axis_index("y"), lax.axis_index("c")


class _Gather:
    def __init__(self, items):
        self.items = items
        self.n = len(items)
        self.filled = [a for a, it in enumerate(items) if it[2] is not None]

    def operands(self):
        return [it[0] for it in self.items] + [self.items[a][2] for a in self.filled]

    def out_shapes(self):
        return [jax.ShapeDtypeStruct((N_DEV,) + it[0].shape, it[0].dtype) for it in self.items]

    def aliases(self, first_in, first_out):
        return {first_in + self.n + k: first_out + a for k, a in enumerate(self.filled)}

    def scratch(self):
        return [pltpu.SemaphoreType.DMA((self.n, 7)), pltpu.SemaphoreType.DMA((self.n, 7)),
                pltpu.SemaphoreType.DMA((self.n,))]

    def bind(self, in_refs, out_refs, sems):
        send_sems, recv_sems, local_sems = sems
        n, items = self.n, self.items
        x, y, c = _position()
        me, sibling = (x, y, c), (x, y, 1 - c)
        chips = [(1 - x, y), (x, 1 - y), (1 - x, 1 - y)]
        src = [in_refs[a].at[items[a][1]] for a in range(n)]

        def copy(a, sem, block, to, own=False):
            dst = out_refs[a].at[4 * block[0] + 2 * block[1] + block[2], items[a][1]]
            return pltpu.make_async_remote_copy(
                src_ref=src[a] if own else dst, dst_ref=dst,
                send_sem=send_sems.at[a, sem], recv_sem=recv_sems.at[a, sem], device_id=to, device_id_type=MESH)

        def local(a):
            return pltpu.make_async_copy(src[a], out_refs[a].at[4 * x + 2 * y + c, items[a][1]], local_sems.at[a])

        def first():
            return [cp for a in range(n) for cp in
                    [copy(a, 0, me, sibling, own=True)]
                    + [copy(a, 1 + j, me, (*chip, c), own=True) for j, chip in enumerate(chips)]]

        def start():
            for a in range(n):
                local(a).start()
            for cp in first():
                cp.start()

        def forward():
            for j, chip in enumerate(chips):
                for a in range(n):
                    copy(a, 1 + j, (*chip, c), me).wait_recv()
                    copy(a, 4 + j, (*chip, c), sibling).start()

        def finish():
            for a in range(n):
                copy(a, 0, sibling, me).wait_recv()
            for j, chip in enumerate(chips):
                for a in range(n):
                    copy(a, 4 + j, (*chip, 1 - c), me).wait_recv()
            for cp in first():
                cp.wait_send()
            for j, chip in enumerate(chips):
                for a in range(n):
                    copy(a, 4 + j, (*chip, c), sibling).wait_send()
            for a in range(n):
                local(a).wait()

        return start, forward, finish


def _all_gather(name, gather):
    n_in = len(gather.operands())

    def body(*refs):
        start, forward, finish = gather.bind(refs[:n_in], refs[n_in:n_in + gather.n], refs[n_in + gather.n:])
        start()
        forward()
        finish()

    return pl.pallas_call(
        body, name=name, in_specs=[ANY] * n_in, out_specs=[ANY] * gather.n, out_shape=gather.out_shapes(),
        input_output_aliases=gather.aliases(0, 0), scratch_shapes=gather.scratch(),
        compiler_params=pltpu.CompilerParams(has_side_effects=True),
    )(*gather.operands())


class _Direct:
    def __init__(self, items):
        self.items = items
        self.n = len(items)

    def operands(self):
        return list(self.items)

    def out_shapes(self):
        return [jax.ShapeDtypeStruct(it.shape, it.dtype) for it in self.items]

    def scratch(self):
        return [pltpu.SemaphoreType.DMA((self.n, 7)), pltpu.SemaphoreType.DMA((self.n, 7))]

    def bind(self, in_refs, out_refs, sems):
        send_sems, recv_sems = sems
        x, y, c = _position()

        def copies():
            out = []
            for a in range(self.n):
                for r in range(1, N_DEV):
                    peer = (1 - x if r & 4 else x, 1 - y if r & 2 else y, 1 - c if r & 1 else c)
                    out.append(pltpu.make_async_remote_copy(
                        src_ref=in_refs[a].at[4 * peer[0] + 2 * peer[1] + peer[2]],
                        dst_ref=out_refs[a].at[4 * x + 2 * y + c],
                        send_sem=send_sems.at[a, r - 1], recv_sem=recv_sems.at[a, r - 1],
                        device_id=peer, device_id_type=MESH))
            return out

        def start():
            for cp in copies():
                cp.start()

        def finish():
            for cp in copies():
                cp.wait()

        return start, finish


def _exchange_sibling(name, grads):
    n = len(grads)

    def body(*refs):
        g_refs, r_refs = refs[:n], refs[n:2 * n]
        send_sems, recv_sems = refs[2 * n:]
        x, y, c = _position()
        copies = []
        for a in range(n):
            for k in range(4):
                copies.append(pltpu.make_async_remote_copy(
                    src_ref=g_refs[a].at[2 * k + (1 - c)], dst_ref=r_refs[a].at[k],
                    send_sem=send_sems.at[a, k], recv_sem=recv_sems.at[a, k],
                    device_id=(x, y, 1 - c), device_id_type=MESH))
        for cp in copies:
            cp.start()
        for cp in copies:
            cp.wait()

    return pl.pallas_call(
        body, name=name, in_specs=[ANY] * n, out_specs=[ANY] * n,
        out_shape=[jax.ShapeDtypeStruct((4,) + g.shape[1:], g.dtype) for g in grads],
        scratch_shapes=[pltpu.SemaphoreType.DMA((n, 4)), pltpu.SemaphoreType.DMA((n, 4))],
        compiler_params=pltpu.CompilerParams(has_side_effects=True),
    )(*grads)


def _exchange_chips(name, parts):
    n = len(parts)

    def body(*refs):
        p_refs, r_refs = refs[:n], refs[n:2 * n]
        send_sems, recv_sems = refs[2 * n:]
        x, y, c = _position()
        chips = [(1 - x, y), (x, 1 - y), (1 - x, 1 - y)]
        copies = []
        for a in range(n):
            for j, chip in enumerate(chips):
                copies.append(pltpu.make_async_remote_copy(
                    src_ref=p_refs[a].at[2 * chip[0] + chip[1]], dst_ref=r_refs[a].at[j],
                    send_sem=send_sems.at[a, j], recv_sem=recv_sems.at[a, j],
                    device_id=(*chip, c), device_id_type=MESH))
        for cp in copies:
            cp.start()
        for cp in copies:
            cp.wait()

    return pl.pallas_call(
        body, name=name, in_specs=[ANY] * n, out_specs=[ANY] * n,
        out_shape=[jax.ShapeDtypeStruct((3,) + p.shape[1:], p.dtype) for p in parts],
        scratch_shapes=[pltpu.SemaphoreType.DMA((n, 3)), pltpu.SemaphoreType.DMA((n, 3))],
        compiler_params=pltpu.CompilerParams(has_side_effects=True),
    )(*parts)


def kernel(x, meta_tokens, norm_gains, conv_in_proj, conv_w, conv_out_proj, kv_norm, w_k, w_v, w_q, w_o, mlp_w1, mlp_w2, loss_target, m_meta_tokens, m_norm_gains, m_conv_in_proj, m_conv_w, m_conv_out_proj, m_kv_norm, m_w_k, m_w_v, m_w_q, m_w_o, m_mlp_w1, m_mlp_w2, v_meta_tokens, v_norm_gains, v_conv_in_proj, v_conv_w, v_conv_out_proj, v_kv_norm, v_w_k, v_w_v, v_w_q, v_w_o, v_mlp_w1, v_mlp_w2):
    xi, target = x[0], loss_target[0]
    S, D = xi.shape
    L = PAD + N_META + S
    dsh = D // N_DEV
    px, py, pc = _position()
    dev = 4 * px + 2 * py + pc
    tm_big = _row_tile(L, 1664)
    tm_mid = _row_tile(L, 640)

    def pack_small(meta, gains, taps):
        return jnp.concatenate([meta, gains.reshape(DEPTH * 4, dsh), taps.reshape(N_A * 3, dsh),
                                jnp.zeros((2, dsh), F32)], axis=0)

    big_w = [conv_in_proj, conv_out_proj, w_k[None], w_v[None], w_q, w_o, mlp_w1, mlp_w2]
    small_w = pack_small(meta_tokens, norm_gains, conv_w)
    w_names = ["win", "wout", "wk", "wv", "wq", "wo", "w1", "w2"]
    wb = {nm: w.astype(BF16) for nm, w in zip(w_names, big_w)}
    gw = {nm: None for nm in w_names}
    rides = {"conv_in0": [("wout", 0), ("w1", 0)], "mlp_up0": [("w2", 0), ("win", 1)],
             "mlp_down0": [("w1", 1), ("wout", 1)], "conv_in1": [("w2", 1)],
             "mlp_up1": [("wq", 0), ("wk", 0), ("wv", 0), ("w2", 2)], "mlp_down1": [("w1", 2), ("wo", 0)],
             "attn0": [("wq", 1), ("wo", 1), ("w1", 3), ("w2", 3)]}

    def gather_of(parts):
        return _Gather([(wb[nm], ly, gw[nm]) for nm, ly in parts]) if parts else None

    def hosting(call, name, *args, **kw):
        parts = rides.get(name, [])
        got = call(name, *args, gather=gather_of(parts), **kw)
        for (nm, _), arr in zip(parts, got[len(got) - len(parts):]):
            gw[nm] = arr
        return got[:len(got) - len(parts)]

    parts0 = [("win", 0)]
    got = _all_gather("gather_first", _Gather([(small_w[None], 0, None)] + gather_of(parts0).items))
    small_full = got[0][:, 0].transpose(1, 0, 2).reshape(40, D)
    for (nm, _), arr in zip(parts0, got[1:]):
        gw[nm] = arr
    meta_full = small_full[0:N_META]
    gain = lambda layer, n: small_full[N_META + 4 * layer + n][None]
    taps = [jnp.concatenate([small_full[32 + 3 * l:35 + 3 * l], jnp.zeros((5, D), F32)], axis=0) for l in range(N_A)]
    kvn = kv_norm[None]

    h = jnp.concatenate([jnp.zeros((PAD, D), F32), meta_full, xi], axis=0)
    n1 = _norm_fwd("norm_in", h, gain(0, 0))
    saved = []
    hk = k = v = None
    for layer in range(DEPTH):
        s = {"h0": h, "n1": n1}
        if layer < N_A:
            s["p"] = hosting(_mm_nn, f"conv_in{layer}", n1, gw["win"], layer, "col", [F32], tm_big)[0]
            s["y"] = _conv_fwd(f"conv{layer}", s["p"], taps[layer])
            mix = _mm_nn(f"conv_out{layer}", s["y"], gw["wout"], layer, "row", [F32], tm_big, tn=D)[0]
        else:
            j = layer - N_A
            if j == 0:
                k = _mm_nn("k_proj", hk, gw["wk"], 0, "row", [BF16], tm_big, tn=D)[0]
                v = _mm_nn("v_proj", hk, gw["wv"], 0, "row", [BF16], tm_big, tn=D)[0]
            s["q"] = _mm_nn(f"q_proj{j}", n1, gw["wq"], j, "row", [BF16], tm_big, tn=D,
                            epi=lambda acc: (acc * ATTN_SCALE,))[0]
            s["o"], s["ob"], s["aw"], s["bw"], s["fl"] = hosting(_attn_fwd, f"attn{j}", s["q"], k, v)
            mix = _mm_nn(f"o_proj{j}", s["ob"], gw["wo"], j, "row", [F32], tm_big, tn=D)[0]
        s["mix"] = mix
        s["h1"], s["n3"] = _res_norm(f"mix_norm{layer}", h, mix, gain(layer, 1), [gain(layer, 2)])
        relu2 = lambda acc: (jnp.square(jnp.maximum(acc, 0.0)),)
        s["act"] = hosting(_mm_nn, f"mlp_up{layer}", s["n3"], gw["w1"], layer, "col", [BF16], tm_big, epi=relu2)[0]
        s["ff"] = hosting(_mm_nn, f"mlp_down{layer}", s["act"], gw["w2"], layer, "row", [F32], tm_big, tn=D // 2)[0]
        pre = [] if layer == DEPTH - 1 else [gain(layer + 1, 0)] + ([kvn] if layer == N_A - 1 else [])
        outs = _res_norm(f"mlp_norm{layer}", s["h1"], s["ff"], gain(layer, 3), pre)
        h = outs[0]
        if pre:
            n1 = outs[1]
        if layer == N_A - 1:
            hk = outs[2]
        saved.append(s)

    sq, dh = _loss_and_grad("loss", h, target)
    loss = lax.psum(0.5 * sq[0, 0] / D, ("x", "y", "c"))

    g_gain = [[None] * 4 for _ in range(DEPTH)]
    g_taps = [None] * N_A
    g_w = {n: None for n in ("win", "wout", "wq", "wo", "w1a", "w1b", "w2a", "w2b")}
    flat = lambda a, lead: a.reshape(lead + (-1, a.shape[-1]))
    early_names = ["w_o", "mlp_w1b", "mlp_w2b"]
    early_g = early_recv = None
    dk_parts, dv_parts = [], []
    g_kvn = None
    s = saved[DEPTH - 1]
    dh, dff, g_gain[DEPTH - 1][3] = _norm_bwd("bwd_top", dh, post=(s["ff"], gain(DEPTH - 1, 3)))
    for layer in reversed(range(DEPTH)):
        s = saved[layer]
        def relu_grad(acc, act):
            act = act.astype(F32)
            return (acc * (2.0 * act * lax.rsqrt(jnp.maximum(act, 1e-30))),)

        da1 = _mm_nt(f"mlp_down_bwd{layer}", dff, gw["w2"], layer, "row", [BF16], tm_big, epi=relu_grad,
                     extras=(s["act"],))[0]
        half = "b" if layer >= DEPTH // 2 else "a"
        g_w["w2" + half] = _mm_tn(f"mlp_w2_grad{layer}", s["act"], dff, "row", tm_big, group=2,
                                  layer=layer % 2, layers=2, into=g_w["w2" + half], wire=True)
        g_w["w1" + half] = _mm_tn(f"mlp_w1_grad{layer}", s["n3"], da1, "col", tm_big, group=4,
                                  layer=layer % 2, layers=2, into=g_w["w1" + half], wire=True)
        dn3 = _mm_nt(f"mlp_up_bwd{layer}", da1, gw["w1"], layer, "col", [F32], tm_mid, tn=D)[0]
        dh, dmix, g_gain[layer][2], g_gain[layer][1] = _norm_bwd(
            f"bwd_mid{layer}", dh, stream=s["h1"], pre=[(gain(layer, 2), dn3)], post=(s["mix"], gain(layer, 1)))
        pre = []
        if layer < N_A:
            dy = _mm_nt(f"conv_out_bwd{layer}", dmix, gw["wout"], layer, "row", [F32], tm_big, group=4)[0]
            g_w["wout"] = _mm_tn(f"conv_out_grad{layer}", s["y"], dmix, "row", tm_big, group=4, layer=layer,
                                 layers=N_A, into=g_w["wout"], wire=True)
            dp, g_taps[layer] = _conv_bwd(f"conv_bwd{layer}", s["p"], dy, taps[layer])
            g_w["win"] = _mm_tn(f"conv_in_grad{layer}", s["n1"], dp, "col", tm_big, group=4, layer=layer, layers=N_A,
                                into=g_w["win"], wire=True)
            dn1 = _mm_nt(f"conv_in_bwd{layer}", dp, gw["win"], layer, "col", [F32], tm_mid, tn=D)[0]
        else:
            j = layer - N_A
            do = _mm_nt(f"o_proj_bwd{j}", dmix, gw["wo"], j, "row", [BF16], tm_big, group=4)[0]
            g_w["wo"] = _mm_tn(f"o_proj_grad{j}", s["ob"], dmix, "row", tm_big, group=4, layer=j,
                               layers=DEPTH - N_A, into=g_w["wo"], wire=True)
            if j == 0:
                early_g = [flat(g_w[nm], (N_DEV,)) for nm in ("wo", "w1b", "w2b")]
                got = _attn_bwd(f"attn_bwd{j}", s["q"], k, v, s["o"], do, s["aw"], s["bw"], s["fl"],
                                side=_Direct(early_g))
                early_recv = got[3:]
            else:
                got = _attn_bwd(f"attn_bwd{j}", s["q"], k, v, s["o"], do, s["aw"], s["bw"], s["fl"])
            dq, dk_j, dv_j = got[:3]
            dk_parts.append(dk_j)
            dv_parts.append(dv_j)
            g_w["wq"] = _mm_tn(f"q_proj_grad{j}", s["n1"], dq, "row", tm_big, group=4, layer=j,
                               layers=DEPTH - N_A, into=g_w["wq"], wire=True)
            dn1 = _mm_nt(f"q_proj_bwd{j}", dq, gw["wq"], j, "row", [F32], tm_big, group=4)[0]
            if j == 0:
                dkb = _add_cast("dk_sum", dk_parts[0], dk_parts[1])
                dvb = _add_cast("dv_sum", dv_parts[0], dv_parts[1])
                g_wk = _mm_tn("k_proj_grad", hk, dkb, "row", tm_big, group=4, wire=True)
                g_wv = _mm_tn("v_proj_grad", hk, dvb, "row", tm_big, group=4, wire=True)
                dhk_k = _mm_nt("k_proj_bwd", dkb, gw["wk"], 0, "row", [F32], tm_big, group=4)[0]
                dhk = _mm_nt("v_proj_bwd", dvb, gw["wv"], 0, "row", [F32], tm_big, group=4,
                             epi=lambda acc, other: (acc + other,), extras=(dhk_k,))[0]
                pre = [(kvn, dhk)]
        pre = [(gain(layer, 0), dn1)] + pre
        if layer > 0:
            sp = saved[layer - 1]
            outs = _norm_bwd(f"bwd_in{layer}", dh, stream=s["h0"], pre=pre, post=(sp["ff"], gain(layer - 1, 3)))
            dh, dff = outs[0], outs[1]
            g_gain[layer][0] = outs[2]
            if len(pre) == 2:
                g_kvn = outs[3]
            g_gain[layer - 1][3] = outs[-1]
        else:
            dh, g_gain[0][0] = _norm_bwd("bwd_in0", dh, stream=s["h0"], pre=pre)

    grad_x = dh[PAD + N_META:][None]
    g_meta = dh[PAD:PAD + N_META]

    small_g = jnp.concatenate(
        [g_meta] + [g_gain[l][n][0:1] for l in range(DEPTH) for n in range(4)]
        + [g_taps[l][0:3] for l in range(N_A)] + [g_kvn[0:1], jnp.zeros((1, D), F32)], axis=0)
    small_sum = _sum_slots("small_grad_sum", _all_gather("gather_small_grads", _Gather([(small_g[None], 0, None)]))[0][:, 0])
    small_mine = lax.dynamic_slice_in_dim(small_sum, dev * dsh, dsh, axis=1)
    small_m = pack_small(m_meta_tokens, m_norm_gains, m_conv_w)
    small_v = pack_small(v_meta_tokens, v_norm_gains, v_conv_w)
    small_d, small_mn, small_vn = _adam_small("adam_small", small_w, small_mine, small_m, small_v)
    pad8 = lambda a: jnp.concatenate([a[None], jnp.zeros((7, D), F32)], axis=0)
    g_kv = small_sum[38]
    kv_d, kv_mn, kv_vn = _adam_small("adam_kv_norm", pad8(kv_norm), pad8(g_kv), pad8(m_kv_norm), pad8(v_kv_norm))

    def unpack_small(a):
        return (a[0:N_META], a[N_META:N_META + 16].reshape(DEPTH, 4, dsh), a[32:38].reshape(N_A, 3, dsh))

    hi, lo = slice(DEPTH // 2, DEPTH), slice(0, DEPTH // 2)
    state = lambda ws, ms, vs, part: tuple(flat(t[part], ()) for t in (ws, ms, vs))
    shards = {"conv_in": (conv_in_proj, m_conv_in_proj, v_conv_in_proj), "conv_out": (conv_out_proj, m_conv_out_proj, v_conv_out_proj),
              "w_k": (w_k, m_w_k, v_w_k), "w_v": (w_v, m_w_v, v_w_v), "w_q": (w_q, m_w_q, v_w_q), "w_o": (w_o, m_w_o, v_w_o)}
    shards = {nm: state(*t, slice(None)) for nm, t in shards.items()}
    for nm, t in (("mlp_w1", (mlp_w1, m_mlp_w1, v_mlp_w1)), ("mlp_w2", (mlp_w2, m_mlp_w2, v_mlp_w2))):
        shards[nm + "a"], shards[nm + "b"] = state(*t, lo), state(*t, hi)
    done = {}
    order = jnp.stack([dev ^ r for r in range(N_DEV)]).astype(jnp.int32)
    for nm, g, recv in zip(early_names, early_g, early_recv):
        done[nm] = _sum8_adam(f"adam_{nm}", g, recv, order, *shards[nm])
    late_names = ["conv_in", "conv_out", "w_k", "w_v", "w_q", "mlp_w1a", "mlp_w2a"]
    late_g = [flat(g, (N_DEV,)) for g in (g_w["win"], g_w["wout"], g_wk, g_wv, g_w["wq"], g_w["w1a"], g_w["w2a"])]
    from_sibling = _exchange_sibling("grads_to_sibling", late_g)
    c_idx = jnp.reshape(pc, (1,)).astype(jnp.int32)
    chip_idx = jnp.reshape(2 * px + py, (1,)).astype(jnp.int32)
    parts = [_pair_sum(f"pair_sum_{nm}", g, r, c_idx) for nm, g, r in zip(late_names, late_g, from_sibling)]
    from_chips = _exchange_chips("grads_to_chips", parts)
    for nm, part, recv in zip(late_names, parts, from_chips):
        done[nm] = _final_sum_adam(f"adam_{nm}", part, recv, chip_idx, *shards[nm])

    def assemble(kind, small_parts, kv_part):
        meta_p, gains_p, taps_p = small_parts
        whole = lambda nm, like: done[nm][kind].reshape(like.shape)
        halves = lambda nm, like: jnp.concatenate([done[nm + "a"][kind], done[nm + "b"][kind]], axis=0).reshape(like.shape)
        return [meta_p, gains_p, whole("conv_in", conv_in_proj), taps_p, whole("conv_out", conv_out_proj), kv_part,
                whole("w_k", w_k), whole("w_v", w_v), whole("w_q", w_q), whole("w_o", w_o),
                halves("mlp_w1", mlp_w1), halves("mlp_w2", mlp_w2)]

    grads = assemble(0, unpack_small(small_mine), g_kv)
    deltas = assemble(1, unpack_small(small_d), kv_d[0])
    new_m = assemble(2, unpack_small(small_mn), kv_mn[0])
    new_v = assemble(3, unpack_small(small_vn), kv_vn[0])
    return (loss, grad_x, *grads, *deltas, *new_m, *new_v)
```
